```python
import jax
import jax.numpy as jnp
from jax import lax
import numpy as np

D_MODEL = 2048
BATCH = 4
SEQ = 2048
DEPTH = 4

GRID_W = 64
CTX_LEN = 256
N_MIXERS = 4
NORM_EPS = 1e-6
POOL_WINDOWS = (2, 4, 8, 16)
POOL_GROUP = D_MODEL // len(POOL_WINDOWS)
NA_HEAD_DIM = 64
NA_HEADS = D_MODEL // NA_HEAD_DIM
NA_KH = 8
NA_KW = 16
ROPE_BASE = 10000.0
SG_CHUNK = 128
SG_WIDTH = D_MODEL
SG_GROUPS = 16
RW_HEAD_DIM = 64
RW_HEADS = D_MODEL // RW_HEAD_DIM
RW_DECAY_LORA = max(32, int(round(1.8 * D_MODEL ** 0.5 / 32)) * 32)
RW_ICLR_LORA = max(32, int(round(1.8 * D_MODEL ** 0.5 / 32)) * 32)
RW_GATE_LORA = max(32, int(round(0.6 * D_MODEL ** 0.8 / 32)) * 32)
RW_GN_EPS = 64e-5
D_FF = 5504

kernel_name = "hybrid_pool_natten_gmlp_rwkv7_dit"


def rms_norm(x, g):
    xf = x.astype(jnp.float32)
    y = xf * lax.rsqrt(jnp.mean(xf * xf, axis=-1, keepdims=True) + NORM_EPS)
    return (y * g.astype(jnp.float32)).astype(x.dtype)


def modulate(x, shift, scale):
    return x * (1 + scale) + shift


def conv_ffn(u, w_gate, w_up, conv_w, conv_b, w_down):
    T = u.shape[1]
    gp = jnp.pad(u @ w_gate, ((0, 0), (1, 1), (0, 0)))
    gte = gp[:, :T] * conv_w[0] + gp[:, 1:T + 1] * conv_w[1] + gp[:, 2:] * conv_w[2] + conv_b
    return (jax.nn.silu(gte) * (u @ w_up)) @ w_down


def pool_mix(u, w, b, scale):
    B, T, D = u.shape
    uf = u.astype(jnp.float32)
    t = jnp.arange(T)
    parts = []
    for g, win in enumerate(POOL_WINDOWS):
        left, right = win // 2, win - 1 - win // 2
        ug = uf[..., g * POOL_GROUP:(g + 1) * POOL_GROUP]
        cs = jnp.cumsum(jnp.pad(ug, ((0, 0), (left + 1, right), (0, 0))), axis=1)
        window_sum = cs[:, win:win + T] - cs[:, :T]
        count = (jnp.minimum(t + right, T - 1) - jnp.maximum(t - left, 0) + 1).astype(jnp.float32)
        parts.append(window_sum / count[None, :, None] - ug)
    p = jnp.stack(parts, axis=2).astype(u.dtype)
    y = jnp.einsum('btgc,gcd->btgd', p, w).reshape(B, T, D) + b
    return y * scale


def axial_rope(t):
    T, dh = t.shape[1], t.shape[-1]
    half = dh // 2
    pos = jnp.arange(T)
    inv_freq = ROPE_BASE ** (-jnp.arange(0, half, 2, dtype=jnp.float32) / half)

    def rotate(u, p):
        ang = p.astype(jnp.float32)[:, None] * inv_freq
        cos, sin = jnp.cos(ang)[:, None], jnp.sin(ang)[:, None]
        u1, u2 = jnp.split(u, 2, axis=-1)
        return jnp.concatenate([u1 * cos - u2 * sin, u1 * sin + u2 * cos], axis=-1)

    tf = t.astype(jnp.float32)
    out = jnp.concatenate([rotate(tf[..., :half], pos // GRID_W),
                           rotate(tf[..., half:], pos % GRID_W)], axis=-1)
    return out.astype(t.dtype)


def neighbourhood_attention(a_ctx, a_lat, w_qkv, rpb, w_o, need_ctx_out):
    B, T, D = a_lat.shape
    L = a_ctx.shape[1]
    H, dh = NA_HEADS, NA_HEAD_DIM
    rows = T // GRID_W
    kh, kw = min(NA_KH, rows), NA_KW
    scale = dh ** -0.5
    f32 = jnp.float32
    qkv = (a_lat @ w_qkv).reshape(B, T, 3, H, dh)
    q, k, v = axial_rope(qkv[:, :, 0]), axial_rope(qkv[:, :, 1]), qkv[:, :, 2]
    kvc = (a_ctx @ w_qkv[:, D:]).reshape(B, L, 2, H, dh)
    kc, vc = kvc[:, :, 0], kvc[:, :, 1]
    y_ctx = None
    if need_ctx_out:
        qc = (a_ctx @ w_qkv[:, :D]).reshape(B, L, H, dh)
        s = jnp.einsum('bqhd,bkhd->bhqk', qc, kc, preferred_element_type=f32) * scale
        p = jax.nn.softmax(s, axis=-1).astype(vc.dtype)
        y_ctx = jnp.einsum('bhqk,bkhd->bqhd', p, vc).reshape(B, L, D) @ w_o
    q_rows = q.reshape(B, rows, GRID_W, H, dh).transpose(1, 0, 3, 2, 4)
    to_grid = lambda z: z.reshape(B, rows, GRID_W, H, dh).transpose(0, 3, 1, 2, 4)
    k_grid, v_grid = to_grid(k), to_grid(v)
    col = jnp.arange(GRID_W)
    col_start = jnp.clip(col - kw // 2, 0, GRID_W - kw)
    col_ok = (col[None, :] >= col_start[:, None]) & (col[None, :] < col_start[:, None] + kw)
    band_ok = jnp.tile(col_ok, (1, kh))
    dc = jnp.clip(col[None, :] - col[:, None], -(kw - 1), kw - 1) + NA_KW - 1
    rpb_cols = rpb[:, :, dc]
    n_loc = kh * GRID_W

    def row_block(args):
        r, q_r = args
        rs = jnp.clip(r - kh // 2, 0, rows - kh)
        k_band = lax.dynamic_slice_in_dim(k_grid, rs, kh, axis=2).reshape(B, H, n_loc, dh)
        v_band = lax.dynamic_slice_in_dim(v_grid, rs, kh, axis=2).reshape(B, H, n_loc, dh)
        dr = rs + jnp.arange(kh) - r + NA_KH - 1
        bias = rpb_cols[:, dr].transpose(0, 2, 1, 3).reshape(H, GRID_W, n_loc)
        s_loc = jnp.einsum('bhqd,bhkd->bhqk', q_r, k_band, preferred_element_type=f32) * scale + bias
        s_loc = jnp.where(band_ok, s_loc, -jnp.inf)
        s_ctx = jnp.einsum('bhqd,bkhd->bhqk', q_r, kc, preferred_element_type=f32) * scale
        p = jax.nn.softmax(jnp.concatenate([s_loc, s_ctx], axis=-1), axis=-1).astype(v_band.dtype)
        return (jnp.einsum('bhqk,bhkd->bhqd', p[..., :n_loc], v_band)
                + jnp.einsum('bhqk,bkhd->bhqd', p[..., n_loc:], vc))

    o = lax.map(row_block, (jnp.arange(rows), q_rows))
    y_lat = o.transpose(1, 0, 3, 2, 4).reshape(B, T, D) @ w_o
    return y_ctx, y_lat


def spatial_gating_mix(u, w_in, b_in, norm_g, w_s, b_s, w_o):
    B, T, _ = u.shape
    z = jax.nn.gelu(u @ w_in + b_in)
    zu, zv = jnp.split(z, 2, axis=-1)
    zv = rms_norm(zv, norm_g).reshape(B, T // SG_CHUNK, SG_CHUNK, SG_GROUPS, SG_WIDTH // SG_GROUPS)
    zv = jnp.einsum('gpq,bnqgc->bnpgc', w_s, zv) + b_s.T[:, :, None]
    return (zu * zv.reshape(B, T, SG_WIDTH)) @ w_o


def rwkv7_project(u, mu, w_rkv, w0, w1, w2, a0, a1, a2, g1, g2, k_k, k_a, need_out):
    B, T, D = u.shape
    f32 = jnp.float32
    heads = lambda z: z.reshape(z.shape[:-1] + (RW_HEADS, RW_HEAD_DIM)).astype(f32)
    prev = jnp.pad(u, ((0, 0), (1, 0), (0, 0)))[:, :T] - u
    nxt = jnp.pad(u, ((0, 0), (0, 1), (0, 0)))[:, 1:] - u
    shifted = lambda n: u + prev * mu[0, n] + nxt * mu[1, n]
    xw, xk, xv, xa = shifted(1), shifted(2), shifted(3), shifted(4)
    k = heads(xk @ w_rkv[1])
    v = heads(xv @ w_rkv[2])
    kk = k * heads(k_k)
    kk = kk * lax.rsqrt(jnp.maximum(jnp.sum(kk * kk, axis=-1, keepdims=True), 1e-12))
    ka = heads(k_a)
    decay, key, iclr = [], [], []
    for d in range(2):
        logw = -jax.nn.softplus(-(w0[d] + jnp.tanh(xw @ w1[d]) @ w2[d])) - 0.5
        decay.append(jnp.exp(-jnp.exp(heads(logw))))
        a = jax.nn.sigmoid(heads(a0[d] + (xa @ a1[d]) @ a2[d]))
        iclr.append(a)
        key.append(k * (1.0 + (a - 1.0) * ka))
    r = g = None
    if need_out:
        r = heads(shifted(0) @ w_rkv[0])
        g = jax.nn.sigmoid(shifted(5) @ g1) @ g2
    return dict(r=r, g=g, v=v, kk=kk, k=key, w=decay, a=iclr)


def wkv_scan(state0, w, k, v, kk, a, r, reverse):
    tm = lambda z: jnp.moveaxis(z, 1, 0)
    xs = (tm(w), tm(k), tm(v), tm(kk), tm(a)) + ((tm(r),) if r is not None else ())

    def step(S, inp):
        w_t, k_t, v_t, kk_t, a_t = inp[:5]
        S = (S * w_t[:, :, None, :]
             - jnp.einsum('bhvk,bhk->bhv', S, kk_t)[..., None] * (kk_t * a_t)[:, :, None, :]
             + v_t[..., None] * k_t[:, :, None, :])
        y = jnp.einsum('bhvk,bhk->bhv', S, inp[5]) if len(inp) == 6 else None
        return S, y

    S, ys = lax.scan(step, state0, xs, reverse=reverse)
    return S, (None if r is None else jnp.moveaxis(ys, 0, 1))


def rwkv7_readout(p, ys, r_k, ln_g, ln_b, w_o, dtype):
    f32 = jnp.float32
    hs = (RW_HEADS, RW_HEAD_DIM)
    y = ys[0] + ys[1]
    mean = jnp.mean(y, axis=-1, keepdims=True)
    var = jnp.mean(jnp.square(y - mean), axis=-1, keepdims=True)
    y = (y - mean) * lax.rsqrt(var + RW_GN_EPS)
    y = y * ln_g.reshape(hs).astype(f32) + ln_b.reshape(hs).astype(f32)
    rk = r_k.astype(f32)
    bonus = (jnp.sum(p['r'] * p['k'][0] * rk, axis=-1, keepdims=True)
             + jnp.sum(p['r'] * p['k'][1] * rk, axis=-1, keepdims=True)) * p['v']
    out = (y + bonus).reshape(y.shape[:2] + (-1,)).astype(dtype) * p['g']
    return out @ w_o


def rwkv7_mix(a_ctx, a_lat, mu, w_rkv, w0, w1, w2, a0, a1, a2, g1, g2, k_k, k_a, r_k,
              ln_g, ln_b, w_o, need_ctx_out):
    proj = lambda u, need: rwkv7_project(u, mu, w_rkv, w0, w1, w2, a0, a1, a2, g1, g2, k_k, k_a, need)
    pc, pl = proj(a_ctx, need_ctx_out), proj(a_lat, True)
    B = a_lat.shape[0]
    s0 = jnp.zeros((B, RW_HEADS, RW_HEAD_DIM, RW_HEAD_DIM), jnp.float32)
    yc_dirs, yl_dirs = [], []
    for d in range(2):
        rev = d == 1
        s_ctx, yc = wkv_scan(s0, pc['w'][d], pc['k'][d], pc['v'], pc['kk'], pc['a'][d], pc['r'], rev)
        _, yl = wkv_scan(s_ctx, pl['w'][d], pl['k'][d], pl['v'], pl['kk'], pl['a'][d], pl['r'], rev)
        yc_dirs.append(yc)
        yl_dirs.append(yl)
    y_lat = rwkv7_readout(pl, yl_dirs, r_k, ln_g, ln_b, w_o, a_lat.dtype)
    y_ctx = rwkv7_readout(pc, yc_dirs, r_k, ln_g, ln_b, w_o, a_ctx.dtype) if need_ctx_out else None
    return y_ctx, y_lat


def setup_inputs(seed: int = 0) -> dict:
    key = jax.random.key(seed)
    keys = iter(jax.random.split(key, 64))
    f32 = jnp.float32
    D, F = D_MODEL, D_FF
    nrm = lambda shape, s: jax.random.normal(next(keys), shape, f32) * s
    gain = lambda shape: 1.0 + nrm(shape, 0.02)
    nA, nB, nC, nD = [len(range(m, DEPTH, N_MIXERS)) for m in range(N_MIXERS)]
    inp = {}
    inp['x'] = nrm((BATCH, SEQ, D), 1.0)
    inp['c'] = nrm((BATCH, D), 1.0)
    inp['ctx'] = nrm((BATCH, CTX_LEN, D), 1.0)
    inp['c_ctx'] = nrm((D,), 1.0)
    inp['norm1_g'] = gain((DEPTH, D))
    inp['norm2_g'] = gain((DEPTH, D))
    inp['w_mod'] = nrm((DEPTH, D, 6 * D), 0.5 * D ** -0.5)
    inp['b_mod'] = nrm((DEPTH, 6 * D), 0.02)
    inp['ffn_w_gate'] = nrm((DEPTH, D, F), D ** -0.5)
    inp['ffn_w_up'] = nrm((DEPTH, D, F), D ** -0.5)
    inp['ffn_conv_w'] = nrm((DEPTH, 3, F), 3 ** -0.5)
    inp['ffn_conv_b'] = nrm((DEPTH, F), 0.02)
    inp['ffn_w_down'] = nrm((DEPTH, F, D), F ** -0.5)
    inp['final_norm_g'] = gain((D,))
    inp['pool_w'] = nrm((nA, len(POOL_WINDOWS), POOL_GROUP, POOL_GROUP), POOL_GROUP ** -0.5)
    inp['pool_b'] = nrm((nA, D), 0.02)
    inp['pool_scale'] = gain((nA, D))
    inp['na_w_qkv'] = nrm((nB, D, 3 * D), D ** -0.5)
    inp['na_rpb'] = nrm((nB, NA_HEADS, 2 * NA_KH - 1, 2 * NA_KW - 1), 0.1)
    inp['na_w_o'] = nrm((nB, D, D), D ** -0.5)
    inp['sg_w_in'] = nrm((nC, D, 2 * SG_WIDTH), D ** -0.5)
    inp['sg_b_in'] = nrm((nC, 2 * SG_WIDTH), 0.02)
    inp['sg_norm_g'] = gain((nC, SG_WIDTH))
    inp['sg_w_s'] = nrm((nC, SG_GROUPS, SG_CHUNK, SG_CHUNK), SG_CHUNK ** -0.5)
    inp['sg_b_s'] = gain((nC, SG_GROUPS, SG_CHUNK))
    inp['sg_w_o'] = nrm((nC, SG_WIDTH, D), SG_WIDTH ** -0.5)
    inp['rw_mu'] = jax.random.uniform(next(keys), (nD, 2, 6, D), f32, 0.0, 0.5)
    inp['rw_w_rkv'] = nrm((nD, 3, D, D), D ** -0.5)
    inp['rw_w0'] = jax.random.uniform(next(keys), (nD, 2, D), f32, -6.0, 0.0)
    inp['rw_w1'] = nrm((nD, 2, D, RW_DECAY_LORA), D ** -0.5)
    inp['rw_w2'] = nrm((nD, 2, RW_DECAY_LORA, D), 0.5 * RW_DECAY_LORA ** -0.5)
    inp['rw_a0'] = nrm((nD, 2, D), 0.1)
    inp['rw_a1'] = nrm((nD, 2, D, RW_ICLR_LORA), D ** -0.5)
    inp['rw_a2'] = nrm((nD, 2, RW_ICLR_LORA, D), 0.5 * RW_ICLR_LORA ** -0.5)
    inp['rw_g1'] = nrm((nD, D, RW_GATE_LORA), D ** -0.5)
    inp['rw_g2'] = nrm((nD, RW_GATE_LORA, D), RW_GATE_LORA ** -0.5)
    inp['rw_k_k'] = 0.85 + nrm((nD, D), 0.02)
    inp['rw_k_a'] = gain((nD, D))
    inp['rw_r_k'] = nrm((nD, RW_HEADS, RW_HEAD_DIM), 0.1)
    inp['rw_ln_g'] = gain((nD, D))
    inp['rw_ln_b'] = nrm((nD, D), 0.02)
    inp['rw_w_o'] = nrm((nD, D, D), D ** -0.5)
    return inp


def reference(x, c, ctx, c_ctx, norm1_g, norm2_g, w_mod, b_mod, ffn_w_gate, ffn_w_up, ffn_conv_w,
              ffn_conv_b, ffn_w_down, final_norm_g, pool_w, pool_b, pool_scale, na_w_qkv, na_rpb,
              na_w_o, sg_w_in, sg_b_in, sg_norm_g, sg_w_s, sg_b_s, sg_w_o, rw_mu, rw_w_rkv, rw_w0,
              rw_w1, rw_w2, rw_a0, rw_a1, rw_a2, rw_g1, rw_g2, rw_k_k, rw_k_a, rw_r_k, rw_ln_g,
              rw_ln_b, rw_w_o):
    B, T, D = x.shape
    h, hc = x, ctx
    silu_c, silu_cc = jax.nn.silu(c), jax.nn.silu(c_ctx)
    for i in range(DEPTH):
        m, j = i % N_MIXERS, i // N_MIXERS
        last = i == DEPTH - 1
        ml = (silu_c @ w_mod[i] + b_mod[i]).reshape(B, 6, 1, D)
        mc = (silu_cc @ w_mod[i] + b_mod[i]).reshape(6, 1, 1, D)
        a_lat = modulate(rms_norm(h, norm1_g[i]), ml[:, 0], ml[:, 1])
        ctx_read = (not last) or m in (1, 3)
        a_ctx = modulate(rms_norm(hc, norm1_g[i]), mc[0], mc[1]) if ctx_read else None
        if m == 0:
            y_lat = pool_mix(a_lat, pool_w[j], pool_b[j], pool_scale[j])
            y_ctx = None if last else pool_mix(a_ctx, pool_w[j], pool_b[j], pool_scale[j])
        elif m == 1:
            y_ctx, y_lat = neighbourhood_attention(a_ctx, a_lat, na_w_qkv[j], na_rpb[j], na_w_o[j], not last)
        elif m == 2:
            y_lat = spatial_gating_mix(a_lat, sg_w_in[j], sg_b_in[j], sg_norm_g[j], sg_w_s[j], sg_b_s[j], sg_w_o[j])
            y_ctx = None if last else spatial_gating_mix(a_ctx, sg_w_in[j], sg_b_in[j], sg_norm_g[j],
                                                         sg_w_s[j], sg_b_s[j], sg_w_o[j])
        else:
            y_ctx, y_lat = rwkv7_mix(a_ctx, a_lat, rw_mu[j], rw_w_rkv[j], rw_w0[j], rw_w1[j], rw_w2[j],
                                     rw_a0[j], rw_a1[j], rw_a2[j], rw_g1[j], rw_g2[j], rw_k_k[j],
                                     rw_k_a[j], rw_r_k[j], rw_ln_g[j], rw_ln_b[j], rw_w_o[j], not last)
        h = h + ml[:, 2] * y_lat
        h = h + ml[:, 5] * conv_ffn(modulate(rms_norm(h, norm2_g[i]), ml[:, 3], ml[:, 4]),
                                    ffn_w_gate[i], ffn_w_up[i], ffn_conv_w[i], ffn_conv_b[i], ffn_w_down[i])
        if not last:
            hc = hc + mc[2] * y_ctx
            hc = hc + mc[5] * conv_ffn(modulate(rms_norm(hc, norm2_g[i]), mc[3], mc[4]),
                                       ffn_w_gate[i], ffn_w_up[i], ffn_conv_w[i], ffn_conv_b[i], ffn_w_down[i])
    return rms_norm(h, final_norm_g)
```

```python
import functools
import math

import jax
import jax.numpy as jnp
from jax import lax
from jax.experimental import pallas as pl
from jax.experimental.pallas import tpu as pltpu

F32 = jnp.float32
BF16 = jnp.bfloat16

GRID_W = 64
NORM_EPS = 1e-6
POOL_WINDOWS = (2, 4, 8, 16)
NA_HEAD_DIM = 64
NA_KH = 8
NA_KW = 16
ROPE_BASE = 10000.0
SG_CHUNK = 128
SG_GROUPS = 16
RW_HEAD_DIM = 64
RW_GN_EPS = 64e-5

LANES = 128
SUBLANES = 8
VMEM_LIMIT = 56 * 1024 * 1024

SCAN_CHUNK = 64
SOLVE_BLOCK = 16


def _cparams(*sem):
    return pltpu.CompilerParams(dimension_semantics=sem, vmem_limit_bytes=VMEM_LIMIT)


def _div(a, b):
    assert a % b == 0, (a, b)
    return a // b


def _nt(a, b):
    return lax.dot_general(a, b, (((1,), (1,)), ((), ())), preferred_element_type=F32)


def _mm(a, b):
    return jnp.dot(a.astype(BF16), b.astype(BF16), preferred_element_type=F32)


def _silu(x):
    return x * jax.nn.sigmoid(x)


def _head_sum(x, lane_is_head0):
    s0 = jnp.sum(jnp.where(lane_is_head0, x, 0.0), axis=-1, keepdims=True)
    s1 = jnp.sum(jnp.where(lane_is_head0, 0.0, x), axis=-1, keepdims=True)
    return jnp.where(lane_is_head0, s0, s1)


def _seq_edges(i, tm, n_lat_rows, seq_lat, seq_ctx):
    row = lax.broadcasted_iota(jnp.int32, (tm, 1), 0)
    grow = i * tm + row
    slen = jnp.where(grow < n_lat_rows, seq_lat, seq_ctx)
    pos = grow & (slen - 1)
    return row, pos == 0, pos == slen - 1


def _shift_rows(x, halo_prev, halo_next, row, first, last, tm):
    xp = pltpu.roll(x, 1, axis=0)
    xp = jnp.where(row == 0, halo_prev[SUBLANES - 1:SUBLANES], xp)
    xp = jnp.where(first, 0.0, xp)
    xn = pltpu.roll(x, tm - 1, axis=0)
    xn = jnp.where(row == tm - 1, halo_next[0:1], xn)
    xn = jnp.where(last, 0.0, xn)
    return xp, xn


def _mod_kernel(c_ref, w_ref, b_ref, o_ref):
    s = _silu(c_ref[...]).astype(BF16)
    o_ref[0] = jnp.dot(s, w_ref[0].astype(BF16), preferred_element_type=F32) + b_ref[0]


def _modulation(cvec, w_mod, b_mod):
    depth, d, n = w_mod.shape
    tn = 1024
    return pl.pallas_call(
        _mod_kernel,
        grid=(depth, _div(n, tn)),
        in_specs=[pl.BlockSpec((SUBLANES, d), lambda l, j: (0, 0)),
                  pl.BlockSpec((1, d, tn), lambda l, j: (l, 0, j)),
                  pl.BlockSpec((1, 1, tn), lambda l, j: (l, 0, j))],
        out_specs=pl.BlockSpec((1, SUBLANES, tn), lambda l, j: (l, 0, j)),
        out_shape=jax.ShapeDtypeStruct((depth, SUBLANES, n), F32),
        compiler_params=_cparams("arbitrary", "arbitrary"),
        name="modulation",
    )(cvec, w_mod, b_mod.reshape(depth, 1, n))


def _norm_kernel(*refs, modulate):
    if modulate:
        h_ref, g_ref, sh_ref, sc_ref, o_ref = refs
    else:
        h_ref, g_ref, o_ref = refs
    x = h_ref[...]
    y = x * lax.rsqrt(jnp.mean(x * x, axis=-1, keepdims=True) + NORM_EPS) * g_ref[...]
    if modulate:
        y = y * (1.0 + sc_ref[0]) + sh_ref[0]
    o_ref[...] = y.astype(o_ref.dtype)


def _norm(h, g, mod, k_shift, k_scale, seq, nseg, out_dtype, rows=None):
    m, d = h.shape if rows is None else (rows, h.shape[1])
    tm = 256
    row_spec = pl.BlockSpec((tm, d), lambda i: (i, 0))
    in_specs = [row_spec, pl.BlockSpec((1, d), lambda i: (0, 0))]
    args = [h, g.reshape(1, d)]
    if mod is not None:
        seg = lambda i: jnp.minimum((i * tm) // seq, nseg)
        in_specs += [pl.BlockSpec((1, 1, d), lambda i: (seg(i) * 6 + k_shift, 0, 0)),
                     pl.BlockSpec((1, 1, d), lambda i: (seg(i) * 6 + k_scale, 0, 0))]
        args += [mod, mod]
    return pl.pallas_call(
        functools.partial(_norm_kernel, modulate=mod is not None),
        grid=(_div(m, tm),),
        in_specs=in_specs,
        out_specs=row_spec,
        out_shape=jax.ShapeDtypeStruct((m, d), out_dtype),
        compiler_params=_cparams("arbitrary"),
        name="rmsnorm_mod",
    )(*args)


def _matmul_kernel(*refs, act, has_bias, has_res):
    x_ref, w_ref = refs[0], refs[1]
    k = 2
    acc = jnp.dot(x_ref[...], w_ref[...], preferred_element_type=F32)
    if has_bias:
        acc = acc + refs[k][...]
        k += 1
    if act == "gelu":
        acc = jax.nn.gelu(acc, approximate=True)
    elif act == "tanh":
        acc = jnp.tanh(acc)
    elif act == "sigmoid":
        acc = jax.nn.sigmoid(acc)
    if has_res:
        acc = refs[k][...] + refs[k + 1][0] * acc
        k += 2
    o_ref = refs[k]
    o_ref[...] = acc.astype(o_ref.dtype)


def _matmul(x, w, *, bias=None, act=None, res=None, mod=None, k_gate=None, seq=None, nseg=None,
            out_dtype=F32, tm=1024, tn=512, rows=None):
    m = x.shape[0] if rows is None else rows
    kdim, n = w.shape
    tn = min(tn, n)
    tm = min(tm, m)
    in_specs = [pl.BlockSpec((tm, kdim), lambda i, j: (i, 0)),
                pl.BlockSpec((kdim, tn), lambda i, j: (0, j))]
    args = [x, w]
    if bias is not None:
        in_specs.append(pl.BlockSpec((1, tn), lambda i, j: (0, j)))
        args.append(bias.reshape(1, n))
    if res is not None:
        seg = lambda i: jnp.minimum((i * tm) // seq, nseg)
        in_specs += [pl.BlockSpec((tm, tn), lambda i, j: (i, j)),
                     pl.BlockSpec((1, 1, tn), lambda i, j: (seg(i) * 6 + k_gate, 0, j))]
        args += [res, mod]
    return pl.pallas_call(
        functools.partial(_matmul_kernel, act=act, has_bias=bias is not None, has_res=res is not None),
        grid=(_div(m, tm), _div(n, tn)),
        in_specs=in_specs,
        out_specs=pl.BlockSpec((tm, tn), lambda i, j: (i, j)),
        out_shape=jax.ShapeDtypeStruct((m, n), out_dtype),
        compiler_params=_cparams("arbitrary", "arbitrary"),
        name="matmul",
    )(*args)


def _ffn1_kernel(x_ref, xp_ref, xn_ref, wg_ref, wu_ref, cw_ref, cb_ref, o_ref, *, tm, n_lat_rows,
                 seq_lat, seq_ctx):
    i = pl.program_id(0)
    x = x_ref[...]
    wg = wg_ref[...]
    g = jnp.dot(x, wg, preferred_element_type=F32)
    gp = jnp.dot(xp_ref[...], wg, preferred_element_type=F32)
    gn = jnp.dot(xn_ref[...], wg, preferred_element_type=F32)
    row, first, last = _seq_edges(i, tm, n_lat_rows, seq_lat, seq_ctx)
    g_prev, g_next = _shift_rows(g, gp, gn, row, first, last, tm)
    gte = g_prev * cw_ref[0:1] + g * cw_ref[1:2] + g_next * cw_ref[2:3] + cb_ref[...]
    up = jnp.dot(x, wu_ref[...], preferred_element_type=F32)
    o_ref[...] = (_silu(gte) * up).astype(o_ref.dtype)


def _halo_specs(tm, kdim, m):
    nb = tm // SUBLANES
    prev = pl.BlockSpec((SUBLANES, kdim), lambda i, j: (jnp.maximum(i * nb - 1, 0), 0))
    nxt = pl.BlockSpec((SUBLANES, kdim), lambda i, j: (jnp.minimum((i + 1) * nb, m // SUBLANES - 1), 0))
    return prev, nxt


def _ffn1(u, wg, wu, cw, cb, *, rows, n_lat_rows, seq_lat, seq_ctx):
    kdim, f = wg.shape
    tm, tn = 1024, 512
    prev, nxt = _halo_specs(tm, kdim, rows)
    return pl.pallas_call(
        functools.partial(_ffn1_kernel, tm=tm, n_lat_rows=n_lat_rows, seq_lat=seq_lat, seq_ctx=seq_ctx),
        grid=(_div(rows, tm), _div(f, tn)),
        in_specs=[pl.BlockSpec((tm, kdim), lambda i, j: (i, 0)), prev, nxt,
                  pl.BlockSpec((kdim, tn), lambda i, j: (0, j)),
                  pl.BlockSpec((kdim, tn), lambda i, j: (0, j)),
                  pl.BlockSpec((3, tn), lambda i, j: (0, j)),
                  pl.BlockSpec((1, tn), lambda i, j: (0, j))],
        out_specs=pl.BlockSpec((tm, tn), lambda i, j: (i, j)),
        out_shape=jax.ShapeDtypeStruct((rows, f), BF16),
        compiler_params=_cparams("arbitrary", "arbitrary"),
        name="ffn_gate_up",
    )(u, u, u, wg, wu, cw, cb)


def _pool_kernel(u_ref, w_ref, b_ref, s_ref, h_ref, gate_ref, o_ref, xp_ref, *, seq):
    grp = pl.program_id(1)
    pad = SUBLANES
    zeros = jnp.zeros((pad, u_ref.shape[1]), F32)
    xp_ref[0:pad] = zeros
    xp_ref[pad:pad + seq] = u_ref[...]
    xp_ref[pad + seq:2 * pad + seq] = zeros
    rt = min(seq, 256)
    for gi, win in enumerate(POOL_WINDOWS):
        @pl.when(grp == gi)
        def _(win=win):
            left, right = win // 2, win - 1 - win // 2
            for rc in range(seq // rt):
                base = pad + rc * rt
                t = rc * rt + lax.broadcasted_iota(jnp.int32, (rt, 1), 0)
                acc = xp_ref[base - left:base - left + rt]
                for j in range(1, win):
                    acc = acc + xp_ref[base - left + j:base - left + j + rt]
                cnt = (jnp.minimum(t + right, seq - 1) - jnp.maximum(t - left, 0) + 1).astype(F32)
                p = acc / cnt - xp_ref[base:base + rt]
                y = jnp.dot(p.astype(BF16), w_ref[0], preferred_element_type=F32) + b_ref[...]
                rows = slice(rc * rt, (rc + 1) * rt)
                o_ref[rows] = h_ref[rows] + gate_ref[0] * (y * s_ref[...])


def _pool_mix(u, w, b, scale, h, mod, *, seq, row0, nseq, seg0):
    d = u.shape[1]
    cg = d // len(POOL_WINDOWS)
    blk0 = row0 // seq
    tile = pl.BlockSpec((seq, cg), lambda s, g: (blk0 + s, g))
    vec = pl.BlockSpec((1, cg), lambda s, g: (0, g))
    return pl.pallas_call(
        functools.partial(_pool_kernel, seq=seq),
        grid=(nseq, len(POOL_WINDOWS)),
        in_specs=[tile, pl.BlockSpec((1, cg, cg), lambda s, g: (g, 0, 0)), vec, vec, tile,
                  pl.BlockSpec((1, 1, cg), lambda s, g: ((seg0(s)) * 6 + 2, 0, g))],
        out_specs=tile,
        out_shape=jax.ShapeDtypeStruct(h.shape, F32),
        scratch_shapes=[pltpu.VMEM((seq + 2 * SUBLANES, cg), F32)],
        input_output_aliases={4: 0},
        compiler_params=_cparams("arbitrary", "arbitrary"),
        name="pool_mix",
    )(u, w, b.reshape(1, d), scale.reshape(1, d), h, mod)


def _na_lat_kernel(q_ref, k_ref, v_ref, kc_ref, vc_ref, cos_ref, sin_ref, bias_ref, o_ref,
                   q_s, k_s, v_s, kc_s, vc_s, *, rows, kh):
    hd = NA_HEAD_DIM
    lane = lax.broadcasted_iota(jnp.int32, (1, LANES), 1)
    low = (lane % 32) < 16
    cos, sin = cos_ref[...], sin_ref[...]

    def rope(x):
        swapped = jnp.where(low, pltpu.roll(x, LANES - 16, axis=1), pltpu.roll(x, 16, axis=1))
        return x * cos + swapped * sin

    q = rope(q_ref[...]) * (hd ** -0.5)
    k = rope(k_ref[...])
    v = v_ref[...]
    kc = kc_ref[...]
    vc = vc_ref[...]
    for hh in range(2):
        ls = slice(hh * hd, (hh + 1) * hd)
        q_s[hh] = q[:, ls].astype(BF16)
        k_s[hh] = k[:, ls].astype(BF16)
        v_s[hh] = v[:, ls].astype(BF16)
        kc_s[hh] = kc[:, ls].astype(BF16)
        vc_s[hh] = vc[:, ls].astype(BF16)
    nloc = kh * GRID_W
    for hh in range(2):
        kch = kc_s[hh]
        vch = vc_s[hh]

        def body(r, carry, hh=hh, kch=kch, vch=vch):
            rs = jnp.clip(r - kh // 2, 0, rows - kh)
            qr = q_s[hh, pl.ds(pl.multiple_of(r * GRID_W, GRID_W), GRID_W), :]
            kb = k_s[hh, pl.ds(pl.multiple_of(rs * GRID_W, GRID_W), nloc), :]
            vb = v_s[hh, pl.ds(pl.multiple_of(rs * GRID_W, GRID_W), nloc), :]
            s_loc = _nt(qr, kb) + bias_ref[hh, r - rs]
            s_ctx = _nt(qr, kch)
            m = jnp.maximum(jnp.max(s_loc, axis=-1, keepdims=True), jnp.max(s_ctx, axis=-1, keepdims=True))
            p_loc = jnp.exp(s_loc - m)
            p_ctx = jnp.exp(s_ctx - m)
            den = jnp.sum(p_loc, axis=-1, keepdims=True) + jnp.sum(p_ctx, axis=-1, keepdims=True)
            o = (jnp.dot(p_loc.astype(BF16), vb, preferred_element_type=F32)
                 + jnp.dot(p_ctx.astype(BF16), vch, preferred_element_type=F32))
            o_ref[pl.ds(pl.multiple_of(r * GRID_W, GRID_W), GRID_W), hh * hd:(hh + 1) * hd] = (
                o / den).astype(o_ref.dtype)
            return carry

        lax.fori_loop(0, rows, body, 0)


def _na_ctx_kernel(q_ref, k_ref, v_ref, o_ref):
    hd = NA_HEAD_DIM
    q = q_ref[...] * (hd ** -0.5)
    k = k_ref[...]
    v = v_ref[...]
    for hh in range(2):
        ls = slice(hh * hd, (hh + 1) * hd)
        s = _nt(q[:, ls].astype(BF16), k[:, ls].astype(BF16))
        m = jnp.max(s, axis=-1, keepdims=True)
        p = jnp.exp(s - m)
        den = jnp.sum(p, axis=-1, keepdims=True)
        o = jnp.dot(p.astype(BF16), v[:, ls].astype(BF16), preferred_element_type=F32)
        o_ref[:, ls] = (o / den).astype(o_ref.dtype)


def _rope_tables(t_len, hd):
    half = hd // 2
    pos = jnp.arange(t_len)
    inv_freq = ROPE_BASE ** (-jnp.arange(0, half, 2, dtype=F32) / half)
    d = jnp.arange(hd)
    p = jnp.where(d[None, :] < half, (pos // GRID_W)[:, None], (pos % GRID_W)[:, None]).astype(F32)
    ang = p * inv_freq[d % (half // 2)][None, :]
    cos = jnp.cos(ang)
    sin = jnp.where((d % half) < half // 2, -jnp.sin(ang), jnp.sin(ang))
    return jnp.tile(cos, (1, 2)), jnp.tile(sin, (1, 2))


def _na_bias_table(rpb, kh):
    col = jnp.arange(GRID_W)
    col_start = jnp.clip(col - NA_KW // 2, 0, GRID_W - NA_KW)
    col_ok = (col[None, :] >= col_start[:, None]) & (col[None, :] < col_start[:, None] + NA_KW)
    dc = jnp.clip(col[None, :] - col[:, None], -(NA_KW - 1), NA_KW - 1) + NA_KW - 1
    rpb_cols = rpb[:, :, dc]
    dr = jnp.arange(kh)[None, :] - jnp.arange(kh)[:, None] + NA_KH - 1
    tab = rpb_cols[:, dr]
    tab = jnp.where(col_ok[None, None, None], tab, -1e30)
    heads = rpb.shape[0]
    return tab.transpose(0, 1, 3, 2, 4).reshape(heads, kh, GRID_W, kh * GRID_W)


def _neighbourhood_attention(qkv, rpb, *, bsz, t_len, l_len):
    d = qkv.shape[1] // 3
    hd = NA_HEAD_DIM
    heads = d // hd
    npair = d // LANES
    rows = t_len // GRID_W
    kh = min(NA_KH, rows)
    n_lat = bsz * t_len
    qkv_lat = qkv[:n_lat].reshape(bsz, t_len, 3 * d)
    qkv_ctx = qkv[n_lat:].reshape(bsz, l_len, 3 * d)
    cos, sin = _rope_tables(t_len, hd)
    bias = _na_bias_table(rpb, kh)
    lat_blk = lambda off: pl.BlockSpec((None, t_len, LANES), lambda b, p: (b, 0, off + p))
    ctx_blk = lambda off: pl.BlockSpec((None, l_len, LANES), lambda b, p: (b, 0, off + p))
    tab_blk = pl.BlockSpec((t_len, LANES), lambda b, p: (0, 0))
    o_lat = pl.pallas_call(
        functools.partial(_na_lat_kernel, rows=rows, kh=kh),
        grid=(bsz, npair),
        in_specs=[lat_blk(0), lat_blk(npair), lat_blk(2 * npair), ctx_blk(npair), ctx_blk(2 * npair),
                  tab_blk, tab_blk,
                  pl.BlockSpec((2, kh, GRID_W, kh * GRID_W), lambda b, p: (p, 0, 0, 0))],
        out_specs=pl.BlockSpec((None, t_len, LANES), lambda b, p: (b, 0, p)),
        out_shape=jax.ShapeDtypeStruct((bsz, t_len, d), BF16),
        scratch_shapes=[pltpu.VMEM((2, t_len, hd), BF16), pltpu.VMEM((2, t_len, hd), BF16),
                        pltpu.VMEM((2, t_len, hd), BF16), pltpu.VMEM((2, l_len, hd), BF16),
                        pltpu.VMEM((2, l_len, hd), BF16)],
        compiler_params=_cparams("arbitrary", "arbitrary"),
        name="na_latent",
    )(qkv_lat, qkv_lat, qkv_lat, qkv_ctx, qkv_ctx, cos, sin, bias)
    o_ctx = pl.pallas_call(
        _na_ctx_kernel,
        grid=(bsz, npair),
        in_specs=[ctx_blk(0), ctx_blk(npair), ctx_blk(2 * npair)],
        out_specs=pl.BlockSpec((None, l_len, LANES), lambda b, p: (b, 0, p)),
        out_shape=jax.ShapeDtypeStruct((bsz, l_len, d), BF16),
        compiler_params=_cparams("arbitrary", "arbitrary"),
        name="na_context",
    )(qkv_ctx, qkv_ctx, qkv_ctx)
    return jnp.concatenate([o_lat.reshape(n_lat, d), o_ctx.reshape(bsz * l_len, d)], axis=0)


def _sg_kernel(z_u_ref, z_v_ref, g_ref, ws_ref, bs_ref, o_ref, *, nchunk):
    zv = z_v_ref[...].astype(F32)
    zv = zv * lax.rsqrt(jnp.mean(zv * zv, axis=-1, keepdims=True) + NORM_EPS) * g_ref[...]
    zvb = zv.astype(BF16)
    for c in range(nchunk):
        rs = slice(c * SG_CHUNK, (c + 1) * SG_CHUNK)
        for g in range(SG_GROUPS):
            ls = slice(g * LANES, (g + 1) * LANES)
            mixed = jnp.dot(ws_ref[g], zvb[rs, ls], preferred_element_type=F32) + bs_ref[:, ls]
            o_ref[rs, ls] = (z_u_ref[rs, ls].astype(F32) * mixed).astype(o_ref.dtype)


def _spatial_gate(z, norm_g, w_s, b_s):
    m, two_w = z.shape
    width = two_w // 2
    nchunk = 2
    tm = nchunk * SG_CHUNK
    bs_full = jnp.repeat(b_s.T, width // SG_GROUPS, axis=1)
    return pl.pallas_call(
        functools.partial(_sg_kernel, nchunk=nchunk),
        grid=(_div(m, tm),),
        in_specs=[pl.BlockSpec((tm, width), lambda i: (i, 0)),
                  pl.BlockSpec((tm, width), lambda i: (i, 1)),
                  pl.BlockSpec((1, width), lambda i: (0, 0)),
                  pl.BlockSpec((SG_GROUPS, SG_CHUNK, SG_CHUNK), lambda i: (0, 0, 0)),
                  pl.BlockSpec((SG_CHUNK, width), lambda i: (0, 0))],
        out_specs=pl.BlockSpec((tm, width), lambda i: (i, 0)),
        out_shape=jax.ShapeDtypeStruct((m, width), BF16),
        compiler_params=_cparams("arbitrary"),
        name="spatial_gate",
    )(z, z, norm_g.reshape(1, width), w_s.astype(BF16), bs_full)


def _shift_kernel(u_ref, up_ref, un_ref, mu_ref, *o_refs, tm, n_lat_rows, seq_lat, seq_ctx):
    i = pl.program_id(0)
    u = u_ref[...]
    row, first, last = _seq_edges(i, tm, n_lat_rows, seq_lat, seq_ctx)
    u_prev, u_next = _shift_rows(u, up_ref[...], un_ref[...], row, first, last, tm)
    prev = u_prev - u
    nxt = u_next - u
    nmix = len(o_refs)
    for n in range(nmix):
        o_refs[n][...] = (u + prev * mu_ref[n:n + 1] + nxt * mu_ref[nmix + n:nmix + n + 1]).astype(BF16)


def _token_shift(u, mu, *, n_lat_rows, seq_lat, seq_ctx):
    m, d = u.shape
    nmix = mu.shape[1]
    tm, tn = 512, 512
    nb = tm // SUBLANES
    tile = pl.BlockSpec((tm, tn), lambda i, j: (i, j))
    prev = pl.BlockSpec((SUBLANES, tn), lambda i, j: (jnp.maximum(i * nb - 1, 0), j))
    nxt = pl.BlockSpec((SUBLANES, tn), lambda i, j: (jnp.minimum((i + 1) * nb, m // SUBLANES - 1), j))
    return pl.pallas_call(
        functools.partial(_shift_kernel, tm=tm, n_lat_rows=n_lat_rows, seq_lat=seq_lat, seq_ctx=seq_ctx),
        grid=(_div(m, tm), _div(d, tn)),
        in_specs=[tile, prev, nxt, pl.BlockSpec((2 * nmix, tn), lambda i, j: (0, j))],
        out_specs=[tile] * nmix,
        out_shape=[jax.ShapeDtypeStruct((m, d), BF16)] * nmix,
        compiler_params=_cparams("arbitrary", "arbitrary"),
        name="token_shift",
    )(u, u, u, mu.reshape(2 * nmix, d))


def _split3(x):
    hi = x.astype(BF16)
    r1 = x - hi.astype(F32)
    mid = r1.astype(BF16)
    lo = (r1 - mid.astype(F32)).astype(BF16)
    return hi, mid, lo


def _rw_prep_kernel(k_ref, r_ref, v_ref, hw_ref, ha_ref, w2_ref, a2_ref, w0_ref, a0_ref, kk_ref, ka_ref,
                    rk_ref, *o_refs, tm, lora):
    (v_o, bc_o, kt0, rt0, bh0, kh0, bt0, kbt0, gc0, kt1, rt1, bh1, kh1, bt1, kbt1, gc1) = o_refs
    per_dir = ((kt0, rt0, bh0, kh0, bt0, kbt0, gc0), (kt1, rt1, bh1, kh1, bt1, kbt1, gc1))
    lane = lax.broadcasted_iota(jnp.int32, (1, LANES), 1)
    head0 = lane < RW_HEAD_DIM
    k = k_ref[...]
    r = r_ref[...]
    v = v_ref[...]
    v_o[...] = v.astype(BF16)
    kkv = k * kk_ref[...]
    kk = kkv * lax.rsqrt(jnp.maximum(_head_sum(kkv * kkv, head0), 1e-12))
    ka = ka_ref[...]
    rk = rk_ref[...]
    hw = hw_ref[...]
    ha = ha_ref[...]
    ri = lax.broadcasted_iota(jnp.int32, (tm, tm), 0)
    ci = lax.broadcasted_iota(jnp.int32, (tm, tm), 1)
    same_chunk = (ri // SCAN_CHUNK) == (ci // SCAN_CHUNK)
    tri = (jnp.where(same_chunk & (ci <= ri), 1.0, 0.0).astype(BF16),
           jnp.where(same_chunk & (ci >= ri), 1.0, 0.0).astype(BF16))
    log_decay, beta, key = [], [], []
    for d in range(2):
        ls = slice(d * lora, (d + 1) * lora)
        pre = w0_ref[d] + jnp.dot(hw[:, ls], w2_ref[d], preferred_element_type=F32)
        z = -pre
        softplus = jnp.maximum(z, 0.0) + jnp.log(1.0 + jnp.exp(-jnp.abs(z)))
        log_decay.append(-jnp.exp(-softplus - 0.5))
        a = jax.nn.sigmoid(a0_ref[d] + jnp.dot(ha[:, ls], a2_ref[d], preferred_element_type=F32))
        beta.append(kk * a)
        key.append(k * (1.0 + (a - 1.0) * ka))
    bc_o[...] = _head_sum(r * key[0] * rk, head0) + _head_sum(r * key[1] * rk, head0)
    for d in range(2):
        ld = log_decay[d]
        pieces = _split3(ld)
        csum = [sum(jnp.dot(tri[e], p, preferred_element_type=F32) for p in pieces) for e in range(2)]
        cum = csum[d]
        total = csum[0] + csum[1] - ld
        kt, rt, bh, kh, bt, kbt, gc = per_dir[d]
        kt[...] = (kk * jnp.exp(cum - ld)).astype(BF16)
        rt[...] = (r * jnp.exp(cum)).astype(BF16)
        inv = jnp.exp(-cum)
        bh[...] = (beta[d] * inv).astype(BF16)
        kh[...] = (key[d] * inv).astype(BF16)
        tail = jnp.exp(total - cum)
        b_t = jnp.transpose(beta[d] * tail)
        k_t = jnp.transpose(key[d] * tail)
        g_t = jnp.transpose(jnp.exp(total))
        for c in range(tm // LANES):
            cs = slice(c * LANES, (c + 1) * LANES)
            bt[c] = b_t[:, cs].astype(BF16)
            kbt[c] = k_t[:, cs].astype(BF16)
            gc[c] = g_t[:, cs]


def _rw_prep(k, r, v, hw, ha, w2, a2, w0, a0, k_k, k_a, r_k, *, bsz, t_len, l_len, lora):
    m, d = k.shape
    tm = 256
    npair = d // LANES
    lat_tiles = bsz * t_len // tm
    per_b = t_len // tm
    tt = t_len + l_len
    nblk = tt // tm

    def out_blk(i):
        is_lat = i < lat_tiles
        b = jnp.where(is_lat, i // per_b, i - lat_tiles)
        blk = jnp.where(is_lat, l_len // tm + i % per_b, 0)
        return b, blk

    tile = pl.BlockSpec((tm, LANES), lambda i, p: (i, p))
    lora_blk = pl.BlockSpec((tm, 2 * lora), lambda i, p: (i, 0))
    wl_blk = pl.BlockSpec((2, lora, LANES), lambda i, p: (0, 0, p))
    v2_blk = pl.BlockSpec((2, 1, LANES), lambda i, p: (0, 0, p))
    v1_blk = pl.BlockSpec((1, LANES), lambda i, p: (0, p))
    row_o = pl.BlockSpec((None, tm, LANES), lambda i, p: (out_blk(i)[0], out_blk(i)[1], p))
    tr_o = pl.BlockSpec((None, None, tm // LANES, LANES, LANES),
                        lambda i, p: (out_blk(i)[0], p, out_blk(i)[1], 0, 0))
    row_bf = jax.ShapeDtypeStruct((bsz, tt, d), BF16)
    row_f = jax.ShapeDtypeStruct((bsz, tt, d), F32)
    tr_bf = jax.ShapeDtypeStruct((bsz, npair, tt // LANES, LANES, LANES), BF16)
    tr_f = jax.ShapeDtypeStruct((bsz, npair, tt // LANES, LANES, LANES), F32)
    dir_specs = [row_o] * 4 + [tr_o] * 3
    dir_shapes = [row_bf] * 4 + [tr_bf, tr_bf, tr_f]
    return pl.pallas_call(
        functools.partial(_rw_prep_kernel, tm=tm, lora=lora),
        grid=(_div(m, tm), npair),
        in_specs=[tile, tile, tile, lora_blk, lora_blk, wl_blk, wl_blk, v2_blk, v2_blk, v1_blk, v1_blk, v1_blk],
        out_specs=[row_o, row_o] + dir_specs + dir_specs,
        out_shape=[row_bf, row_f] + dir_shapes + dir_shapes,
        compiler_params=_cparams("arbitrary", "arbitrary"),
        name="rwkv_prep",
    )(k, r, v, hw, ha, w2, a2, w0.reshape(2, 1, d), a0.reshape(2, 1, d), k_k.reshape(1, d),
      k_a.reshape(1, d), r_k.reshape(1, d))


def _scan_kernel(kt_ref, rt_ref, bh_ref, kh_ref, v_ref, bt_ref, kbt_ref, gc_ref, y_ref, s_ref, *, rev,
                 ctx_groups, n_groups):
    c = SCAN_CHUNK
    hd = RW_HEAD_DIM
    s_ref[...] = jnp.zeros(s_ref.shape, F32)
    ri = lax.broadcasted_iota(jnp.int32, (c, c), 0)
    ci = lax.broadcasted_iota(jnp.int32, (c, c), 1)
    strict = (ci > ri) if rev else (ci < ri)
    incl = (ci >= ri) if rev else (ci <= ri)
    diag_blk = (ri // SOLVE_BLOCK) == (ci // SOLVE_BLOCK)
    eye = jnp.where(ri == ci, 1.0, 0.0)

    def chunk(g, half, hh):
        r0 = pl.multiple_of(g * LANES + half * c, c)
        ls = slice(hh * hd, (hh + 1) * hd)
        ts = slice(half * c, (half + 1) * c)
        kt = kt_ref[pl.ds(r0, c), ls]
        rt = rt_ref[pl.ds(r0, c), ls]
        bh = bh_ref[pl.ds(r0, c), ls]
        kh = kh_ref[pl.ds(r0, c), ls]
        vv = v_ref[pl.ds(r0, c), ls]
        b_t = bt_ref[g, ls, ts]
        k_t = kbt_ref[g, ls, ts]
        gc = gc_ref[g, ls, half * c:half * c + 1]
        s0 = s_ref[hh]
        s0b = s0.astype(BF16)
        a1 = jnp.where(strict, _nt(kt, bh), 0.0)
        a2 = jnp.where(strict, _nt(kt, kh), 0.0).astype(BF16)
        a3 = jnp.where(incl, _nt(rt, bh), 0.0).astype(BF16)
        a4 = jnp.where(incl, _nt(rt, kh), 0.0).astype(BF16)
        rhs = -(jnp.dot(kt, s0b, preferred_element_type=F32) + jnp.dot(a2, vv, preferred_element_type=F32))
        ad = jnp.where(diag_blk, a1, 0.0)
        ao = a1 - ad
        tinv = eye - ad
        pw = ad
        for _ in range(int(math.log2(SOLVE_BLOCK)) - 1):
            pw = _mm(pw, pw)
            tinv = tinv + _mm(tinv, pw)
        nmat = _mm(tinv, ao).astype(BF16)
        x1 = _mm(tinv, rhs)
        u = x1
        for _ in range(c // SOLVE_BLOCK - 1):
            u = x1 - jnp.dot(nmat, u.astype(BF16), preferred_element_type=F32)
        ub = u.astype(BF16)
        y = (jnp.dot(rt, s0b, preferred_element_type=F32) + jnp.dot(a3, ub, preferred_element_type=F32)
             + jnp.dot(a4, vv, preferred_element_type=F32))
        y_ref[pl.ds(r0, c), ls] = y
        s_ref[hh] = (gc * s0 + jnp.dot(b_t, ub, preferred_element_type=F32)
                     + jnp.dot(k_t, vv, preferred_element_type=F32))

    def group(gi, carry):
        if rev:
            g = jnp.where(gi < ctx_groups, ctx_groups - 1 - gi, n_groups - 1 - (gi - ctx_groups))
        else:
            g = gi
        for half in ((1, 0) if rev else (0, 1)):
            for hh in range(2):
                chunk(g, half, hh)
        return carry

    lax.fori_loop(0, n_groups, group, 0)


def _rw_scan(ops, v, *, rev, l_len):
    kt, rt, bh, kh, bt, kbt, gc = ops
    bsz, tt, d = kt.shape
    npair = d // LANES
    ng = tt // LANES
    row = pl.BlockSpec((None, tt, LANES), lambda b, p: (b, 0, p))
    tr = pl.BlockSpec((None, None, ng, LANES, LANES), lambda b, p: (b, p, 0, 0, 0))
    return pl.pallas_call(
        functools.partial(_scan_kernel, rev=rev, ctx_groups=l_len // LANES, n_groups=ng),
        grid=(bsz, npair),
        in_specs=[row, row, row, row, row, tr, tr, tr],
        out_specs=row,
        out_shape=jax.ShapeDtypeStruct((bsz, tt, d), F32),
        scratch_shapes=[pltpu.VMEM((2, RW_HEAD_DIM, RW_HEAD_DIM), F32)],
        compiler_params=_cparams("arbitrary", "arbitrary"),
        name="rwkv_scan_rev" if rev else "rwkv_scan_fwd",
    )(kt, rt, bh, kh, v, bt, kbt, gc)


def _readout_kernel(y0_ref, y1_ref, bc_ref, v_ref, g_ref, lng_ref, lnb_ref, o_ref):
    lane = lax.broadcasted_iota(jnp.int32, (1, LANES), 1)
    head0 = lane < RW_HEAD_DIM
    y = y0_ref[...] + y1_ref[...]
    inv_n = 1.0 / RW_HEAD_DIM
    mean = _head_sum(y, head0) * inv_n
    yc = y - mean
    var = _head_sum(yc * yc, head0) * inv_n
    yn = yc * lax.rsqrt(var + RW_GN_EPS) * lng_ref[...] + lnb_ref[...]
    o_ref[...] = ((yn + bc_ref[...] * v_ref[...]) * g_ref[...]).astype(o_ref.dtype)


def _rw_readout(y0, y1, bc, v, g, ln_g, ln_b, *, bsz, t_len, l_len):
    d = v.shape[1]
    tm = 256
    per_b = t_len // tm
    off = l_len // tm
    scan_blk = pl.BlockSpec((None, tm, LANES), lambda i, p: (i // per_b, off + i % per_b, p))
    tile = pl.BlockSpec((tm, LANES), lambda i, p: (i, p))
    vec = pl.BlockSpec((1, LANES), lambda i, p: (0, p))
    return pl.pallas_call(
        _readout_kernel,
        grid=(_div(bsz * t_len, tm), _div(d, LANES)),
        in_specs=[scan_blk, scan_blk, scan_blk, tile, tile, vec, vec],
        out_specs=tile,
        out_shape=jax.ShapeDtypeStruct((bsz * t_len, d), BF16),
        compiler_params=_cparams("arbitrary", "arbitrary"),
        name="rwkv_readout",
    )(y0, y1, bc, v, g, ln_g.reshape(1, d), ln_b.reshape(1, d))


def _pad_cols(w, n):
    return jnp.pad(w, ((0, 0), (0, n - w.shape[1])))


def _pad_rows(w, n):
    return jnp.pad(w, ((0, n - w.shape[0]), (0, 0)))


def kernel(x, c, ctx, c_ctx, norm1_g, norm2_g, w_mod, b_mod, ffn_w_gate, ffn_w_up, ffn_conv_w, ffn_conv_b, ffn_w_down, final_norm_g, pool_w, pool_b, pool_scale, na_w_qkv, na_rpb, na_w_o, sg_w_in, sg_b_in, sg_norm_g, sg_w_s, sg_b_s, sg_w_o, rw_mu, rw_w_rkv, rw_w0, rw_w1, rw_w2, rw_a0, rw_a1, rw_a2, rw_g1, rw_g2, rw_k_k, rw_k_a, rw_r_k, rw_ln_g, rw_ln_b, rw_w_o):
    bsz, t_len, d = x.shape
    l_len = ctx.shape[1]
    depth = norm1_g.shape[0]
    n_mixers = 4
    n_lat = bsz * t_len
    n_all = n_lat + bsz * l_len
    f = ffn_w_gate.shape[2]
    f_pad = -(-f // 512) * 512

    cvec = jnp.concatenate([c, c_ctx[None], jnp.zeros((SUBLANES - bsz - 1, d), F32)], axis=0)
    mods = _modulation(cvec, w_mod, b_mod)
    h = jnp.concatenate([x.reshape(n_lat, d), ctx.reshape(bsz * l_len, d)], axis=0)
    seg = dict(seq=t_len, nseg=bsz)
    edges = dict(n_lat_rows=n_lat, seq_lat=t_len, seq_ctx=l_len)

    for i in range(depth):
        m_kind, j = i % n_mixers, i // n_mixers
        last = i == depth - 1
        mod = mods[i].reshape(SUBLANES * 6, 1, d)
        rows = n_lat if last else n_all
        if m_kind == 0:
            u = _norm(h, norm1_g[i], mod, 0, 1, t_len, bsz, F32, rows=rows)
            pw = pool_w[j].astype(BF16)
            h = _pool_mix(u, pw, pool_b[j], pool_scale[j], h, mod, seq=t_len, row0=0, nseq=bsz,
                          seg0=lambda s: s)
            if not last:
                h = _pool_mix(u, pw, pool_b[j], pool_scale[j], h, mod, seq=l_len, row0=n_lat, nseq=bsz,
                              seg0=lambda s: bsz)
        elif m_kind == 1:
            u = _norm(h, norm1_g[i], mod, 0, 1, t_len, bsz, BF16)
            qkv = _matmul(u, na_w_qkv[j].astype(BF16))
            o = _neighbourhood_attention(qkv, na_rpb[j], bsz=bsz, t_len=t_len, l_len=l_len)
            h = _matmul(o, na_w_o[j].astype(BF16), res=h, mod=mod, k_gate=2, rows=rows, **seg)
        elif m_kind == 2:
            u = _norm(h, norm1_g[i], mod, 0, 1, t_len, bsz, BF16, rows=rows)
            z = _matmul(u, sg_w_in[j].astype(BF16), bias=sg_b_in[j], act="gelu", out_dtype=BF16)
            gated = _spatial_gate(z, sg_norm_g[j], sg_w_s[j], sg_b_s[j])
            h = _matmul(gated, sg_w_o[j].astype(BF16), res=h, mod=mod, k_gate=2, rows=rows, **seg)
        else:
            lora = LANES
            u = _norm(h, norm1_g[i], mod, 0, 1, t_len, bsz, F32)
            xr, xw, xk, xv, xa, xg = _token_shift(u, rw_mu[j], **edges)
            w_rkv = rw_w_rkv[j].astype(BF16)
            r = _matmul(xr, w_rkv[0])
            k = _matmul(xk, w_rkv[1])
            v = _matmul(xv, w_rkv[2])
            w1 = jnp.concatenate([_pad_cols(rw_w1[j, e], lora) for e in range(2)], axis=1).astype(BF16)
            a1 = jnp.concatenate([_pad_cols(rw_a1[j, e], lora) for e in range(2)], axis=1).astype(BF16)
            w2 = jnp.stack([_pad_rows(rw_w2[j, e], lora) for e in range(2)]).astype(BF16)
            a2 = jnp.stack([_pad_rows(rw_a2[j, e], lora) for e in range(2)]).astype(BF16)
            hw = _matmul(xw, w1, act="tanh", out_dtype=BF16)
            ha = _matmul(xa, a1, out_dtype=BF16)
            hg = _matmul(xg, rw_g1[j].astype(BF16), act="sigmoid", out_dtype=BF16, rows=n_lat)
            g = _matmul(hg, rw_g2[j].astype(BF16))
            prep = _rw_prep(k, r, v, hw, ha, w2, a2, rw_w0[j], rw_a0[j], rw_k_k[j], rw_k_a[j], rw_r_k[j],
                            bsz=bsz, t_len=t_len, l_len=l_len, lora=lora)
            v_b, bc = prep[0], prep[1]
            y0 = _rw_scan(prep[2:9], v_b, rev=False, l_len=l_len)
            y1 = _rw_scan(prep[9:16], v_b, rev=True, l_len=l_len)
            o = _rw_readout(y0, y1, bc, v, g, rw_ln_g[j], rw_ln_b[j], bsz=bsz, t_len=t_len, l_len=l_len)
            h = _matmul(o, rw_w_o[j].astype(BF16), res=h, mod=mod, k_gate=2, rows=n_lat, **seg)
        u2 = _norm(h, norm2_g[i], mod, 3, 4, t_len, bsz, BF16, rows=rows)
        wg = _pad_cols(ffn_w_gate[i], f_pad).astype(BF16)
        wu = _pad_cols(ffn_w_up[i], f_pad).astype(BF16)
        cw = _pad_cols(ffn_conv_w[i], f_pad)
        cb = _pad_cols(ffn_conv_b[i][None], f_pad)
        wd = _pad_rows(ffn_w_down[i], f_pad).astype(BF16)
        mid = _ffn1(u2, wg, wu, cw, cb, rows=rows, **edges)
        h = _matmul(mid, wd, res=h, mod=mod, k_gate=5, rows=rows, tm=512, **seg)

    out = _norm(h, final_norm_g, None, 0, 0, t_len, bsz, F32, rows=n_lat)
    return out.reshape(bsz, t_len, d)
```

```python
import functools
import math

import jax
import jax.numpy as jnp
from jax import lax
from jax.experimental import pallas as pl
from jax.experimental.pallas import tpu as pltpu

F32 = jnp.float32
BF16 = jnp.bfloat16

GRID_W = 64
NORM_EPS = 1e-6
POOL_WINDOWS = (2, 4, 8, 16)
NA_HEAD_DIM = 64
NA_KH = 8
NA_KW = 16
ROPE_BASE = 10000.0
SG_CHUNK = 128
SG_GROUPS = 16
RW_HEAD_DIM = 64
RW_GN_EPS = 64e-5

LANES = 128
SUBLANES = 8
VMEM_LIMIT = 56 * 1024 * 1024

SCAN_CHUNK = 64
SOLVE_BLOCK = 16


def _cparams(*sem):
    return pltpu.CompilerParams(dimension_semantics=sem, vmem_limit_bytes=VMEM_LIMIT)


def _div(a, b):
    assert a % b == 0, (a, b)
    return a // b


def _nt(a, b):
    return lax.dot_general(a, b, (((1,), (1,)), ((), ())), preferred_element_type=F32)


def _mm(a, b):
    return jnp.dot(a.astype(BF16), b.astype(BF16), preferred_element_type=F32)


def _silu(x):
    return x * jax.nn.sigmoid(x)


def _head_sum(x, lane_is_head0):
    s0 = jnp.sum(jnp.where(lane_is_head0, x, 0.0), axis=-1, keepdims=True)
    s1 = jnp.sum(jnp.where(lane_is_head0, 0.0, x), axis=-1, keepdims=True)
    return jnp.where(lane_is_head0, s0, s1)


def _seq_edges(i, tm, n_lat_rows, seq_lat, seq_ctx):
    row = lax.broadcasted_iota(jnp.int32, (tm, 1), 0)
    grow = i * tm + row
    slen = jnp.where(grow < n_lat_rows, seq_lat, seq_ctx)
    pos = grow & (slen - 1)
    return row, pos == 0, pos == slen - 1


def _shift_rows(x, halo_prev, halo_next, row, first, last, tm):
    xp = pltpu.roll(x, 1, axis=0)
    xp = jnp.where(row == 0, halo_prev[SUBLANES - 1:SUBLANES], xp)
    xp = jnp.where(first, 0.0, xp)
    xn = pltpu.roll(x, tm - 1, axis=0)
    xn = jnp.where(row == tm - 1, halo_next[0:1], xn)
    xn = jnp.where(last, 0.0, xn)
    return xp, xn


def _mod_kernel(c_ref, w_ref, b_ref, o_ref):
    s = _silu(c_ref[...]).astype(BF16)
    o_ref[0] = jnp.dot(s, w_ref[0].astype(BF16), preferred_element_type=F32) + b_ref[0]


def _modulation(cvec, w_mod, b_mod):
    depth, d, n = w_mod.shape
    tn = 1024
    return pl.pallas_call(
        _mod_kernel,
        grid=(depth, _div(n, tn)),
        in_specs=[pl.BlockSpec((SUBLANES, d), lambda l, j: (0, 0)),
                  pl.BlockSpec((1, d, tn), lambda l, j: (l, 0, j)),
                  pl.BlockSpec((1, 1, tn), lambda l, j: (l, 0, j))],
        out_specs=pl.BlockSpec((1, SUBLANES, tn), lambda l, j: (l, 0, j)),
        out_shape=jax.ShapeDtypeStruct((depth, SUBLANES, n), F32),
        compiler_params=_cparams("arbitrary", "arbitrary"),
        name="modulation",
    )(cvec, w_mod, b_mod.reshape(depth, 1, n))


def _norm_kernel(*refs, modulate):
    if modulate:
        h_ref, g_ref, sh_ref, sc_ref, o_ref = refs
    else:
        h_ref, g_ref, o_ref = refs
    x = h_ref[...]
    y = x * lax.rsqrt(jnp.mean(x * x, axis=-1, keepdims=True) + NORM_EPS) * g_ref[...]
    if modulate:
        y = y * (1.0 + sc_ref[0]) + sh_ref[0]
    o_ref[...] = y.astype(o_ref.dtype)


def _norm(h, g, mod, k_shift, k_scale, seq, nseg, out_dtype, rows=None):
    m, d = h.shape if rows is None else (rows, h.shape[1])
    tm = 256
    row_spec = pl.BlockSpec((tm, d), lambda i: (i, 0))
    in_specs = [row_spec, pl.BlockSpec((1, d), lambda i: (0, 0))]
    args = [h, g.reshape(1, d)]
    if mod is not None:
        seg = lambda i: jnp.minimum((i * tm) // seq, nseg)
        in_specs += [pl.BlockSpec((1, 1, d), lambda i: (seg(i) * 6 + k_shift, 0, 0)),
                     pl.BlockSpec((1, 1, d), lambda i: (seg(i) * 6 + k_scale, 0, 0))]
        args += [mod, mod]
    return pl.pallas_call(
        functools.partial(_norm_kernel, modulate=mod is not None),
        grid=(_div(m, tm),),
        in_specs=in_specs,
        out_specs=row_spec,
        out_shape=jax.ShapeDtypeStruct((m, d), out_dtype),
        compiler_params=_cparams("arbitrary"),
        name="rmsnorm_mod",
    )(*args)


def _matmul_kernel(*refs, act, has_bias, has_res):
    x_ref, w_ref = refs[0], refs[1]
    k = 2
    acc = jnp.dot(x_ref[...], w_ref[...], preferred_element_type=F32)
    if has_bias:
        acc = acc + refs[k][...]
        k += 1
    if act == "gelu":
        acc = jax.nn.gelu(acc, approximate=True)
    elif act == "tanh":
        acc = jnp.tanh(acc)
    elif act == "sigmoid":
        acc = jax.nn.sigmoid(acc)
    if has_res:
        acc = refs[k][...] + refs[k + 1][0] * acc
        k += 2
    o_ref = refs[k]
    o_ref[...] = acc.astype(o_ref.dtype)


def _matmul(x, w, *, bias=None, act=None, res=None, mod=None, k_gate=None, seq=None, nseg=None,
            out_dtype=F32, tm=1024, tn=512, rows=None):
    m = x.shape[0] if rows is None else rows
    kdim, n = w.shape
    tn = min(tn, n)
    tm = min(tm, m)
    in_specs = [pl.BlockSpec((tm, kdim), lambda i, j: (i, 0)),
                pl.BlockSpec((kdim, tn), lambda i, j: (0, j))]
    args = [x, w]
    if bias is not None:
        in_specs.append(pl.BlockSpec((1, tn), lambda i, j: (0, j)))
        args.append(bias.reshape(1, n))
    if res is not None:
        seg = lambda i: jnp.minimum((i * tm) // seq, nseg)
        in_specs += [pl.BlockSpec((tm, tn), lambda i, j: (i, j)),
                     pl.BlockSpec((1, 1, tn), lambda i, j: (seg(i) * 6 + k_gate, 0, j))]
        args += [res, mod]
    return pl.pallas_call(
        functools.partial(_matmul_kernel, act=act, has_bias=bias is not None, has_res=res is not None),
        grid=(_div(m, tm), _div(n, tn)),
        in_specs=in_specs,
        out_specs=pl.BlockSpec((tm, tn), lambda i, j: (i, j)),
        out_shape=jax.ShapeDtypeStruct((m, n), out_dtype),
        compiler_params=_cparams("arbitrary", "arbitrary"),
        name="matmul",
    )(*args)


def _ffn1_kernel(x_ref, xp_ref, xn_ref, wg_ref, wu_ref, cw_ref, cb_ref, o_ref, *, tm, n_lat_rows,
                 seq_lat, seq_ctx):
    i = pl.program_id(0)
    x = x_ref[...]
    wg = wg_ref[...]
    g = jnp.dot(x, wg, preferred_element_type=F32)
    gp = jnp.dot(xp_ref[...], wg, preferred_element_type=F32)
    gn = jnp.dot(xn_ref[...], wg, preferred_element_type=F32)
    row, first, last = _seq_edges(i, tm, n_lat_rows, seq_lat, seq_ctx)
    g_prev, g_next = _shift_rows(g, gp, gn, row, first, last, tm)
    gte = g_prev * cw_ref[0:1] + g * cw_ref[1:2] + g_next * cw_ref[2:3] + cb_ref[...]
    up = jnp.dot(x, wu_ref[...], preferred_element_type=F32)
    o_ref[...] = (_silu(gte) * up).astype(o_ref.dtype)


def _halo_specs(tm, kdim, m):
    nb = tm // SUBLANES
    prev = pl.BlockSpec((SUBLANES, kdim), lambda i, j: (jnp.maximum(i * nb - 1, 0), 0))
    nxt = pl.BlockSpec((SUBLANES, kdim), lambda i, j: (jnp.minimum((i + 1) * nb, m // SUBLANES - 1), 0))
    return prev, nxt


def _ffn1(u, wg, wu, cw, cb, *, rows, n_lat_rows, seq_lat, seq_ctx):
    kdim, f = wg.shape
    tm, tn = 1024, 512
    prev, nxt = _halo_specs(tm, kdim, rows)
    return pl.pallas_call(
        functools.partial(_ffn1_kernel, tm=tm, n_lat_rows=n_lat_rows, seq_lat=seq_lat, seq_ctx=seq_ctx),
        grid=(_div(rows, tm), _div(f, tn)),
        in_specs=[pl.BlockSpec((tm, kdim), lambda i, j: (i, 0)), prev, nxt,
                  pl.BlockSpec((kdim, tn), lambda i, j: (0, j)),
                  pl.BlockSpec((kdim, tn), lambda i, j: (0, j)),
                  pl.BlockSpec((3, tn), lambda i, j: (0, j)),
                  pl.BlockSpec((1, tn), lambda i, j: (0, j))],
        out_specs=pl.BlockSpec((tm, tn), lambda i, j: (i, j)),
        out_shape=jax.ShapeDtypeStruct((rows, f), BF16),
        compiler_params=_cparams("arbitrary", "arbitrary"),
        name="ffn_gate_up",
    )(u, u, u, wg, wu, cw, cb)


def _pool_kernel(u_ref, w_ref, b_ref, s_ref, h_ref, gate_ref, o_ref, xp_ref, *, seq):
    grp = pl.program_id(1)
    pad = SUBLANES
    zeros = jnp.zeros((pad, u_ref.shape[1]), F32)
    xp_ref[0:pad] = zeros
    xp_ref[pad:pad + seq] = u_ref[...]
    xp_ref[pad + seq:2 * pad + seq] = zeros
    rt = min(seq, 256)
    for gi, win in enumerate(POOL_WINDOWS):
        @pl.when(grp == gi)
        def _(win=win):
            left, right = win // 2, win - 1 - win // 2
            for rc in range(seq // rt):
                base = pad + rc * rt
                t = rc * rt + lax.broadcasted_iota(jnp.int32, (rt, 1), 0)
                acc = xp_ref[base - left:base - left + rt]
                for j in range(1, win):
                    acc = acc + xp_ref[base - left + j:base - left + j + rt]
                cnt = (jnp.minimum(t + right, seq - 1) - jnp.maximum(t - left, 0) + 1).astype(F32)
                p = acc / cnt - xp_ref[base:base + rt]
                y = jnp.dot(p.astype(BF16), w_ref[0], preferred_element_type=F32) + b_ref[...]
                rows = slice(rc * rt, (rc + 1) * rt)
                o_ref[rows] = h_ref[rows] + gate_ref[0] * (y * s_ref[...])


def _pool_mix(u, w, b, scale, h, mod, *, seq, row0, nseq, seg0):
    d = u.shape[1]
    cg = d // len(POOL_WINDOWS)
    blk0 = row0 // seq
    tile = pl.BlockSpec((seq, cg), lambda s, g: (blk0 + s, g))
    vec = pl.BlockSpec((1, cg), lambda s, g: (0, g))
    return pl.pallas_call(
        functools.partial(_pool_kernel, seq=seq),
        grid=(nseq, len(POOL_WINDOWS)),
        in_specs=[tile, pl.BlockSpec((1, cg, cg), lambda s, g: (g, 0, 0)), vec, vec, tile,
                  pl.BlockSpec((1, 1, cg), lambda s, g: ((seg0(s)) * 6 + 2, 0, g))],
        out_specs=tile,
        out_shape=jax.ShapeDtypeStruct(h.shape, F32),
        scratch_shapes=[pltpu.VMEM((seq + 2 * SUBLANES, cg), F32)],
        input_output_aliases={4: 0},
        compiler_params=_cparams("arbitrary", "arbitrary"),
        name="pool_mix",
    )(u, w, b.reshape(1, d), scale.reshape(1, d), h, mod)


def _na_lat_kernel(q_ref, k_ref, v_ref, kc_ref, vc_ref, cos_ref, sin_ref, bias_ref, o_ref,
                   q_s, k_s, v_s, kc_s, vc_s, *, rows, kh):
    hd = NA_HEAD_DIM
    lane = lax.broadcasted_iota(jnp.int32, (1, LANES), 1)
    low = (lane % 32) < 16
    cos, sin = cos_ref[...], sin_ref[...]

    def rope(x):
        swapped = jnp.where(low, pltpu.roll(x, LANES - 16, axis=1), pltpu.roll(x, 16, axis=1))
        return x * cos + swapped * sin

    q = rope(q_ref[...]) * (hd ** -0.5)
    k = rope(k_ref[...])
    v = v_ref[...]
    kc = kc_ref[...]
    vc = vc_ref[...]
    for hh in range(2):
        ls = slice(hh * hd, (hh + 1) * hd)
        q_s[hh] = q[:, ls].astype(BF16)
        k_s[hh] = k[:, ls].astype(BF16)
        v_s[hh] = v[:, ls].astype(BF16)
        kc_s[hh] = kc[:, ls].astype(BF16)
        vc_s[hh] = vc[:, ls].astype(BF16)
    nloc = kh * GRID_W

    rows_per_step = 2
    dot = lambda a, b: jnp.dot(a, b, preferred_element_type=F32)

    def body(step, carry):
        chains = [(step * rows_per_step + dr, hh) for dr in range(rows_per_step) for hh in range(2)]
        n = range(len(chains))
        rs = [jnp.clip(r - kh // 2, 0, rows - kh) for r, _ in chains]
        q0 = [pl.multiple_of(r * GRID_W, GRID_W) for r, _ in chains]
        k0 = [pl.multiple_of(rs[i] * GRID_W, GRID_W) for i in n]
        hh = [h for _, h in chains]
        qr = [q_s[hh[i], pl.ds(q0[i], GRID_W), :] for i in n]
        s_loc = [_nt(qr[i], k_s[hh[i], pl.ds(k0[i], nloc), :]) + bias_ref[hh[i], chains[i][0] - rs[i]] for i in n]
        s_ctx = [_nt(qr[i], kc_s[hh[i]]) for i in n]
        m = [jnp.maximum(jnp.max(s_loc[i], axis=-1, keepdims=True), jnp.max(s_ctx[i], axis=-1, keepdims=True))
             for i in n]
        p_loc = [jnp.exp(s_loc[i] - m[i]) for i in n]
        p_ctx = [jnp.exp(s_ctx[i] - m[i]) for i in n]
        den = [jnp.sum(p_loc[i], axis=-1, keepdims=True) + jnp.sum(p_ctx[i], axis=-1, keepdims=True) for i in n]
        o = [dot(p_loc[i].astype(BF16), v_s[hh[i], pl.ds(k0[i], nloc), :]) + dot(p_ctx[i].astype(BF16), vc_s[hh[i]])
             for i in n]
        o = [o[i] / den[i] for i in n]
        for dr in range(rows_per_step):
            o_ref[pl.ds(q0[2 * dr], GRID_W), :] = jnp.concatenate(o[2 * dr:2 * dr + 2], axis=1).astype(o_ref.dtype)
        return carry

    lax.fori_loop(0, _div(rows, rows_per_step), body, 0)


def _na_ctx_kernel(q_ref, k_ref, v_ref, o_ref):
    hd = NA_HEAD_DIM
    q = q_ref[...] * (hd ** -0.5)
    k = k_ref[...]
    v = v_ref[...]
    for hh in range(2):
        ls = slice(hh * hd, (hh + 1) * hd)
        s = _nt(q[:, ls].astype(BF16), k[:, ls].astype(BF16))
        m = jnp.max(s, axis=-1, keepdims=True)
        p = jnp.exp(s - m)
        den = jnp.sum(p, axis=-1, keepdims=True)
        o = jnp.dot(p.astype(BF16), v[:, ls].astype(BF16), preferred_element_type=F32)
        o_ref[:, ls] = (o / den).astype(o_ref.dtype)


def _rope_tables(t_len, hd):
    half = hd // 2
    pos = jnp.arange(t_len)
    inv_freq = ROPE_BASE ** (-jnp.arange(0, half, 2, dtype=F32) / half)
    d = jnp.arange(hd)
    p = jnp.where(d[None, :] < half, (pos // GRID_W)[:, None], (pos % GRID_W)[:, None]).astype(F32)
    ang = p * inv_freq[d % (half // 2)][None, :]
    cos = jnp.cos(ang)
    sin = jnp.where((d % half) < half // 2, -jnp.sin(ang), jnp.sin(ang))
    return jnp.tile(cos, (1, 2)), jnp.tile(sin, (1, 2))


def _na_bias_table(rpb, kh):
    col = jnp.arange(GRID_W)
    col_start = jnp.clip(col - NA_KW // 2, 0, GRID_W - NA_KW)
    col_ok = (col[None, :] >= col_start[:, None]) & (col[None, :] < col_start[:, None] + NA_KW)
    dc = jnp.clip(col[None, :] - col[:, None], -(NA_KW - 1), NA_KW - 1) + NA_KW - 1
    rpb_cols = rpb[:, :, dc]
    dr = jnp.arange(kh)[None, :] - jnp.arange(kh)[:, None] + NA_KH - 1
    tab = rpb_cols[:, dr]
    tab = jnp.where(col_ok[None, None, None], tab, -1e30)
    heads = rpb.shape[0]
    return tab.transpose(0, 1, 3, 2, 4).reshape(heads, kh, GRID_W, kh * GRID_W)


def _neighbourhood_attention(qkv, rpb, *, bsz, t_len, l_len):
    d = qkv.shape[1] // 3
    hd = NA_HEAD_DIM
    heads = d // hd
    npair = d // LANES
    rows = t_len // GRID_W
    kh = min(NA_KH, rows)
    n_lat = bsz * t_len
    qkv_lat = qkv[:n_lat].reshape(bsz, t_len, 3 * d)
    qkv_ctx = qkv[n_lat:].reshape(bsz, l_len, 3 * d)
    cos, sin = _rope_tables(t_len, hd)
    bias = _na_bias_table(rpb, kh)
    lat_blk = lambda off: pl.BlockSpec((None, t_len, LANES), lambda b, p: (b, 0, off + p))
    ctx_blk = lambda off: pl.BlockSpec((None, l_len, LANES), lambda b, p: (b, 0, off + p))
    tab_blk = pl.BlockSpec((t_len, LANES), lambda b, p: (0, 0))
    o_lat = pl.pallas_call(
        functools.partial(_na_lat_kernel, rows=rows, kh=kh),
        grid=(bsz, npair),
        in_specs=[lat_blk(0), lat_blk(npair), lat_blk(2 * npair), ctx_blk(npair), ctx_blk(2 * npair),
                  tab_blk, tab_blk,
                  pl.BlockSpec((2, kh, GRID_W, kh * GRID_W), lambda b, p: (p, 0, 0, 0))],
        out_specs=pl.BlockSpec((None, t_len, LANES), lambda b, p: (b, 0, p)),
        out_shape=jax.ShapeDtypeStruct((bsz, t_len, d), BF16),
        scratch_shapes=[pltpu.VMEM((2, t_len, hd), BF16), pltpu.VMEM((2, t_len, hd), BF16),
                        pltpu.VMEM((2, t_len, hd), BF16), pltpu.VMEM((2, l_len, hd), BF16),
                        pltpu.VMEM((2, l_len, hd), BF16)],
        compiler_params=_cparams("arbitrary", "arbitrary"),
        name="na_latent",
    )(qkv_lat, qkv_lat, qkv_lat, qkv_ctx, qkv_ctx, cos, sin, bias)
    o_ctx = pl.pallas_call(
        _na_ctx_kernel,
        grid=(bsz, npair),
        in_specs=[ctx_blk(0), ctx_blk(npair), ctx_blk(2 * npair)],
        out_specs=pl.BlockSpec((None, l_len, LANES), lambda b, p: (b, 0, p)),
        out_shape=jax.ShapeDtypeStruct((bsz, l_len, d), BF16),
        compiler_params=_cparams("arbitrary", "arbitrary"),
        name="na_context",
    )(qkv_ctx, qkv_ctx, qkv_ctx)
    return jnp.concatenate([o_lat.reshape(n_lat, d), o_ctx.reshape(bsz * l_len, d)], axis=0)


def _sg_kernel(z_u_ref, z_v_ref, g_ref, ws_ref, bs_ref, o_ref, *, nchunk):
    zv = z_v_ref[...].astype(F32)
    zv = zv * lax.rsqrt(jnp.mean(zv * zv, axis=-1, keepdims=True) + NORM_EPS) * g_ref[...]
    zvb = zv.astype(BF16)
    for c in range(nchunk):
        rs = slice(c * SG_CHUNK, (c + 1) * SG_CHUNK)
        for g in range(SG_GROUPS):
            ls = slice(g * LANES, (g + 1) * LANES)
            mixed = jnp.dot(ws_ref[g], zvb[rs, ls], preferred_element_type=F32) + bs_ref[:, ls]
            o_ref[rs, ls] = (z_u_ref[rs, ls].astype(F32) * mixed).astype(o_ref.dtype)


def _spatial_gate(z, norm_g, w_s, b_s):
    m, two_w = z.shape
    width = two_w // 2
    nchunk = 2
    tm = nchunk * SG_CHUNK
    bs_full = jnp.repeat(b_s.T, width // SG_GROUPS, axis=1)
    return pl.pallas_call(
        functools.partial(_sg_kernel, nchunk=nchunk),
        grid=(_div(m, tm),),
        in_specs=[pl.BlockSpec((tm, width), lambda i: (i, 0)),
                  pl.BlockSpec((tm, width), lambda i: (i, 1)),
                  pl.BlockSpec((1, width), lambda i: (0, 0)),
                  pl.BlockSpec((SG_GROUPS, SG_CHUNK, SG_CHUNK), lambda i: (0, 0, 0)),
                  pl.BlockSpec((SG_CHUNK, width), lambda i: (0, 0))],
        out_specs=pl.BlockSpec((tm, width), lambda i: (i, 0)),
        out_shape=jax.ShapeDtypeStruct((m, width), BF16),
        compiler_params=_cparams("arbitrary"),
        name="spatial_gate",
    )(z, z, norm_g.reshape(1, width), w_s.astype(BF16), bs_full)


def _shift_kernel(u_ref, up_ref, un_ref, mu_ref, *o_refs, tm, n_lat_rows, seq_lat, seq_ctx):
    i = pl.program_id(0)
    u = u_ref[...]
    row, first, last = _seq_edges(i, tm, n_lat_rows, seq_lat, seq_ctx)
    u_prev, u_next = _shift_rows(u, up_ref[...], un_ref[...], row, first, last, tm)
    prev = u_prev - u
    nxt = u_next - u
    nmix = len(o_refs)
    for n in range(nmix):
        o_refs[n][...] = (u + prev * mu_ref[n:n + 1] + nxt * mu_ref[nmix + n:nmix + n + 1]).astype(BF16)


def _token_shift(u, mu, *, n_lat_rows, seq_lat, seq_ctx):
    m, d = u.shape
    nmix = mu.shape[1]
    tm, tn = 512, 512
    nb = tm // SUBLANES
    tile = pl.BlockSpec((tm, tn), lambda i, j: (i, j))
    prev = pl.BlockSpec((SUBLANES, tn), lambda i, j: (jnp.maximum(i * nb - 1, 0), j))
    nxt = pl.BlockSpec((SUBLANES, tn), lambda i, j: (jnp.minimum((i + 1) * nb, m // SUBLANES - 1), j))
    return pl.pallas_call(
        functools.partial(_shift_kernel, tm=tm, n_lat_rows=n_lat_rows, seq_lat=seq_lat, seq_ctx=seq_ctx),
        grid=(_div(m, tm), _div(d, tn)),
        in_specs=[tile, prev, nxt, pl.BlockSpec((2 * nmix, tn), lambda i, j: (0, j))],
        out_specs=[tile] * nmix,
        out_shape=[jax.ShapeDtypeStruct((m, d), BF16)] * nmix,
        compiler_params=_cparams("arbitrary", "arbitrary"),
        name="token_shift",
    )(u, u, u, mu.reshape(2 * nmix, d))


def _split3(x):
    hi = x.astype(BF16)
    r1 = x - hi.astype(F32)
    mid = r1.astype(BF16)
    lo = (r1 - mid.astype(F32)).astype(BF16)
    return hi, mid, lo


def _rw_prep_kernel(k_ref, r_ref, v_ref, hw_ref, ha_ref, w2_ref, a2_ref, w0_ref, a0_ref, kk_ref, ka_ref,
                    rk_ref, *o_refs, tm, lora):
    (v_o, bc_o, kt0, rt0, bh0, kh0, bt0, kbt0, gc0, kt1, rt1, bh1, kh1, bt1, kbt1, gc1) = o_refs
    per_dir = ((kt0, rt0, bh0, kh0, bt0, kbt0, gc0), (kt1, rt1, bh1, kh1, bt1, kbt1, gc1))
    lane = lax.broadcasted_iota(jnp.int32, (1, LANES), 1)
    head0 = lane < RW_HEAD_DIM
    k = k_ref[...]
    r = r_ref[...]
    v = v_ref[...]
    v_o[...] = v.astype(BF16)
    kkv = k * kk_ref[...]
    kk = kkv * lax.rsqrt(jnp.maximum(_head_sum(kkv * kkv, head0), 1e-12))
    ka = ka_ref[...]
    rk = rk_ref[...]
    hw = hw_ref[...]
    ha = ha_ref[...]
    ri = lax.broadcasted_iota(jnp.int32, (tm, tm), 0)
    ci = lax.broadcasted_iota(jnp.int32, (tm, tm), 1)
    same_chunk = (ri // SCAN_CHUNK) == (ci // SCAN_CHUNK)
    tri = (jnp.where(same_chunk & (ci <= ri), 1.0, 0.0).astype(BF16),
           jnp.where(same_chunk & (ci >= ri), 1.0, 0.0).astype(BF16))
    log_decay, beta, key = [], [], []
    for d in range(2):
        ls = slice(d * lora, (d + 1) * lora)
        pre = w0_ref[d] + jnp.dot(hw[:, ls], w2_ref[d], preferred_element_type=F32)
        z = -pre
        softplus = jnp.maximum(z, 0.0) + jnp.log(1.0 + jnp.exp(-jnp.abs(z)))
        log_decay.append(-jnp.exp(-softplus - 0.5))
        a = jax.nn.sigmoid(a0_ref[d] + jnp.dot(ha[:, ls], a2_ref[d], preferred_element_type=F32))
        beta.append(kk * a)
        key.append(k * (1.0 + (a - 1.0) * ka))
    bc_o[...] = _head_sum(r * key[0] * rk, head0) + _head_sum(r * key[1] * rk, head0)
    for d in range(2):
        ld = log_decay[d]
        pieces = _split3(ld)
        csum = [sum(jnp.dot(tri[e], p, preferred_element_type=F32) for p in pieces) for e in range(2)]
        cum = csum[d]
        total = csum[0] + csum[1] - ld
        kt, rt, bh, kh, bt, kbt, gc = per_dir[d]
        kt[...] = (kk * jnp.exp(cum - ld)).astype(BF16)
        rt[...] = (r * jnp.exp(cum)).astype(BF16)
        inv = jnp.exp(-cum)
        bh[...] = (beta[d] * inv).astype(BF16)
        kh[...] = (key[d] * inv).astype(BF16)
        tail = jnp.exp(total - cum)
        b_t = jnp.transpose(beta[d] * tail)
        k_t = jnp.transpose(key[d] * tail)
        g_t = jnp.transpose(jnp.exp(total))
        for c in range(tm // LANES):
            cs = slice(c * LANES, (c + 1) * LANES)
            bt[c] = b_t[:, cs].astype(BF16)
            kbt[c] = k_t[:, cs].astype(BF16)
            gc[c] = g_t[:, cs]


def _rw_prep(k, r, v, hw, ha, w2, a2, w0, a0, k_k, k_a, r_k, *, bsz, t_len, l_len, lora):
    m, d = k.shape
    tm = 256
    npair = d // LANES
    lat_tiles = bsz * t_len // tm
    per_b = t_len // tm
    tt = t_len + l_len
    nblk = tt // tm

    def out_blk(i):
        is_lat = i < lat_tiles
        b = jnp.where(is_lat, i // per_b, i - lat_tiles)
        blk = jnp.where(is_lat, l_len // tm + i % per_b, 0)
        return b, blk

    tile = pl.BlockSpec((tm, LANES), lambda i, p: (i, p))
    lora_blk = pl.BlockSpec((tm, 2 * lora), lambda i, p: (i, 0))
    wl_blk = pl.BlockSpec((2, lora, LANES), lambda i, p: (0, 0, p))
    v2_blk = pl.BlockSpec((2, 1, LANES), lambda i, p: (0, 0, p))
    v1_blk = pl.BlockSpec((1, LANES), lambda i, p: (0, p))
    row_o = pl.BlockSpec((None, tm, LANES), lambda i, p: (out_blk(i)[0], out_blk(i)[1], p))
    tr_o = pl.BlockSpec((None, None, tm // LANES, LANES, LANES),
                        lambda i, p: (out_blk(i)[0], p, out_blk(i)[1], 0, 0))
    row_bf = jax.ShapeDtypeStruct((bsz, tt, d), BF16)
    row_f = jax.ShapeDtypeStruct((bsz, tt, d), F32)
    tr_bf = jax.ShapeDtypeStruct((bsz, npair, tt // LANES, LANES, LANES), BF16)
    tr_f = jax.ShapeDtypeStruct((bsz, npair, tt // LANES, LANES, LANES), F32)
    dir_specs = [row_o] * 4 + [tr_o] * 3
    dir_shapes = [row_bf] * 4 + [tr_bf, tr_bf, tr_f]
    return pl.pallas_call(
        functools.partial(_rw_prep_kernel, tm=tm, lora=lora),
        grid=(_div(m, tm), npair),
        in_specs=[tile, tile, tile, lora_blk, lora_blk, wl_blk, wl_blk, v2_blk, v2_blk, v1_blk, v1_blk, v1_blk],
        out_specs=[row_o, row_o] + dir_specs + dir_specs,
        out_shape=[row_bf, row_f] + dir_shapes + dir_shapes,
        compiler_params=_cparams("arbitrary", "arbitrary"),
        name="rwkv_prep",
    )(k, r, v, hw, ha, w2, a2, w0.reshape(2, 1, d), a0.reshape(2, 1, d), k_k.reshape(1, d),
      k_a.reshape(1, d), r_k.reshape(1, d))


def _chunk_summary_kernel(kt_ref, rt_ref, bh_ref, kh_ref, v_ref, bt_ref, kbt_ref, gc_ref,
                          m_o, c_o, g_o, rp_o, y0_o, *, rev, nheads):
    c = SCAN_CHUNK
    hd = RW_HEAD_DIM
    ri = lax.broadcasted_iota(jnp.int32, (c, c), 0)
    ci = lax.broadcasted_iota(jnp.int32, (c, c), 1)
    strict = (ci > ri) if rev else (ci < ri)
    incl = (ci >= ri) if rev else (ci <= ri)
    diag_blk = (ri // SOLVE_BLOCK) == (ci // SOLVE_BLOCK)
    eye = jnp.where(ri == ci, 1.0, 0.0)
    chains = [(half, hh) for half in range(LANES // c) for hh in range(nheads)]
    rows = [slice(half * c, (half + 1) * c) for half, _ in chains]
    ls = [slice(hh * hd, (hh + 1) * hd) for _, hh in chains]
    pair = [hh // 2 for _, hh in chains]
    sub = [slice((hh % 2) * hd, (hh % 2 + 1) * hd) for _, hh in chains]
    n = range(len(chains))
    dot = lambda a, b: jnp.dot(a, b, preferred_element_type=F32)
    kt = [kt_ref[rows[i], ls[i]] for i in n]
    rt = [rt_ref[rows[i], ls[i]] for i in n]
    bh = [bh_ref[rows[i], ls[i]] for i in n]
    kh = [kh_ref[rows[i], ls[i]] for i in n]
    vv = [v_ref[rows[i], ls[i]] for i in n]
    a1 = [jnp.where(strict, _nt(kt[i], bh[i]), 0.0) for i in n]
    a2 = [jnp.where(strict, _nt(kt[i], kh[i]), 0.0).astype(BF16) for i in n]
    a3 = [jnp.where(incl, _nt(rt[i], bh[i]), 0.0).astype(BF16) for i in n]
    a4 = [jnp.where(incl, _nt(rt[i], kh[i]), 0.0).astype(BF16) for i in n]
    ad = [jnp.where(diag_blk, a1[i], 0.0) for i in n]
    ao = [(a1[i] - ad[i]).astype(BF16) for i in n]
    tinv = [eye - ad[i] for i in n]
    pw = ad
    for _ in range(int(math.log2(SOLVE_BLOCK)) - 1):
        pw = [_mm(pw[i], pw[i]) for i in n]
        tinv = [tinv[i] + _mm(tinv[i], pw[i]) for i in n]
    a2v = [dot(a2[i], vv[i]) for i in n]
    rhs = [jnp.concatenate([kt[i].astype(F32), a2v[i]], axis=1).astype(BF16) for i in n]
    tinv = [t.astype(BF16) for t in tinv]
    nmat = [dot(tinv[i], ao[i]).astype(BF16) for i in n]
    x1 = [dot(tinv[i], rhs[i]) for i in n]
    x = x1
    for _ in range(c // SOLVE_BLOCK - 1):
        x = [x1[i] - dot(nmat[i], x[i].astype(BF16)) for i in n]
    xb = [x[i].astype(BF16) for i in n]
    gmat = [dot(bt_ref[pair[i], 0, sub[i], rows[i]], xb[i]) for i in n]
    qmat = [dot(a3[i], xb[i]) for i in n]
    kv = [dot(kbt_ref[pair[i], 0, sub[i], rows[i]], vv[i]) for i in n]
    a4v = [dot(a4[i], vv[i]) for i in n]
    for i, (half, hh) in enumerate(chains):
        gc = gc_ref[pair[i], 0, sub[i], half * c:half * c + 1]
        m_o[hh, half] = (-gmat[i][:, :hd]).astype(m_o.dtype)
        c_o[hh, half] = kv[i] - gmat[i][:, hd:]
        g_o[hh, half] = jnp.broadcast_to(gc, (hd, hd))
        rp_o[rows[i], ls[i]] = (rt[i].astype(F32) - qmat[i][:, :hd]).astype(rp_o.dtype)
        y0_o[rows[i], ls[i]] = a4v[i] - qmat[i][:, hd:]


def _state_pass_kernel(m_ref, c_ref, g_ref, rp_ref, y0_ref, y_ref, s_ref, *, rev, ctx_chunks, n_chunks,
                       nheads):
    c = SCAN_CHUNK
    hd = RW_HEAD_DIM
    s_ref[...] = jnp.zeros(s_ref.shape, F32)

    def step(i, carry):
        if rev:
            ch = jnp.where(i < ctx_chunks, ctx_chunks - 1 - i, n_chunks - 1 - (i - ctx_chunks))
        else:
            ch = i
        r0 = pl.multiple_of(ch * c, c)
        heads = range(nheads)
        ls = [slice(hh * hd, (hh + 1) * hd) for hh in heads]
        s0 = [s_ref[hh] for hh in heads]
        s0b = [s.astype(BF16) for s in s0]
        ms = [jnp.dot(m_ref[hh, ch], s0b[hh], preferred_element_type=F32) for hh in heads]
        ys = [jnp.dot(rp_ref[pl.ds(r0, c), ls[hh]], s0b[hh], preferred_element_type=F32) for hh in heads]
        for hh in heads:
            s_ref[hh] = g_ref[hh, ch] * s0[hh] + ms[hh] + c_ref[hh, ch]
            y_ref[pl.ds(r0, c), ls[hh]] = ys[hh] + y0_ref[pl.ds(r0, c), ls[hh]]
        return carry

    lax.fori_loop(0, n_chunks, step, 0)


def _rw_scan(ops, v, *, rev, l_len):
    kt, rt, bh, kh, bt, kbt, gc = ops
    bsz, tt, d = kt.shape
    hd = RW_HEAD_DIM
    nheads = 4
    width = nheads * hd
    ng = _div(tt, LANES)
    nc = _div(tt, SCAN_CHUNK)
    cpg = LANES // SCAN_CHUNK
    heads = d // hd
    row = pl.BlockSpec((None, LANES, width), lambda b, p, g: (b, g, p))
    tr = pl.BlockSpec((None, width // LANES, 1, LANES, LANES), lambda b, p, g: (b, p, g, 0, 0))
    sq = pl.BlockSpec((None, nheads, cpg, hd, hd), lambda b, p, g: (b, p, g, 0, 0))
    sq_shape = lambda dt: jax.ShapeDtypeStruct((bsz, heads, nc, hd, hd), dt)
    m_c, c_c, g_c, rp, y0 = pl.pallas_call(
        functools.partial(_chunk_summary_kernel, rev=rev, nheads=nheads),
        grid=(bsz, _div(d, width), ng),
        in_specs=[row, row, row, row, row, tr, tr, tr],
        out_specs=[sq, sq, sq, row, row],
        out_shape=[sq_shape(BF16), sq_shape(F32), sq_shape(F32),
                   jax.ShapeDtypeStruct((bsz, tt, d), BF16), jax.ShapeDtypeStruct((bsz, tt, d), F32)],
        compiler_params=_cparams("arbitrary", "arbitrary", "arbitrary"),
        name="rwkv_chunk_rev" if rev else "rwkv_chunk_fwd",
    )(kt, rt, bh, kh, v, bt, kbt, gc)
    sq_all = pl.BlockSpec((None, nheads, nc, hd, hd), lambda b, p: (b, p, 0, 0, 0))
    row_all = pl.BlockSpec((None, tt, width), lambda b, p: (b, 0, p))
    return pl.pallas_call(
        functools.partial(_state_pass_kernel, rev=rev, ctx_chunks=l_len // SCAN_CHUNK, n_chunks=nc,
                          nheads=nheads),
        grid=(bsz, _div(d, width)),
        in_specs=[sq_all, sq_all, sq_all, row_all, row_all],
        out_specs=row_all,
        out_shape=jax.ShapeDtypeStruct((bsz, tt, d), F32),
        scratch_shapes=[pltpu.VMEM((nheads, hd, hd), F32)],
        compiler_params=_cparams("arbitrary", "arbitrary"),
        name="rwkv_state_rev" if rev else "rwkv_state_fwd",
    )(m_c, c_c, g_c, rp, y0)


def _readout_kernel(y0_ref, y1_ref, bc_ref, v_ref, g_ref, lng_ref, lnb_ref, o_ref):
    lane = lax.broadcasted_iota(jnp.int32, (1, LANES), 1)
    head0 = lane < RW_HEAD_DIM
    y = y0_ref[...] + y1_ref[...]
    inv_n = 1.0 / RW_HEAD_DIM
    mean = _head_sum(y, head0) * inv_n
    yc = y - mean
    var = _head_sum(yc * yc, head0) * inv_n
    yn = yc * lax.rsqrt(var + RW_GN_EPS) * lng_ref[...] + lnb_ref[...]
    o_ref[...] = ((yn + bc_ref[...] * v_ref[...]) * g_ref[...]).astype(o_ref.dtype)


def _rw_readout(y0, y1, bc, v, g, ln_g, ln_b, *, bsz, t_len, l_len):
    d = v.shape[1]
    tm = 256
    per_b = t_len // tm
    off = l_len // tm
    scan_blk = pl.BlockSpec((None, tm, LANES), lambda i, p: (i // per_b, off + i % per_b, p))
    tile = pl.BlockSpec((tm, LANES), lambda i, p: (i, p))
    vec = pl.BlockSpec((1, LANES), lambda i, p: (0, p))
    return pl.pallas_call(
        _readout_kernel,
        grid=(_div(bsz * t_len, tm), _div(d, LANES)),
        in_specs=[scan_blk, scan_blk, scan_blk, tile, tile, vec, vec],
        out_specs=tile,
        out_shape=jax.ShapeDtypeStruct((bsz * t_len, d), BF16),
        compiler_params=_cparams("arbitrary", "arbitrary"),
        name="rwkv_readout",
    )(y0, y1, bc, v, g, ln_g.reshape(1, d), ln_b.reshape(1, d))


def _pad_cols(w, n):
    return jnp.pad(w, ((0, 0), (0, n - w.shape[1])))


def _pad_rows(w, n):
    return jnp.pad(w, ((0, n - w.shape[0]), (0, 0)))


def kernel(x, c, ctx, c_ctx, norm1_g, norm2_g, w_mod, b_mod, ffn_w_gate, ffn_w_up, ffn_conv_w, ffn_conv_b, ffn_w_down, final_norm_g, pool_w, pool_b, pool_scale, na_w_qkv, na_rpb, na_w_o, sg_w_in, sg_b_in, sg_norm_g, sg_w_s, sg_b_s, sg_w_o, rw_mu, rw_w_rkv, rw_w0, rw_w1, rw_w2, rw_a0, rw_a1, rw_a2, rw_g1, rw_g2, rw_k_k, rw_k_a, rw_r_k, rw_ln_g, rw_ln_b, rw_w_o):
    bsz, t_len, d = x.shape
    l_len = ctx.shape[1]
    depth = norm1_g.shape[0]
    n_mixers = 4
    n_lat = bsz * t_len
    n_all = n_lat + bsz * l_len
    f = ffn_w_gate.shape[2]
    f_pad = -(-f // 512) * 512

    cvec = jnp.concatenate([c, c_ctx[None], jnp.zeros((SUBLANES - bsz - 1, d), F32)], axis=0)
    mods = _modulation(cvec, w_mod, b_mod)
    h = jnp.concatenate([x.reshape(n_lat, d), ctx.reshape(bsz * l_len, d)], axis=0)
    seg = dict(seq=t_len, nseg=bsz)
    edges = dict(n_lat_rows=n_lat, seq_lat=t_len, seq_ctx=l_len)

    for i in range(depth):
        m_kind, j = i % n_mixers, i // n_mixers
        last = i == depth - 1
        mod = mods[i].reshape(SUBLANES * 6, 1, d)
        rows = n_lat if last else n_all
        if m_kind == 0:
            u = _norm(h, norm1_g[i], mod, 0, 1, t_len, bsz, F32, rows=rows)
            pw = pool_w[j].astype(BF16)
            h = _pool_mix(u, pw, pool_b[j], pool_scale[j], h, mod, seq=t_len, row0=0, nseq=bsz,
                          seg0=lambda s: s)
            if not last:
                h = _pool_mix(u, pw, pool_b[j], pool_scale[j], h, mod, seq=l_len, row0=n_lat, nseq=bsz,
                              seg0=lambda s: bsz)
        elif m_kind == 1:
            u = _norm(h, norm1_g[i], mod, 0, 1, t_len, bsz, BF16)
            qkv = _matmul(u, na_w_qkv[j].astype(BF16))
            o = _neighbourhood_attention(qkv, na_rpb[j], bsz=bsz, t_len=t_len, l_len=l_len)
            h = _matmul(o, na_w_o[j].astype(BF16), res=h, mod=mod, k_gate=2, rows=rows, **seg)
        elif m_kind == 2:
            u = _norm(h, norm1_g[i], mod, 0, 1, t_len, bsz, BF16, rows=rows)
            z = _matmul(u, sg_w_in[j].astype(BF16), bias=sg_b_in[j], act="gelu", out_dtype=BF16)
            gated = _spatial_gate(z, sg_norm_g[j], sg_w_s[j], sg_b_s[j])
            h = _matmul(gated, sg_w_o[j].astype(BF16), res=h, mod=mod, k_gate=2, rows=rows, **seg)
        else:
            lora = LANES
            u = _norm(h, norm1_g[i], mod, 0, 1, t_len, bsz, F32)
            xr, xw, xk, xv, xa, xg = _token_shift(u, rw_mu[j], **edges)
            w_rkv = rw_w_rkv[j].astype(BF16)
            r = _matmul(xr, w_rkv[0])
            k = _matmul(xk, w_rkv[1])
            v = _matmul(xv, w_rkv[2])
            w1 = jnp.concatenate([_pad_cols(rw_w1[j, e], lora) for e in range(2)], axis=1).astype(BF16)
            a1 = jnp.concatenate([_pad_cols(rw_a1[j, e], lora) for e in range(2)], axis=1).astype(BF16)
            w2 = jnp.stack([_pad_rows(rw_w2[j, e], lora) for e in range(2)]).astype(BF16)
            a2 = jnp.stack([_pad_rows(rw_a2[j, e], lora) for e in range(2)]).astype(BF16)
            hw = _matmul(xw, w1, act="tanh", out_dtype=BF16)
            ha = _matmul(xa, a1, out_dtype=BF16)
            hg = _matmul(xg, rw_g1[j].astype(BF16), act="sigmoid", out_dtype=BF16, rows=n_lat)
            g = _matmul(hg, rw_g2[j].astype(BF16))
            prep = _rw_prep(k, r, v, hw, ha, w2, a2, rw_w0[j], rw_a0[j], rw_k_k[j], rw_k_a[j], rw_r_k[j],
                            bsz=bsz, t_len=t_len, l_len=l_len, lora=lora)
            v_b, bc = prep[0], prep[1]
            y0 = _rw_scan(prep[2:9], v_b, rev=False, l_len=l_len)
            y1 = _rw_scan(prep[9:16], v_b, rev=True, l_len=l_len)
            o = _rw_readout(y0, y1, bc, v, g, rw_ln_g[j], rw_ln_b[j], bsz=bsz, t_len=t_len, l_len=l_len)
            h = _matmul(o, rw_w_o[j].astype(BF16), res=h, mod=mod, k_gate=2, rows=n_lat, **seg)
        u2 = _norm(h, norm2_g[i], mod, 3, 4, t_len, bsz, BF16, rows=rows)
        wg = _pad_cols(ffn_w_gate[i], f_pad).astype(BF16)
        wu = _pad_cols(ffn_w_up[i], f_pad).astype(BF16)
        cw = _pad_cols(ffn_conv_w[i], f_pad)
        cb = _pad_cols(ffn_conv_b[i][None], f_pad)
        wd = _pad_rows(ffn_w_down[i], f_pad).astype(BF16)
        mid = _ffn1(u2, wg, wu, cw, cb, rows=rows, **edges)
        h = _matmul(mid, wd, res=h, mod=mod, k_gate=5, rows=rows, tm=512, **seg)

    out = _norm(h, final_norm_g, None, 0, 0, t_len, bsz, F32, rows=n_lat)
    return out.reshape(bsz, t_len, d)
```

```python
import functools
import math

import jax
import jax.numpy as jnp
from jax import lax
from jax.experimental import pallas as pl
from jax.experimental.pallas import tpu as pltpu

F32 = jnp.float32
BF16 = jnp.bfloat16

GRID_W = 64
NORM_EPS = 1e-6
POOL_WINDOWS = (2, 4, 8, 16)
NA_HEAD_DIM = 64
NA_KH = 8
NA_KW = 16
ROPE_BASE = 10000.0
SG_CHUNK = 128
SG_GROUPS = 16
RW_HEAD_DIM = 64
RW_GN_EPS = 64e-5

LANES = 128
SUBLANES = 8
VMEM_LIMIT = 56 * 1024 * 1024

SCAN_CHUNK = 64
SOLVE_BLOCK = 16


def _cparams(*sem):
    return pltpu.CompilerParams(dimension_semantics=sem, vmem_limit_bytes=VMEM_LIMIT)


def _div(a, b):
    assert a % b == 0, (a, b)
    return a // b


def _nt(a, b):
    return lax.dot_general(a, b, (((1,), (1,)), ((), ())), preferred_element_type=F32)


def _mm(a, b):
    return jnp.dot(a.astype(BF16), b.astype(BF16), preferred_element_type=F32)


def _silu(x):
    return x * jax.nn.sigmoid(x)


def _head_sum(x, lane_is_head0):
    s0 = jnp.sum(jnp.where(lane_is_head0, x, 0.0), axis=-1, keepdims=True)
    s1 = jnp.sum(jnp.where(lane_is_head0, 0.0, x), axis=-1, keepdims=True)
    return jnp.where(lane_is_head0, s0, s1)


def _seq_edges(i, tm, n_lat_rows, seq_lat, seq_ctx):
    row = lax.broadcasted_iota(jnp.int32, (tm, 1), 0)
    grow = i * tm + row
    slen = jnp.where(grow < n_lat_rows, seq_lat, seq_ctx)
    pos = grow & (slen - 1)
    return row, pos == 0, pos == slen - 1


def _shift_rows(x, halo_prev, halo_next, row, first, last, tm):
    xp = pltpu.roll(x, 1, axis=0)
    xp = jnp.where(row == 0, halo_prev[SUBLANES - 1:SUBLANES], xp)
    xp = jnp.where(first, 0.0, xp)
    xn = pltpu.roll(x, tm - 1, axis=0)
    xn = jnp.where(row == tm - 1, halo_next[0:1], xn)
    xn = jnp.where(last, 0.0, xn)
    return xp, xn


def _mod_kernel(c_ref, w_ref, b_ref, o_ref):
    s = _silu(c_ref[...]).astype(BF16)
    o_ref[0] = jnp.dot(s, w_ref[0].astype(BF16), preferred_element_type=F32) + b_ref[0]


def _modulation(cvec, w_mod, b_mod):
    depth, d, n = w_mod.shape
    tn = 1024
    return pl.pallas_call(
        _mod_kernel,
        grid=(depth, _div(n, tn)),
        in_specs=[pl.BlockSpec((SUBLANES, d), lambda l, j: (0, 0)),
                  pl.BlockSpec((1, d, tn), lambda l, j: (l, 0, j)),
                  pl.BlockSpec((1, 1, tn), lambda l, j: (l, 0, j))],
        out_specs=pl.BlockSpec((1, SUBLANES, tn), lambda l, j: (l, 0, j)),
        out_shape=jax.ShapeDtypeStruct((depth, SUBLANES, n), F32),
        compiler_params=_cparams("arbitrary", "arbitrary"),
        name="modulation",
    )(cvec, w_mod, b_mod.reshape(depth, 1, n))


def _norm_kernel(*refs, modulate):
    if modulate:
        h_ref, g_ref, sh_ref, sc_ref, o_ref = refs
    else:
        h_ref, g_ref, o_ref = refs
    x = h_ref[...]
    y = x * lax.rsqrt(jnp.mean(x * x, axis=-1, keepdims=True) + NORM_EPS) * g_ref[...]
    if modulate:
        y = y * (1.0 + sc_ref[0]) + sh_ref[0]
    o_ref[...] = y.astype(o_ref.dtype)


def _norm(h, g, mod, k_shift, k_scale, seq, nseg, out_dtype, rows=None):
    m, d = h.shape if rows is None else (rows, h.shape[1])
    tm = 256
    row_spec = pl.BlockSpec((tm, d), lambda i: (i, 0))
    in_specs = [row_spec, pl.BlockSpec((1, d), lambda i: (0, 0))]
    args = [h, g.reshape(1, d)]
    if mod is not None:
        seg = lambda i: jnp.minimum((i * tm) // seq, nseg)
        in_specs += [pl.BlockSpec((1, 1, d), lambda i: (seg(i) * 6 + k_shift, 0, 0)),
                     pl.BlockSpec((1, 1, d), lambda i: (seg(i) * 6 + k_scale, 0, 0))]
        args += [mod, mod]
    return pl.pallas_call(
        functools.partial(_norm_kernel, modulate=mod is not None),
        grid=(_div(m, tm),),
        in_specs=in_specs,
        out_specs=row_spec,
        out_shape=jax.ShapeDtypeStruct((m, d), out_dtype),
        compiler_params=_cparams("arbitrary"),
        name="rmsnorm_mod",
    )(*args)


def _matmul_kernel(*refs, act, has_bias, has_res):
    x_ref, w_ref = refs[0], refs[1]
    k = 2
    acc = jnp.dot(x_ref[...], w_ref[...], preferred_element_type=F32)
    if has_bias:
        acc = acc + refs[k][...]
        k += 1
    if act == "gelu":
        acc = jax.nn.gelu(acc, approximate=True)
    elif act == "tanh":
        acc = jnp.tanh(acc)
    elif act == "sigmoid":
        acc = jax.nn.sigmoid(acc)
    if has_res:
        acc = refs[k][...] + refs[k + 1][0] * acc
        k += 2
    o_ref = refs[k]
    o_ref[...] = acc.astype(o_ref.dtype)


def _matmul(x, w, *, bias=None, act=None, res=None, mod=None, k_gate=None, seq=None, nseg=None,
            out_dtype=F32, tm=1024, tn=512, rows=None):
    m = x.shape[0] if rows is None else rows
    kdim, n = w.shape
    tn = min(tn, n)
    tm = min(tm, m)
    in_specs = [pl.BlockSpec((tm, kdim), lambda i, j: (i, 0)),
                pl.BlockSpec((kdim, tn), lambda i, j: (0, j))]
    args = [x, w]
    if bias is not None:
        in_specs.append(pl.BlockSpec((1, tn), lambda i, j: (0, j)))
        args.append(bias.reshape(1, n))
    if res is not None:
        seg = lambda i: jnp.minimum((i * tm) // seq, nseg)
        in_specs += [pl.BlockSpec((tm, tn), lambda i, j: (i, j)),
                     pl.BlockSpec((1, 1, tn), lambda i, j: (seg(i) * 6 + k_gate, 0, j))]
        args += [res, mod]
    return pl.pallas_call(
        functools.partial(_matmul_kernel, act=act, has_bias=bias is not None, has_res=res is not None),
        grid=(_div(m, tm), _div(n, tn)),
        in_specs=in_specs,
        out_specs=pl.BlockSpec((tm, tn), lambda i, j: (i, j)),
        out_shape=jax.ShapeDtypeStruct((m, n), out_dtype),
        compiler_params=_cparams("arbitrary", "arbitrary"),
        name="matmul",
    )(*args)


def _ffn1_kernel(x_ref, xp_ref, xn_ref, wg_ref, wu_ref, cw_ref, cb_ref, o_ref, *, tm, n_lat_rows,
                 seq_lat, seq_ctx):
    i = pl.program_id(0)
    x = x_ref[...]
    wg = wg_ref[...]
    g = jnp.dot(x, wg, preferred_element_type=F32)
    gp = jnp.dot(xp_ref[...], wg, preferred_element_type=F32)
    gn = jnp.dot(xn_ref[...], wg, preferred_element_type=F32)
    row, first, last = _seq_edges(i, tm, n_lat_rows, seq_lat, seq_ctx)
    g_prev, g_next = _shift_rows(g, gp, gn, row, first, last, tm)
    gte = g_prev * cw_ref[0:1] + g * cw_ref[1:2] + g_next * cw_ref[2:3] + cb_ref[...]
    up = jnp.dot(x, wu_ref[...], preferred_element_type=F32)
    o_ref[...] = (_silu(gte) * up).astype(o_ref.dtype)


def _halo_specs(tm, kdim, m):
    nb = tm // SUBLANES
    prev = pl.BlockSpec((SUBLANES, kdim), lambda i, j: (jnp.maximum(i * nb - 1, 0), 0))
    nxt = pl.BlockSpec((SUBLANES, kdim), lambda i, j: (jnp.minimum((i + 1) * nb, m // SUBLANES - 1), 0))
    return prev, nxt


def _ffn1(u, wg, wu, cw, cb, *, rows, n_lat_rows, seq_lat, seq_ctx):
    kdim, f = wg.shape
    tm, tn = 1024, 512
    prev, nxt = _halo_specs(tm, kdim, rows)
    return pl.pallas_call(
        functools.partial(_ffn1_kernel, tm=tm, n_lat_rows=n_lat_rows, seq_lat=seq_lat, seq_ctx=seq_ctx),
        grid=(_div(rows, tm), _div(f, tn)),
        in_specs=[pl.BlockSpec((tm, kdim), lambda i, j: (i, 0)), prev, nxt,
                  pl.BlockSpec((kdim, tn), lambda i, j: (0, j)),
                  pl.BlockSpec((kdim, tn), lambda i, j: (0, j)),
                  pl.BlockSpec((3, tn), lambda i, j: (0, j)),
                  pl.BlockSpec((1, tn), lambda i, j: (0, j))],
        out_specs=pl.BlockSpec((tm, tn), lambda i, j: (i, j)),
        out_shape=jax.ShapeDtypeStruct((rows, f), BF16),
        compiler_params=_cparams("arbitrary", "arbitrary"),
        name="ffn_gate_up",
    )(u, u, u, wg, wu, cw, cb)


def _pool_kernel(u_ref, w_ref, b_ref, s_ref, h_ref, gate_ref, o_ref, xp_ref, *, seq):
    grp = pl.program_id(1)
    pad = SUBLANES
    zeros = jnp.zeros((pad, u_ref.shape[1]), F32)
    xp_ref[0:pad] = zeros
    xp_ref[pad:pad + seq] = u_ref[...]
    xp_ref[pad + seq:2 * pad + seq] = zeros
    rt = min(seq, 256)
    for gi, win in enumerate(POOL_WINDOWS):
        @pl.when(grp == gi)
        def _(win=win):
            left, right = win // 2, win - 1 - win // 2
            for rc in range(seq // rt):
                base = pad + rc * rt
                t = rc * rt + lax.broadcasted_iota(jnp.int32, (rt, 1), 0)
                acc = xp_ref[base - left:base - left + rt]
                for j in range(1, win):
                    acc = acc + xp_ref[base - left + j:base - left + j + rt]
                cnt = (jnp.minimum(t + right, seq - 1) - jnp.maximum(t - left, 0) + 1).astype(F32)
                p = acc / cnt - xp_ref[base:base + rt]
                y = jnp.dot(p.astype(BF16), w_ref[0], preferred_element_type=F32) + b_ref[...]
                rows = slice(rc * rt, (rc + 1) * rt)
                o_ref[rows] = h_ref[rows] + gate_ref[0] * (y * s_ref[...])


def _pool_mix(u, w, b, scale, h, mod, *, seq, row0, nseq, seg0):
    d = u.shape[1]
    cg = d // len(POOL_WINDOWS)
    blk0 = row0 // seq
    tile = pl.BlockSpec((seq, cg), lambda s, g: (blk0 + s, g))
    vec = pl.BlockSpec((1, cg), lambda s, g: (0, g))
    return pl.pallas_call(
        functools.partial(_pool_kernel, seq=seq),
        grid=(nseq, len(POOL_WINDOWS)),
        in_specs=[tile, pl.BlockSpec((1, cg, cg), lambda s, g: (g, 0, 0)), vec, vec, tile,
                  pl.BlockSpec((1, 1, cg), lambda s, g: ((seg0(s)) * 6 + 2, 0, g))],
        out_specs=tile,
        out_shape=jax.ShapeDtypeStruct(h.shape, F32),
        scratch_shapes=[pltpu.VMEM((seq + 2 * SUBLANES, cg), F32)],
        input_output_aliases={4: 0},
        compiler_params=_cparams("arbitrary", "arbitrary"),
        name="pool_mix",
    )(u, w, b.reshape(1, d), scale.reshape(1, d), h, mod)


def _na_lat_kernel(q_ref, k_ref, v_ref, kc_ref, vc_ref, cos_ref, sin_ref, bias_ref, o_ref,
                   q_s, k_s, v_s, kc_s, vc_s, *, rows, kh):
    hd = NA_HEAD_DIM
    lane = lax.broadcasted_iota(jnp.int32, (1, LANES), 1)
    low = (lane % 32) < 16
    cos, sin = cos_ref[...], sin_ref[...]

    def rope(x):
        swapped = jnp.where(low, pltpu.roll(x, LANES - 16, axis=1), pltpu.roll(x, 16, axis=1))
        return x * cos + swapped * sin

    q = rope(q_ref[...]) * (hd ** -0.5)
    k = rope(k_ref[...])
    v = v_ref[...]
    kc = kc_ref[...]
    vc = vc_ref[...]
    for hh in range(2):
        ls = slice(hh * hd, (hh + 1) * hd)
        q_s[hh] = q[:, ls].astype(BF16)
        k_s[hh] = k[:, ls].astype(BF16)
        v_s[hh] = v[:, ls].astype(BF16)
        kc_s[hh] = kc[:, ls].astype(BF16)
        vc_s[hh] = vc[:, ls].astype(BF16)
    nloc = kh * GRID_W

    rows_per_step = 4
    dot = lambda a, b: jnp.dot(a, b, preferred_element_type=F32)

    def body(step, carry):
        chains = [(step * rows_per_step + dr, hh) for dr in range(rows_per_step) for hh in range(2)]
        n = range(len(chains))
        rs = [jnp.clip(r - kh // 2, 0, rows - kh) for r, _ in chains]
        q0 = [pl.multiple_of(r * GRID_W, GRID_W) for r, _ in chains]
        k0 = [pl.multiple_of(rs[i] * GRID_W, GRID_W) for i in n]
        hh = [h for _, h in chains]
        qr = [q_s[hh[i], pl.ds(q0[i], GRID_W), :] for i in n]
        s_loc = [_nt(qr[i], k_s[hh[i], pl.ds(k0[i], nloc), :]) + bias_ref[hh[i], chains[i][0] - rs[i]] for i in n]
        s_ctx = [_nt(qr[i], kc_s[hh[i]]) for i in n]
        m = [jnp.maximum(jnp.max(s_loc[i], axis=-1, keepdims=True), jnp.max(s_ctx[i], axis=-1, keepdims=True))
             for i in n]
        p_loc = [jnp.exp(s_loc[i] - m[i]) for i in n]
        p_ctx = [jnp.exp(s_ctx[i] - m[i]) for i in n]
        den = [jnp.sum(p_loc[i], axis=-1, keepdims=True) + jnp.sum(p_ctx[i], axis=-1, keepdims=True) for i in n]
        o = [dot(p_loc[i].astype(BF16), v_s[hh[i], pl.ds(k0[i], nloc), :]) + dot(p_ctx[i].astype(BF16), vc_s[hh[i]])
             for i in n]
        o = [o[i] / den[i] for i in n]
        for dr in range(rows_per_step):
            o_ref[pl.ds(q0[2 * dr], GRID_W), :] = jnp.concatenate(o[2 * dr:2 * dr + 2], axis=1).astype(o_ref.dtype)
        return carry

    lax.fori_loop(0, _div(rows, rows_per_step), body, 0)


def _na_ctx_kernel(q_ref, k_ref, v_ref, o_alias_ref, o_ref):
    del o_alias_ref
    hd = NA_HEAD_DIM
    q = q_ref[...] * (hd ** -0.5)
    k = k_ref[...]
    v = v_ref[...]
    for hh in range(2):
        ls = slice(hh * hd, (hh + 1) * hd)
        s = _nt(q[:, ls].astype(BF16), k[:, ls].astype(BF16))
        m = jnp.max(s, axis=-1, keepdims=True)
        p = jnp.exp(s - m)
        den = jnp.sum(p, axis=-1, keepdims=True)
        o = jnp.dot(p.astype(BF16), v[:, ls].astype(BF16), preferred_element_type=F32)
        o_ref[:, ls] = (o / den).astype(o_ref.dtype)


def _rope_tables(t_len, hd):
    half = hd // 2
    pos = jnp.arange(t_len)
    inv_freq = ROPE_BASE ** (-jnp.arange(0, half, 2, dtype=F32) / half)
    d = jnp.arange(hd)
    p = jnp.where(d[None, :] < half, (pos // GRID_W)[:, None], (pos % GRID_W)[:, None]).astype(F32)
    ang = p * inv_freq[d % (half // 2)][None, :]
    cos = jnp.cos(ang)
    sin = jnp.where((d % half) < half // 2, -jnp.sin(ang), jnp.sin(ang))
    return jnp.tile(cos, (1, 2)), jnp.tile(sin, (1, 2))


def _na_bias_table(rpb, kh):
    col = jnp.arange(GRID_W)
    col_start = jnp.clip(col - NA_KW // 2, 0, GRID_W - NA_KW)
    col_ok = (col[None, :] >= col_start[:, None]) & (col[None, :] < col_start[:, None] + NA_KW)
    dc = jnp.clip(col[None, :] - col[:, None], -(NA_KW - 1), NA_KW - 1) + NA_KW - 1
    rpb_cols = rpb[:, :, dc]
    dr = jnp.arange(kh)[None, :] - jnp.arange(kh)[:, None] + NA_KH - 1
    tab = rpb_cols[:, dr]
    tab = jnp.where(col_ok[None, None, None], tab, -1e30)
    heads = rpb.shape[0]
    return tab.transpose(0, 1, 3, 2, 4).reshape(heads, kh, GRID_W, kh * GRID_W)


def _neighbourhood_attention(qkv, rpb, *, bsz, t_len, l_len):
    d = qkv.shape[1] // 3
    hd = NA_HEAD_DIM
    heads = d // hd
    npair = d // LANES
    rows = t_len // GRID_W
    kh = min(NA_KH, rows)
    n_lat = bsz * t_len
    n_all = qkv.shape[0]
    ctx0 = _div(n_lat, l_len)
    cos, sin = _rope_tables(t_len, hd)
    bias = _na_bias_table(rpb, kh)
    lat_blk = lambda off: pl.BlockSpec((t_len, LANES), lambda b, p: (b, off + p))
    ctx_blk = lambda off: pl.BlockSpec((l_len, LANES), lambda b, p: (ctx0 + b, off + p))
    tab_blk = pl.BlockSpec((t_len, LANES), lambda b, p: (0, 0))
    out = jax.ShapeDtypeStruct((n_all, d), BF16)
    o = pl.pallas_call(
        functools.partial(_na_lat_kernel, rows=rows, kh=kh),
        grid=(bsz, npair),
        in_specs=[lat_blk(0), lat_blk(npair), lat_blk(2 * npair), ctx_blk(npair), ctx_blk(2 * npair),
                  tab_blk, tab_blk,
                  pl.BlockSpec((2, kh, GRID_W, kh * GRID_W), lambda b, p: (p, 0, 0, 0))],
        out_specs=lat_blk(0),
        out_shape=out,
        scratch_shapes=[pltpu.VMEM((2, t_len, hd), BF16), pltpu.VMEM((2, t_len, hd), BF16),
                        pltpu.VMEM((2, t_len, hd), BF16), pltpu.VMEM((2, l_len, hd), BF16),
                        pltpu.VMEM((2, l_len, hd), BF16)],
        compiler_params=_cparams("arbitrary", "arbitrary"),
        name="na_latent",
    )(qkv, qkv, qkv, qkv, qkv, cos, sin, bias)
    return pl.pallas_call(
        _na_ctx_kernel,
        grid=(bsz, npair),
        in_specs=[ctx_blk(0), ctx_blk(npair), ctx_blk(2 * npair), pl.BlockSpec(memory_space=pl.ANY)],
        out_specs=ctx_blk(0),
        out_shape=out,
        input_output_aliases={3: 0},
        compiler_params=_cparams("arbitrary", "arbitrary"),
        name="na_context",
    )(qkv, qkv, qkv, o)


def _sg_kernel(z_u_ref, z_v_ref, g_ref, ws_ref, bs_ref, o_ref, *, nchunk):
    zv = z_v_ref[...].astype(F32)
    zv = zv * lax.rsqrt(jnp.mean(zv * zv, axis=-1, keepdims=True) + NORM_EPS) * g_ref[...]
    zvb = zv.astype(BF16)
    for c in range(nchunk):
        rs = slice(c * SG_CHUNK, (c + 1) * SG_CHUNK)
        for g in range(SG_GROUPS):
            ls = slice(g * LANES, (g + 1) * LANES)
            mixed = jnp.dot(ws_ref[g], zvb[rs, ls], preferred_element_type=F32) + bs_ref[:, ls]
            o_ref[rs, ls] = (z_u_ref[rs, ls].astype(F32) * mixed).astype(o_ref.dtype)


def _spatial_gate(z, norm_g, w_s, b_s):
    m, two_w = z.shape
    width = two_w // 2
    nchunk = 2
    tm = nchunk * SG_CHUNK
    bs_full = jnp.repeat(b_s.T, width // SG_GROUPS, axis=1)
    return pl.pallas_call(
        functools.partial(_sg_kernel, nchunk=nchunk),
        grid=(_div(m, tm),),
        in_specs=[pl.BlockSpec((tm, width), lambda i: (i, 0)),
                  pl.BlockSpec((tm, width), lambda i: (i, 1)),
                  pl.BlockSpec((1, width), lambda i: (0, 0)),
                  pl.BlockSpec((SG_GROUPS, SG_CHUNK, SG_CHUNK), lambda i: (0, 0, 0)),
                  pl.BlockSpec((SG_CHUNK, width), lambda i: (0, 0))],
        out_specs=pl.BlockSpec((tm, width), lambda i: (i, 0)),
        out_shape=jax.ShapeDtypeStruct((m, width), BF16),
        compiler_params=_cparams("arbitrary"),
        name="spatial_gate",
    )(z, z, norm_g.reshape(1, width), w_s.astype(BF16), bs_full)


def _shift_kernel(u_ref, up_ref, un_ref, mu_ref, *o_refs, tm, n_lat_rows, seq_lat, seq_ctx):
    i = pl.program_id(0)
    u = u_ref[...]
    row, first, last = _seq_edges(i, tm, n_lat_rows, seq_lat, seq_ctx)
    u_prev, u_next = _shift_rows(u, up_ref[...], un_ref[...], row, first, last, tm)
    prev = u_prev - u
    nxt = u_next - u
    nmix = len(o_refs)
    for n in range(nmix):
        o_refs[n][...] = (u + prev * mu_ref[n:n + 1] + nxt * mu_ref[nmix + n:nmix + n + 1]).astype(BF16)


def _token_shift(u, mu, *, n_lat_rows, seq_lat, seq_ctx):
    m, d = u.shape
    nmix = mu.shape[1]
    tm, tn = 512, 512
    nb = tm // SUBLANES
    tile = pl.BlockSpec((tm, tn), lambda i, j: (i, j))
    prev = pl.BlockSpec((SUBLANES, tn), lambda i, j: (jnp.maximum(i * nb - 1, 0), j))
    nxt = pl.BlockSpec((SUBLANES, tn), lambda i, j: (jnp.minimum((i + 1) * nb, m // SUBLANES - 1), j))
    return pl.pallas_call(
        functools.partial(_shift_kernel, tm=tm, n_lat_rows=n_lat_rows, seq_lat=seq_lat, seq_ctx=seq_ctx),
        grid=(_div(m, tm), _div(d, tn)),
        in_specs=[tile, prev, nxt, pl.BlockSpec((2 * nmix, tn), lambda i, j: (0, j))],
        out_specs=[tile] * nmix,
        out_shape=[jax.ShapeDtypeStruct((m, d), BF16)] * nmix,
        compiler_params=_cparams("arbitrary", "arbitrary"),
        name="token_shift",
    )(u, u, u, mu.reshape(2 * nmix, d))


def _split2(x):
    hi = x.astype(BF16)
    lo = (x - hi.astype(F32)).astype(BF16)
    return hi, lo


def _rw_prep_kernel(k_ref, r_ref, v_ref, hw_ref, ha_ref, w2_ref, a2_ref, w0_ref, a0_ref, kk_ref, ka_ref,
                    rk_ref, *o_refs, tm, lora):
    (v_o, bc_o, kt0, rt0, bh0, kh0, bt0, kbt0, gc0, kt1, rt1, bh1, kh1, bt1, kbt1, gc1) = o_refs
    per_dir = ((kt0, rt0, bh0, kh0, bt0, kbt0, gc0), (kt1, rt1, bh1, kh1, bt1, kbt1, gc1))
    lane = lax.broadcasted_iota(jnp.int32, (1, LANES), 1)
    head0 = lane < RW_HEAD_DIM
    k = k_ref[...]
    r = r_ref[...]
    v = v_ref[...]
    v_o[...] = v.astype(BF16)
    kkv = k * kk_ref[...]
    kk = kkv * lax.rsqrt(jnp.maximum(_head_sum(kkv * kkv, head0), 1e-12))
    ka = ka_ref[...]
    rk = rk_ref[...]
    hw = hw_ref[...]
    ha = ha_ref[...]
    ri = lax.broadcasted_iota(jnp.int32, (tm, tm), 0)
    ci = lax.broadcasted_iota(jnp.int32, (tm, tm), 1)
    same_chunk = (ri // SCAN_CHUNK) == (ci // SCAN_CHUNK)
    tri = (jnp.where(same_chunk & (ci <= ri), 1.0, 0.0).astype(BF16),
           jnp.where(same_chunk & (ci >= ri), 1.0, 0.0).astype(BF16))
    log_decay, beta, key = [], [], []
    for d in range(2):
        ls = slice(d * lora, (d + 1) * lora)
        pre = w0_ref[d] + jnp.dot(hw[:, ls], w2_ref[d], preferred_element_type=F32)
        z = -pre
        softplus = jnp.maximum(z, 0.0) + jnp.log(1.0 + jnp.exp(-jnp.abs(z)))
        log_decay.append(-jnp.exp(-softplus - 0.5))
        a = jax.nn.sigmoid(a0_ref[d] + jnp.dot(ha[:, ls], a2_ref[d], preferred_element_type=F32))
        beta.append(kk * a)
        key.append(k * (1.0 + (a - 1.0) * ka))
    bc_o[...] = _head_sum(r * key[0] * rk, head0) + _head_sum(r * key[1] * rk, head0)
    for d in range(2):
        ld = log_decay[d]
        pieces = _split2(ld)
        csum = [sum(jnp.dot(tri[e], p, preferred_element_type=F32) for p in pieces) for e in range(2)]
        cum = csum[d]
        total = csum[0] + csum[1] - ld
        kt, rt, bh, kh, bt, kbt, gc = per_dir[d]
        kt[...] = (kk * jnp.exp(cum - ld)).astype(BF16)
        rt[...] = (r * jnp.exp(cum)).astype(BF16)
        inv = jnp.exp(-cum)
        bh[...] = (beta[d] * inv).astype(BF16)
        kh[...] = (key[d] * inv).astype(BF16)
        tail = jnp.exp(total - cum)
        b_t = jnp.transpose(beta[d] * tail)
        k_t = jnp.transpose(key[d] * tail)
        g_t = jnp.transpose(jnp.exp(total))
        for c in range(tm // LANES):
            cs = slice(c * LANES, (c + 1) * LANES)
            bt[c] = b_t[:, cs].astype(BF16)
            kbt[c] = k_t[:, cs].astype(BF16)
            gc[c] = g_t[:, cs]


def _rw_prep(k, r, v, hw, ha, w2, a2, w0, a0, k_k, k_a, r_k, *, bsz, t_len, l_len, lora):
    m, d = k.shape
    tm = 256
    npair = d // LANES
    lat_tiles = bsz * t_len // tm
    per_b = t_len // tm
    tt = t_len + l_len
    nblk = tt // tm

    def out_blk(i):
        is_lat = i < lat_tiles
        b = jnp.where(is_lat, i // per_b, i - lat_tiles)
        blk = jnp.where(is_lat, l_len // tm + i % per_b, 0)
        return b, blk

    tile = pl.BlockSpec((tm, LANES), lambda i, p: (i, p))
    lora_blk = pl.BlockSpec((tm, 2 * lora), lambda i, p: (i, 0))
    wl_blk = pl.BlockSpec((2, lora, LANES), lambda i, p: (0, 0, p))
    v2_blk = pl.BlockSpec((2, 1, LANES), lambda i, p: (0, 0, p))
    v1_blk = pl.BlockSpec((1, LANES), lambda i, p: (0, p))
    row_o = pl.BlockSpec((None, tm, LANES), lambda i, p: (out_blk(i)[0], out_blk(i)[1], p))
    tr_o = pl.BlockSpec((None, None, tm // LANES, LANES, LANES),
                        lambda i, p: (out_blk(i)[0], p, out_blk(i)[1], 0, 0))
    row_bf = jax.ShapeDtypeStruct((bsz, tt, d), BF16)
    row_f = jax.ShapeDtypeStruct((bsz, tt, d), F32)
    tr_bf = jax.ShapeDtypeStruct((bsz, npair, tt // LANES, LANES, LANES), BF16)
    tr_f = jax.ShapeDtypeStruct((bsz, npair, tt // LANES, LANES, LANES), F32)
    dir_specs = [row_o] * 4 + [tr_o] * 3
    dir_shapes = [row_bf] * 4 + [tr_bf, tr_bf, tr_f]
    return pl.pallas_call(
        functools.partial(_rw_prep_kernel, tm=tm, lora=lora),
        grid=(_div(m, tm), npair),
        in_specs=[tile, tile, tile, lora_blk, lora_blk, wl_blk, wl_blk, v2_blk, v2_blk, v1_blk, v1_blk, v1_blk],
        out_specs=[row_o, row_o] + dir_specs + dir_specs,
        out_shape=[row_bf, row_f] + dir_shapes + dir_shapes,
        compiler_params=_cparams("arbitrary", "arbitrary"),
        name="rwkv_prep",
    )(k, r, v, hw, ha, w2, a2, w0.reshape(2, 1, d), a0.reshape(2, 1, d), k_k.reshape(1, d),
      k_a.reshape(1, d), r_k.reshape(1, d))


def _chunk_summary_kernel(kt_ref, rt_ref, bh_ref, kh_ref, v_ref, bt_ref, kbt_ref, gc_ref,
                          m_o, c_o, g_o, rp_o, y0_o, *, rev, nheads):
    c = SCAN_CHUNK
    hd = RW_HEAD_DIM
    ri = lax.broadcasted_iota(jnp.int32, (c, c), 0)
    ci = lax.broadcasted_iota(jnp.int32, (c, c), 1)
    strict = (ci > ri) if rev else (ci < ri)
    incl = (ci >= ri) if rev else (ci <= ri)
    diag_blk = (ri // SOLVE_BLOCK) == (ci // SOLVE_BLOCK)
    eye = jnp.where(ri == ci, 1.0, 0.0)
    chains = [(half, hh) for half in range(LANES // c) for hh in range(nheads)]
    rows = [slice(half * c, (half + 1) * c) for half, _ in chains]
    ls = [slice(hh * hd, (hh + 1) * hd) for _, hh in chains]
    pair = [hh // 2 for _, hh in chains]
    sub = [slice((hh % 2) * hd, (hh % 2 + 1) * hd) for _, hh in chains]
    n = range(len(chains))
    dot = lambda a, b: jnp.dot(a, b, preferred_element_type=F32)
    kt = [kt_ref[rows[i], ls[i]] for i in n]
    rt = [rt_ref[rows[i], ls[i]] for i in n]
    vv = [v_ref[rows[i], ls[i]] for i in n]
    ktrt = [jnp.concatenate([kt[i], rt[i]], axis=0) for i in n]
    pb = [_nt(ktrt[i], bh_ref[rows[i], ls[i]]) for i in n]
    pk = [_nt(ktrt[i], kh_ref[rows[i], ls[i]]) for i in n]
    a1 = [jnp.where(strict, pb[i][:c], 0.0) for i in n]
    a3 = [jnp.where(incl, pb[i][c:], 0.0).astype(BF16) for i in n]
    ad = [jnp.where(diag_blk, a1[i], 0.0) for i in n]
    ao = [(a1[i] - ad[i]).astype(BF16) for i in n]
    tinv = [eye - ad[i] for i in n]
    pw = [_mm(ad[i], ad[i]) for i in n]
    lhs_v = [jnp.concatenate([jnp.where(strict, pk[i][:c], 0.0).astype(BF16),
                              jnp.where(incl, pk[i][c:], 0.0).astype(BF16),
                              kbt_ref[pair[i], 0, sub[i], rows[i]]], axis=0) for i in n]
    pv = [dot(lhs_v[i], vv[i]) for i in n]
    tinv = [tinv[i] + _mm(tinv[i], pw[i]) for i in n]
    for _ in range(int(math.log2(SOLVE_BLOCK)) - 2):
        pw = [_mm(pw[i], pw[i]) for i in n]
        tinv = [tinv[i] + _mm(tinv[i], pw[i]) for i in n]
    rhs = [jnp.concatenate([kt[i].astype(F32), pv[i][:c]], axis=1).astype(BF16) for i in n]
    tinv = [t.astype(BF16) for t in tinv]
    nmat = [dot(tinv[i], ao[i]).astype(BF16) for i in n]
    x1 = [dot(tinv[i], rhs[i]) for i in n]
    x = x1
    for _ in range(c // SOLVE_BLOCK - 1):
        x = [x1[i] - dot(nmat[i], x[i].astype(BF16)) for i in n]
    lhs_x = [jnp.concatenate([bt_ref[pair[i], 0, sub[i], rows[i]], a3[i]], axis=0) for i in n]
    px = [dot(lhs_x[i], x[i].astype(BF16)) for i in n]
    for i, (half, hh) in enumerate(chains):
        gc = gc_ref[pair[i], 0, sub[i], half * c:half * c + 1]
        m_o[hh, half] = (-px[i][:c, :hd]).astype(m_o.dtype)
        c_o[hh, half] = pv[i][2 * c:] - px[i][:c, hd:]
        g_o[hh, half] = jnp.broadcast_to(gc, (hd, hd))
        rp_o[rows[i], ls[i]] = (rt[i].astype(F32) - px[i][c:, :hd]).astype(rp_o.dtype)
        y0_o[rows[i], ls[i]] = pv[i][c:2 * c] - px[i][c:, hd:]


def _state_pass_kernel(m_ref, c_ref, g_ref, rp_ref, y0_ref, y_ref, *, rev, ctx_chunks, n_chunks, nheads):
    c = SCAN_CHUNK
    hd = RW_HEAD_DIM
    heads = range(nheads)
    ls = [slice(hh * hd, (hh + 1) * hd) for hh in heads]

    def step(i, s0):
        if rev:
            ch = jnp.where(i < ctx_chunks, ctx_chunks - 1 - i, n_chunks - 1 - (i - ctx_chunks))
        else:
            ch = i
        r0 = pl.multiple_of(ch * c, c)
        s0b = [s.astype(BF16) for s in s0]
        ms = [jnp.dot(m_ref[hh, ch], s0b[hh], preferred_element_type=F32) for hh in heads]
        ys = [jnp.dot(rp_ref[pl.ds(r0, c), ls[hh]], s0b[hh], preferred_element_type=F32) for hh in heads]
        for hh in heads:
            y_ref[pl.ds(r0, c), ls[hh]] = ys[hh] + y0_ref[pl.ds(r0, c), ls[hh]]
        return tuple(g_ref[hh, ch] * s0[hh] + ms[hh] + c_ref[hh, ch] for hh in heads)

    lax.fori_loop(0, n_chunks, step, tuple(jnp.zeros((hd, hd), F32) for _ in heads))


def _rw_scan(ops, v, *, rev, l_len):
    kt, rt, bh, kh, bt, kbt, gc = ops
    bsz, tt, d = kt.shape
    hd = RW_HEAD_DIM
    ng = _div(tt, LANES)
    nc = _div(tt, SCAN_CHUNK)
    cpg = LANES // SCAN_CHUNK
    heads = d // hd
    nheads = 8
    width = nheads * hd
    row = pl.BlockSpec((None, LANES, width), lambda b, p, g: (b, g, p))
    tr = pl.BlockSpec((None, width // LANES, 1, LANES, LANES), lambda b, p, g: (b, p, g, 0, 0))
    sq = pl.BlockSpec((None, nheads, cpg, hd, hd), lambda b, p, g: (b, p, g, 0, 0))
    sq_shape = lambda dt: jax.ShapeDtypeStruct((bsz, heads, nc, hd, hd), dt)
    m_c, c_c, g_c, rp, y0 = pl.pallas_call(
        functools.partial(_chunk_summary_kernel, rev=rev, nheads=nheads),
        grid=(bsz, _div(d, width), ng),
        in_specs=[row, row, row, row, row, tr, tr, tr],
        out_specs=[sq, sq, sq, row, row],
        out_shape=[sq_shape(BF16), sq_shape(F32), sq_shape(F32),
                   jax.ShapeDtypeStruct((bsz, tt, d), BF16), jax.ShapeDtypeStruct((bsz, tt, d), F32)],
        compiler_params=_cparams("arbitrary", "arbitrary", "arbitrary"),
        name="rwkv_chunk_rev" if rev else "rwkv_chunk_fwd",
    )(kt, rt, bh, kh, v, bt, kbt, gc)
    nheads = 4
    width = nheads * hd
    sq_all = pl.BlockSpec((None, nheads, nc, hd, hd), lambda b, p: (b, p, 0, 0, 0))
    row_all = pl.BlockSpec((None, tt, width), lambda b, p: (b, 0, p))
    return pl.pallas_call(
        functools.partial(_state_pass_kernel, rev=rev, ctx_chunks=l_len // SCAN_CHUNK, n_chunks=nc,
                          nheads=nheads),
        grid=(bsz, _div(d, width)),
        in_specs=[sq_all, sq_all, sq_all, row_all, row_all],
        out_specs=row_all,
        out_shape=jax.ShapeDtypeStruct((bsz, tt, d), F32),
        compiler_params=_cparams("arbitrary", "arbitrary"),
        name="rwkv_state_rev" if rev else "rwkv_state_fwd",
    )(m_c, c_c, g_c, rp, y0)


def _readout_kernel(y0_ref, y1_ref, bc_ref, v_ref, g_ref, lng_ref, lnb_ref, o_ref):
    lane = lax.broadcasted_iota(jnp.int32, (1, LANES), 1)
    head0 = lane < RW_HEAD_DIM
    inv_n = 1.0 / RW_HEAD_DIM
    for cb in range(o_ref.shape[1] // LANES):
        ls = slice(cb * LANES, (cb + 1) * LANES)
        y = y0_ref[:, ls] + y1_ref[:, ls]
        mean = _head_sum(y, head0) * inv_n
        yc = y - mean
        var = _head_sum(yc * yc, head0) * inv_n
        yn = yc * lax.rsqrt(var + RW_GN_EPS) * lng_ref[:, ls] + lnb_ref[:, ls]
        o_ref[:, ls] = ((yn + bc_ref[:, ls] * v_ref[:, ls]) * g_ref[:, ls]).astype(o_ref.dtype)


def _rw_readout(y0, y1, bc, v, g, ln_g, ln_b, *, bsz, t_len, l_len):
    d = v.shape[1]
    tm, tn = 256, 512
    per_b = t_len // tm
    off = l_len // tm
    scan_blk = pl.BlockSpec((None, tm, tn), lambda i, p: (i // per_b, off + i % per_b, p))
    tile = pl.BlockSpec((tm, tn), lambda i, p: (i, p))
    vec = pl.BlockSpec((1, tn), lambda i, p: (0, p))
    return pl.pallas_call(
        _readout_kernel,
        grid=(_div(bsz * t_len, tm), _div(d, tn)),
        in_specs=[scan_blk, scan_blk, scan_blk, tile, tile, vec, vec],
        out_specs=tile,
        out_shape=jax.ShapeDtypeStruct((bsz * t_len, d), BF16),
        compiler_params=_cparams("arbitrary", "arbitrary"),
        name="rwkv_readout",
    )(y0, y1, bc, v, g, ln_g.reshape(1, d), ln_b.reshape(1, d))


def _pad_cols(w, n):
    return jnp.pad(w, ((0, 0), (0, n - w.shape[1])))


def _pad_rows(w, n):
    return jnp.pad(w, ((0, n - w.shape[0]), (0, 0)))


def kernel(x, c, ctx, c_ctx, norm1_g, norm2_g, w_mod, b_mod, ffn_w_gate, ffn_w_up, ffn_conv_w, ffn_conv_b, ffn_w_down, final_norm_g, pool_w, pool_b, pool_scale, na_w_qkv, na_rpb, na_w_o, sg_w_in, sg_b_in, sg_norm_g, sg_w_s, sg_b_s, sg_w_o, rw_mu, rw_w_rkv, rw_w0, rw_w1, rw_w2, rw_a0, rw_a1, rw_a2, rw_g1, rw_g2, rw_k_k, rw_k_a, rw_r_k, rw_ln_g, rw_ln_b, rw_w_o):
    bsz, t_len, d = x.shape
    l_len = ctx.shape[1]
    depth = norm1_g.shape[0]
    n_mixers = 4
    n_lat = bsz * t_len
    n_all = n_lat + bsz * l_len
    f = ffn_w_gate.shape[2]
    f_pad = -(-f // 512) * 512

    cvec = jnp.concatenate([c, c_ctx[None], jnp.zeros((SUBLANES - bsz - 1, d), F32)], axis=0)
    mods = _modulation(cvec, w_mod, b_mod)
    h = jnp.concatenate([x.reshape(n_lat, d), ctx.reshape(bsz * l_len, d)], axis=0)
    seg = dict(seq=t_len, nseg=bsz)
    edges = dict(n_lat_rows=n_lat, seq_lat=t_len, seq_ctx=l_len)

    for i in range(depth):
        m_kind, j = i % n_mixers, i // n_mixers
        last = i == depth - 1
        mod = mods[i].reshape(SUBLANES * 6, 1, d)
        rows = n_lat if last else n_all
        if m_kind == 0:
            u = _norm(h, norm1_g[i], mod, 0, 1, t_len, bsz, F32, rows=rows)
            pw = pool_w[j].astype(BF16)
            h = _pool_mix(u, pw, pool_b[j], pool_scale[j], h, mod, seq=t_len, row0=0, nseq=bsz,
                          seg0=lambda s: s)
            if not last:
                h = _pool_mix(u, pw, pool_b[j], pool_scale[j], h, mod, seq=l_len, row0=n_lat, nseq=bsz,
                              seg0=lambda s: bsz)
        elif m_kind == 1:
            u = _norm(h, norm1_g[i], mod, 0, 1, t_len, bsz, BF16)
            qkv = _matmul(u, na_w_qkv[j].astype(BF16))
            o = _neighbourhood_attention(qkv, na_rpb[j], bsz=bsz, t_len=t_len, l_len=l_len)
            h = _matmul(o, na_w_o[j].astype(BF16), res=h, mod=mod, k_gate=2, rows=rows, **seg)
        elif m_kind == 2:
            u = _norm(h, norm1_g[i], mod, 0, 1, t_len, bsz, BF16, rows=rows)
            z = _matmul(u, sg_w_in[j].astype(BF16), bias=sg_b_in[j], act="gelu", out_dtype=BF16)
            gated = _spatial_gate(z, sg_norm_g[j], sg_w_s[j], sg_b_s[j])
            h = _matmul(gated, sg_w_o[j].astype(BF16), res=h, mod=mod, k_gate=2, rows=rows, **seg)
        else:
            lora = LANES
            u = _norm(h, norm1_g[i], mod, 0, 1, t_len, bsz, F32)
            xr, xw, xk, xv, xa, xg = _token_shift(u, rw_mu[j], **edges)
            w_rkv = rw_w_rkv[j].astype(BF16)
            r = _matmul(xr, w_rkv[0])
            k = _matmul(xk, w_rkv[1])
            v = _matmul(xv, w_rkv[2])
            w1 = jnp.concatenate([_pad_cols(rw_w1[j, e], lora) for e in range(2)], axis=1).astype(BF16)
            a1 = jnp.concatenate([_pad_cols(rw_a1[j, e], lora) for e in range(2)], axis=1).astype(BF16)
            w2 = jnp.stack([_pad_rows(rw_w2[j, e], lora) for e in range(2)]).astype(BF16)
            a2 = jnp.stack([_pad_rows(rw_a2[j, e], lora) for e in range(2)]).astype(BF16)
            hw = _matmul(xw, w1, act="tanh", out_dtype=BF16)
            ha = _matmul(xa, a1, out_dtype=BF16)
            hg = _matmul(xg, rw_g1[j].astype(BF16), act="sigmoid", out_dtype=BF16, rows=n_lat)
            g = _matmul(hg, rw_g2[j].astype(BF16))
            prep = _rw_prep(k, r, v, hw, ha, w2, a2, rw_w0[j], rw_a0[j], rw_k_k[j], rw_k_a[j], rw_r_k[j],
                            bsz=bsz, t_len=t_len, l_len=l_len, lora=lora)
            v_b, bc = prep[0], prep[1]
            y0 = _rw_scan(prep[2:9], v_b, rev=False, l_len=l_len)
            y1 = _rw_scan(prep[9:16], v_b, rev=True, l_len=l_len)
            o = _rw_readout(y0, y1, bc, v, g, rw_ln_g[j], rw_ln_b[j], bsz=bsz, t_len=t_len, l_len=l_len)
            h = _matmul(o, rw_w_o[j].astype(BF16), res=h, mod=mod, k_gate=2, rows=n_lat, **seg)
        u2 = _norm(h, norm2_g[i], mod, 3, 4, t_len, bsz, BF16, rows=rows)
        wg = _pad_cols(ffn_w_gate[i].astype(BF16), f_pad)
        wu = _pad_cols(ffn_w_up[i].astype(BF16), f_pad)
        cw = _pad_cols(ffn_conv_w[i], f_pad)
        cb = _pad_cols(ffn_conv_b[i][None], f_pad)
        wd = _pad_rows(ffn_w_down[i].astype(BF16), f_pad)
        mid = _ffn1(u2, wg, wu, cw, cb, rows=rows, **edges)
        h = _matmul(mid, wd, res=h, mod=mod, k_gate=5, rows=rows, tm=512, **seg)

    out = _norm(h, final_norm_g, None, 0, 0, t_len, bsz, F32, rows=n_lat)
    return out.reshape(bsz, t_len, d)
```

```python
import functools
import math

import jax
import jax.numpy as jnp
from jax import lax
from jax.experimental import pallas as pl
from jax.experimental.pallas import tpu as pltpu

F32 = jnp.float32
BF16 = jnp.bfloat16

GRID_W = 64
NORM_EPS = 1e-6
POOL_WINDOWS = (2, 4, 8, 16)
NA_HEAD_DIM = 64
NA_KH = 8
NA_KW = 16
ROPE_BASE = 10000.0
SG_CHUNK = 128
SG_GROUPS = 16
RW_HEAD_DIM = 64
RW_GN_EPS = 64e-5

LANES = 128
SUBLANES = 8
MXU_WIDTH = 256
VMEM_LIMIT = 56 * 1024 * 1024

SCAN_CHUNK = 64
SOLVE_BLOCK = 16


def _cparams(*sem):
    return pltpu.CompilerParams(dimension_semantics=sem, vmem_limit_bytes=VMEM_LIMIT)


def _div(a, b):
    assert a % b == 0, (a, b)
    return a // b


def _nt(a, b):
    return lax.dot_general(a, b, (((1,), (1,)), ((), ())), preferred_element_type=F32)


def _mm(a, b):
    return jnp.dot(a.astype(BF16), b.astype(BF16), preferred_element_type=F32)


def _silu(x):
    return x * jax.nn.sigmoid(x)


def _head_sum(x, lane_is_head0):
    s0 = jnp.sum(jnp.where(lane_is_head0, x, 0.0), axis=-1, keepdims=True)
    s1 = jnp.sum(jnp.where(lane_is_head0, 0.0, x), axis=-1, keepdims=True)
    return jnp.where(lane_is_head0, s0, s1)


def _seq_edges(i, tm, n_lat_rows, seq_lat, seq_ctx):
    row = lax.broadcasted_iota(jnp.int32, (tm, 1), 0)
    grow = i * tm + row
    slen = jnp.where(grow < n_lat_rows, seq_lat, seq_ctx)
    pos = grow & (slen - 1)
    return row, pos == 0, pos == slen - 1


def _shift_rows(x, halo_prev, halo_next, row, first, last, tm):
    xp = pltpu.roll(x, 1, axis=0)
    xp = jnp.where(row == 0, halo_prev[SUBLANES - 1:SUBLANES], xp)
    xp = jnp.where(first, 0.0, xp)
    xn = pltpu.roll(x, tm - 1, axis=0)
    xn = jnp.where(row == tm - 1, halo_next[0:1], xn)
    xn = jnp.where(last, 0.0, xn)
    return xp, xn


def _mod_kernel(c_ref, w_ref, b_ref, o_ref):
    s = _silu(c_ref[...]).astype(BF16)
    o_ref[0] = jnp.dot(s, w_ref[0].astype(BF16), preferred_element_type=F32) + b_ref[0]


def _modulation(cvec, w_mod, b_mod):
    depth, d, n = w_mod.shape
    tn = 1024
    return pl.pallas_call(
        _mod_kernel,
        grid=(depth, _div(n, tn)),
        in_specs=[pl.BlockSpec((SUBLANES, d), lambda l, j: (0, 0)),
                  pl.BlockSpec((1, d, tn), lambda l, j: (l, 0, j)),
                  pl.BlockSpec((1, 1, tn), lambda l, j: (l, 0, j))],
        out_specs=pl.BlockSpec((1, SUBLANES, tn), lambda l, j: (l, 0, j)),
        out_shape=jax.ShapeDtypeStruct((depth, SUBLANES, n), F32),
        compiler_params=_cparams("arbitrary", "arbitrary"),
        name="modulation",
    )(cvec, w_mod, b_mod.reshape(depth, 1, n))


def _norm_kernel(*refs, modulate):
    if modulate:
        h_ref, g_ref, sh_ref, sc_ref, o_ref = refs
    else:
        h_ref, g_ref, o_ref = refs
    x = h_ref[...]
    y = x * lax.rsqrt(jnp.mean(x * x, axis=-1, keepdims=True) + NORM_EPS) * g_ref[...]
    if modulate:
        y = y * (1.0 + sc_ref[0]) + sh_ref[0]
    o_ref[...] = y.astype(o_ref.dtype)


def _norm(h, g, mod, k_shift, k_scale, seq, nseg, out_dtype, rows=None):
    m, d = h.shape if rows is None else (rows, h.shape[1])
    tm = 256
    row_spec = pl.BlockSpec((tm, d), lambda i: (i, 0))
    in_specs = [row_spec, pl.BlockSpec((1, d), lambda i: (0, 0))]
    args = [h, g.reshape(1, d)]
    if mod is not None:
        seg = lambda i: jnp.minimum((i * tm) // seq, nseg)
        in_specs += [pl.BlockSpec((1, 1, d), lambda i: (seg(i) * 6 + k_shift, 0, 0)),
                     pl.BlockSpec((1, 1, d), lambda i: (seg(i) * 6 + k_scale, 0, 0))]
        args += [mod, mod]
    return pl.pallas_call(
        functools.partial(_norm_kernel, modulate=mod is not None),
        grid=(_div(m, tm),),
        in_specs=in_specs,
        out_specs=row_spec,
        out_shape=jax.ShapeDtypeStruct((m, d), out_dtype),
        compiler_params=_cparams("arbitrary"),
        name="rmsnorm_mod",
    )(*args)


def _matmul_kernel(*refs, act, has_bias, has_res):
    x_ref, w_ref = refs[0], refs[1]
    k = 2
    acc = jnp.dot(x_ref[...], w_ref[...], preferred_element_type=F32)
    if has_bias:
        acc = acc + refs[k][...]
        k += 1
    if act == "gelu":
        acc = jax.nn.gelu(acc, approximate=True)
    elif act == "tanh":
        acc = jnp.tanh(acc)
    elif act == "sigmoid":
        acc = jax.nn.sigmoid(acc)
    if has_res:
        acc = refs[k][...] + refs[k + 1][0] * acc
        k += 2
    o_ref = refs[k]
    o_ref[...] = acc.astype(o_ref.dtype)


def _matmul(x, w, *, bias=None, act=None, res=None, mod=None, k_gate=None, seq=None, nseg=None,
            out_dtype=F32, tm=1024, tn=512, rows=None, w_lead=()):
    m = x.shape[0] if rows is None else rows
    kdim, n = w.shape[-2:]
    tn = min(tn, n)
    tm = min(tm, m)
    in_specs = [pl.BlockSpec((tm, kdim), lambda i, j: (i, 0)),
                pl.BlockSpec((None,) * len(w_lead) + (kdim, tn), lambda i, j: (*w_lead, 0, j))]
    args = [x, w]
    if bias is not None:
        in_specs.append(pl.BlockSpec((1, tn), lambda i, j: (0, j)))
        args.append(bias.reshape(1, n))
    if res is not None:
        seg = lambda i: jnp.minimum((i * tm) // seq, nseg)
        in_specs += [pl.BlockSpec((tm, tn), lambda i, j: (i, j)),
                     pl.BlockSpec((1, 1, tn), lambda i, j: (seg(i) * 6 + k_gate, 0, j))]
        args += [res, mod]
    return pl.pallas_call(
        functools.partial(_matmul_kernel, act=act, has_bias=bias is not None, has_res=res is not None),
        grid=(_div(m, tm), _div(n, tn)),
        in_specs=in_specs,
        out_specs=pl.BlockSpec((tm, tn), lambda i, j: (i, j)),
        out_shape=jax.ShapeDtypeStruct((m, n), out_dtype),
        compiler_params=_cparams("arbitrary", "arbitrary"),
        name="matmul",
    )(*args)


def _ffn1_kernel(x_ref, xp_ref, xn_ref, wg_ref, wu_ref, cw_ref, cb_ref, o_ref, *, tm, n_lat_rows,
                 seq_lat, seq_ctx):
    i = pl.program_id(0)
    x = x_ref[...]
    wg = wg_ref[...]
    g = jnp.dot(x, wg, preferred_element_type=F32)
    gp = jnp.dot(xp_ref[...], wg, preferred_element_type=F32)
    gn = jnp.dot(xn_ref[...], wg, preferred_element_type=F32)
    row, first, last = _seq_edges(i, tm, n_lat_rows, seq_lat, seq_ctx)
    g_prev, g_next = _shift_rows(g, gp, gn, row, first, last, tm)
    gte = g_prev * cw_ref[0:1] + g * cw_ref[1:2] + g_next * cw_ref[2:3] + cb_ref[...]
    up = jnp.dot(x, wu_ref[...], preferred_element_type=F32)
    o_ref[...] = (_silu(gte) * up).astype(o_ref.dtype)


def _halo_specs(tm, kdim, m):
    nb = tm // SUBLANES
    prev = pl.BlockSpec((SUBLANES, kdim), lambda i, j: (jnp.maximum(i * nb - 1, 0), 0))
    nxt = pl.BlockSpec((SUBLANES, kdim), lambda i, j: (jnp.minimum((i + 1) * nb, m // SUBLANES - 1), 0))
    return prev, nxt


def _ffn1(u, wg, wu, cw, cb, *, rows, n_lat_rows, seq_lat, seq_ctx):
    kdim, f = wg.shape
    tm, tn = 1024, 512
    prev, nxt = _halo_specs(tm, kdim, rows)
    return pl.pallas_call(
        functools.partial(_ffn1_kernel, tm=tm, n_lat_rows=n_lat_rows, seq_lat=seq_lat, seq_ctx=seq_ctx),
        grid=(_div(rows, tm), _div(f, tn)),
        in_specs=[pl.BlockSpec((tm, kdim), lambda i, j: (i, 0)), prev, nxt,
                  pl.BlockSpec((kdim, tn), lambda i, j: (0, j)),
                  pl.BlockSpec((kdim, tn), lambda i, j: (0, j)),
                  pl.BlockSpec((3, tn), lambda i, j: (0, j)),
                  pl.BlockSpec((1, tn), lambda i, j: (0, j))],
        out_specs=pl.BlockSpec((tm, tn), lambda i, j: (i, j)),
        out_shape=jax.ShapeDtypeStruct((rows, f), BF16),
        compiler_params=_cparams("arbitrary", "arbitrary"),
        name="ffn_gate_up",
    )(u, u, u, wg, wu, cw, cb)


def _pool_kernel(u_ref, w_ref, b_ref, s_ref, h_ref, gate_ref, o_ref, xp_ref, *, seq):
    grp = pl.program_id(1)
    pad = SUBLANES
    zeros = jnp.zeros((pad, u_ref.shape[1]), F32)
    xp_ref[0:pad] = zeros
    xp_ref[pad:pad + seq] = u_ref[...]
    xp_ref[pad + seq:2 * pad + seq] = zeros
    rt = min(seq, 256)
    for gi, win in enumerate(POOL_WINDOWS):
        @pl.when(grp == gi)
        def _(win=win):
            left, right = win // 2, win - 1 - win // 2
            for rc in range(seq // rt):
                base = pad + rc * rt
                t = rc * rt + lax.broadcasted_iota(jnp.int32, (rt, 1), 0)
                acc = xp_ref[base - left:base - left + rt]
                for j in range(1, win):
                    acc = acc + xp_ref[base - left + j:base - left + j + rt]
                cnt = (jnp.minimum(t + right, seq - 1) - jnp.maximum(t - left, 0) + 1).astype(F32)
                p = acc / cnt - xp_ref[base:base + rt]
                y = jnp.dot(p.astype(BF16), w_ref[0], preferred_element_type=F32) + b_ref[...]
                rows = slice(rc * rt, (rc + 1) * rt)
                o_ref[rows] = h_ref[rows] + gate_ref[0] * (y * s_ref[...])


def _pool_mix(u, w, b, scale, h, mod, *, seq, row0, nseq, seg0):
    d = u.shape[1]
    cg = d // len(POOL_WINDOWS)
    blk0 = row0 // seq
    tile = pl.BlockSpec((seq, cg), lambda s, g: (blk0 + s, g))
    vec = pl.BlockSpec((1, cg), lambda s, g: (0, g))
    return pl.pallas_call(
        functools.partial(_pool_kernel, seq=seq),
        grid=(nseq, len(POOL_WINDOWS)),
        in_specs=[tile, pl.BlockSpec((1, cg, cg), lambda s, g: (g, 0, 0)), vec, vec, tile,
                  pl.BlockSpec((1, 1, cg), lambda s, g: ((seg0(s)) * 6 + 2, 0, g))],
        out_specs=tile,
        out_shape=jax.ShapeDtypeStruct(h.shape, F32),
        scratch_shapes=[pltpu.VMEM((seq + 2 * SUBLANES, cg), F32)],
        input_output_aliases={4: 0},
        compiler_params=_cparams("arbitrary", "arbitrary"),
        name="pool_mix",
    )(u, w, b.reshape(1, d), scale.reshape(1, d), h, mod)


def _na_lat_kernel(q_ref, k_ref, v_ref, kc_ref, vc_ref, cos_ref, sin_ref, bias_ref, o_ref,
                   q_s, k_s, v_s, kc_s, vc_s, *, rows, kh):
    hd = NA_HEAD_DIM
    lane = lax.broadcasted_iota(jnp.int32, (1, LANES), 1)
    low = (lane % 32) < 16
    cos, sin = cos_ref[...], sin_ref[...]

    def rope(x):
        swapped = jnp.where(low, pltpu.roll(x, LANES - 16, axis=1), pltpu.roll(x, 16, axis=1))
        return x * cos + swapped * sin

    q = rope(q_ref[...].astype(F32)) * (hd ** -0.5)
    k = rope(k_ref[...].astype(F32))
    v = v_ref[...]
    kc = kc_ref[...]
    vc = vc_ref[...]
    for hh in range(2):
        ls = slice(hh * hd, (hh + 1) * hd)
        q_s[hh] = q[:, ls].astype(BF16)
        k_s[hh] = k[:, ls].astype(BF16)
        v_s[hh] = v[:, ls].astype(BF16)
        kc_s[hh] = kc[:, ls].astype(BF16)
        vc_s[hh] = vc[:, ls].astype(BF16)
    nloc = kh * GRID_W

    rows_per_step = 8
    dot = lambda a, b: jnp.dot(a, b, preferred_element_type=F32)

    def body(step, carry):
        chains = [(step * rows_per_step + dr, hh) for dr in range(rows_per_step) for hh in range(2)]
        n = range(len(chains))
        rs = [jnp.clip(r - kh // 2, 0, rows - kh) for r, _ in chains]
        q0 = [pl.multiple_of(r * GRID_W, GRID_W) for r, _ in chains]
        k0 = [pl.multiple_of(rs[i] * GRID_W, GRID_W) for i in n]
        hh = [h for _, h in chains]
        qr = [q_s[hh[i], pl.ds(q0[i], GRID_W), :] for i in n]
        s_loc = [_nt(qr[i], k_s[hh[i], pl.ds(k0[i], nloc), :]) + bias_ref[hh[i], chains[i][0] - rs[i]] for i in n]
        s_ctx = [_nt(qr[i], kc_s[hh[i]]) for i in n]
        m = [jnp.maximum(jnp.max(s_loc[i], axis=-1, keepdims=True), jnp.max(s_ctx[i], axis=-1, keepdims=True))
             for i in n]
        p_loc = [jnp.exp(s_loc[i] - m[i]) for i in n]
        p_ctx = [jnp.exp(s_ctx[i] - m[i]) for i in n]
        den = [jnp.sum(p_loc[i], axis=-1, keepdims=True) + jnp.sum(p_ctx[i], axis=-1, keepdims=True) for i in n]
        o = [dot(p_loc[i].astype(BF16), v_s[hh[i], pl.ds(k0[i], nloc), :]) + dot(p_ctx[i].astype(BF16), vc_s[hh[i]])
             for i in n]
        o = [o[i] / den[i] for i in n]
        for dr in range(rows_per_step):
            o_ref[pl.ds(q0[2 * dr], GRID_W), :] = jnp.concatenate(o[2 * dr:2 * dr + 2], axis=1).astype(o_ref.dtype)
        return carry

    lax.fori_loop(0, _div(rows, rows_per_step), body, 0)


def _na_ctx_kernel(q_ref, k_ref, v_ref, o_alias_ref, o_ref):
    del o_alias_ref
    hd = NA_HEAD_DIM
    q = q_ref[...] * (hd ** -0.5)
    k = k_ref[...]
    v = v_ref[...]
    for hh in range(2):
        ls = slice(hh * hd, (hh + 1) * hd)
        s = _nt(q[:, ls].astype(BF16), k[:, ls].astype(BF16))
        m = jnp.max(s, axis=-1, keepdims=True)
        p = jnp.exp(s - m)
        den = jnp.sum(p, axis=-1, keepdims=True)
        o = jnp.dot(p.astype(BF16), v[:, ls].astype(BF16), preferred_element_type=F32)
        o_ref[:, ls] = (o / den).astype(o_ref.dtype)


def _rope_tables(t_len, hd):
    half = hd // 2
    pos = jnp.arange(t_len)
    inv_freq = ROPE_BASE ** (-jnp.arange(0, half, 2, dtype=F32) / half)
    d = jnp.arange(hd)
    p = jnp.where(d[None, :] < half, (pos // GRID_W)[:, None], (pos % GRID_W)[:, None]).astype(F32)
    ang = p * inv_freq[d % (half // 2)][None, :]
    cos = jnp.cos(ang)
    sin = jnp.where((d % half) < half // 2, -jnp.sin(ang), jnp.sin(ang))
    return jnp.tile(cos, (1, 2)), jnp.tile(sin, (1, 2))


def _na_bias_table(rpb, kh):
    col = jnp.arange(GRID_W)
    col_start = jnp.clip(col - NA_KW // 2, 0, GRID_W - NA_KW)
    col_ok = (col[None, :] >= col_start[:, None]) & (col[None, :] < col_start[:, None] + NA_KW)
    dc = jnp.clip(col[None, :] - col[:, None], -(NA_KW - 1), NA_KW - 1) + NA_KW - 1
    rpb_cols = rpb[:, :, dc]
    dr = jnp.arange(kh)[None, :] - jnp.arange(kh)[:, None] + NA_KH - 1
    tab = rpb_cols[:, dr]
    tab = jnp.where(col_ok[None, None, None], tab, -1e30)
    heads = rpb.shape[0]
    return tab.transpose(0, 1, 3, 2, 4).reshape(heads, kh, GRID_W, kh * GRID_W)


def _neighbourhood_attention(qkv, rpb, *, bsz, t_len, l_len):
    d = qkv.shape[1] // 3
    hd = NA_HEAD_DIM
    heads = d // hd
    npair = d // LANES
    rows = t_len // GRID_W
    kh = min(NA_KH, rows)
    n_lat = bsz * t_len
    n_all = qkv.shape[0]
    ctx0 = _div(n_lat, l_len)
    cos, sin = _rope_tables(t_len, hd)
    bias = _na_bias_table(rpb, kh)
    lat_blk = lambda off: pl.BlockSpec((t_len, LANES), lambda b, p: (b, off + p))
    ctx_blk = lambda off: pl.BlockSpec((l_len, LANES), lambda b, p: (ctx0 + b, off + p))
    tab_blk = pl.BlockSpec((t_len, LANES), lambda b, p: (0, 0))
    out = jax.ShapeDtypeStruct((n_all, d), BF16)
    o = pl.pallas_call(
        functools.partial(_na_lat_kernel, rows=rows, kh=kh),
        grid=(bsz, npair),
        in_specs=[lat_blk(0), lat_blk(npair), lat_blk(2 * npair), ctx_blk(npair), ctx_blk(2 * npair),
                  tab_blk, tab_blk,
                  pl.BlockSpec((2, kh, GRID_W, kh * GRID_W), lambda b, p: (p, 0, 0, 0))],
        out_specs=lat_blk(0),
        out_shape=out,
        scratch_shapes=[pltpu.VMEM((2, t_len, hd), BF16), pltpu.VMEM((2, t_len, hd), BF16),
                        pltpu.VMEM((2, t_len, hd), BF16), pltpu.VMEM((2, l_len, hd), BF16),
                        pltpu.VMEM((2, l_len, hd), BF16)],
        compiler_params=_cparams("arbitrary", "arbitrary"),
        name="na_latent",
    )(qkv, qkv, qkv, qkv, qkv, cos, sin, bias)
    return pl.pallas_call(
        _na_ctx_kernel,
        grid=(bsz, npair),
        in_specs=[ctx_blk(0), ctx_blk(npair), ctx_blk(2 * npair), pl.BlockSpec(memory_space=pl.ANY)],
        out_specs=ctx_blk(0),
        out_shape=out,
        input_output_aliases={3: 0},
        compiler_params=_cparams("arbitrary", "arbitrary"),
        name="na_context",
    )(qkv, qkv, qkv, o)


def _sg_kernel(z_u_ref, z_v_ref, g_ref, ws_ref, bs_ref, o_ref, *, nchunk):
    zv = z_v_ref[...].astype(F32)
    zv = zv * lax.rsqrt(jnp.mean(zv * zv, axis=-1, keepdims=True) + NORM_EPS) * g_ref[...]
    zvb = zv.astype(BF16)
    for c in range(nchunk):
        rs = slice(c * SG_CHUNK, (c + 1) * SG_CHUNK)
        for g in range(SG_GROUPS):
            ls = slice(g * LANES, (g + 1) * LANES)
            mixed = jnp.dot(ws_ref[g], zvb[rs, ls], preferred_element_type=F32) + bs_ref[:, ls]
            o_ref[rs, ls] = (z_u_ref[rs, ls].astype(F32) * mixed).astype(o_ref.dtype)


def _spatial_gate(z, norm_g, w_s, b_s):
    m, two_w = z.shape
    width = two_w // 2
    nchunk = 2
    tm = nchunk * SG_CHUNK
    bs_full = jnp.repeat(b_s.T, width // SG_GROUPS, axis=1)
    return pl.pallas_call(
        functools.partial(_sg_kernel, nchunk=nchunk),
        grid=(_div(m, tm),),
        in_specs=[pl.BlockSpec((tm, width), lambda i: (i, 0)),
                  pl.BlockSpec((tm, width), lambda i: (i, 1)),
                  pl.BlockSpec((1, width), lambda i: (0, 0)),
                  pl.BlockSpec((SG_GROUPS, SG_CHUNK, SG_CHUNK), lambda i: (0, 0, 0)),
                  pl.BlockSpec((SG_CHUNK, width), lambda i: (0, 0))],
        out_specs=pl.BlockSpec((tm, width), lambda i: (i, 0)),
        out_shape=jax.ShapeDtypeStruct((m, width), BF16),
        compiler_params=_cparams("arbitrary"),
        name="spatial_gate",
    )(z, z, norm_g.reshape(1, width), w_s.astype(BF16), bs_full)


def _shift_kernel(u_ref, up_ref, un_ref, mu_ref, *o_refs, tm, n_lat_rows, seq_lat, seq_ctx):
    i = pl.program_id(0)
    u = u_ref[...]
    row, first, last = _seq_edges(i, tm, n_lat_rows, seq_lat, seq_ctx)
    u_prev, u_next = _shift_rows(u, up_ref[...], un_ref[...], row, first, last, tm)
    prev = u_prev - u
    nxt = u_next - u
    nmix = len(o_refs)
    for n in range(nmix):
        o_refs[n][...] = (u + prev * mu_ref[n:n + 1] + nxt * mu_ref[nmix + n:nmix + n + 1]).astype(BF16)


def _token_shift(u, mu, *, n_lat_rows, seq_lat, seq_ctx):
    m, d = u.shape
    nmix = mu.shape[1]
    tm, tn = 512, 512
    nb = tm // SUBLANES
    tile = pl.BlockSpec((tm, tn), lambda i, j: (i, j))
    prev = pl.BlockSpec((SUBLANES, tn), lambda i, j: (jnp.maximum(i * nb - 1, 0), j))
    nxt = pl.BlockSpec((SUBLANES, tn), lambda i, j: (jnp.minimum((i + 1) * nb, m // SUBLANES - 1), j))
    return pl.pallas_call(
        functools.partial(_shift_kernel, tm=tm, n_lat_rows=n_lat_rows, seq_lat=seq_lat, seq_ctx=seq_ctx),
        grid=(_div(m, tm), _div(d, tn)),
        in_specs=[tile, prev, nxt, pl.BlockSpec((2 * nmix, tn), lambda i, j: (0, j))],
        out_specs=[tile] * nmix,
        out_shape=[jax.ShapeDtypeStruct((m, d), BF16)] * nmix,
        compiler_params=_cparams("arbitrary", "arbitrary"),
        name="token_shift",
    )(u, u, u, mu.reshape(2 * nmix, d))


def _split2(x):
    hi = x.astype(BF16)
    lo = (x - hi.astype(F32)).astype(BF16)
    return hi, lo


def _rw_prep_kernel(k_ref, r_ref, v_ref, hw_ref, ha_ref, w2_ref, a2_ref, w0_ref, a0_ref, kk_ref, ka_ref,
                    rk_ref, *o_refs, tm, lora):
    (v_o, bc_o, kt0, rt0, bh0, kh0, bt0, kbt0, gc0, kt1, rt1, bh1, kh1, bt1, kbt1, gc1) = o_refs
    per_dir = ((kt0, rt0, bh0, kh0, bt0, kbt0, gc0), (kt1, rt1, bh1, kh1, bt1, kbt1, gc1))
    lane = lax.broadcasted_iota(jnp.int32, (1, LANES), 1)
    head0 = lane < RW_HEAD_DIM
    k = k_ref[...]
    r = r_ref[...]
    v = v_ref[...]
    v_o[...] = v.astype(BF16)
    kkv = k * kk_ref[...]
    kk = kkv * lax.rsqrt(jnp.maximum(_head_sum(kkv * kkv, head0), 1e-12))
    ka = ka_ref[...]
    rk = rk_ref[...]
    hw = hw_ref[...]
    ha = ha_ref[...]
    ri = lax.broadcasted_iota(jnp.int32, (tm, tm), 0)
    ci = lax.broadcasted_iota(jnp.int32, (tm, tm), 1)
    same_chunk = (ri // SCAN_CHUNK) == (ci // SCAN_CHUNK)
    tri = (jnp.where(same_chunk & (ci <= ri), 1.0, 0.0).astype(BF16),
           jnp.where(same_chunk & (ci >= ri), 1.0, 0.0).astype(BF16))
    log_decay, beta, key = [], [], []
    for d in range(2):
        ls = slice(d * lora, (d + 1) * lora)
        pre = w0_ref[d] + jnp.dot(hw[:, ls], w2_ref[d], preferred_element_type=F32)
        z = -pre
        softplus = jnp.maximum(z, 0.0) + jnp.log(1.0 + jnp.exp(-jnp.abs(z)))
        log_decay.append(-jnp.exp(-softplus - 0.5))
        a = jax.nn.sigmoid(a0_ref[d] + jnp.dot(ha[:, ls], a2_ref[d], preferred_element_type=F32))
        beta.append(kk * a)
        key.append(k * (1.0 + (a - 1.0) * ka))
    bc_o[...] = _head_sum(r * key[0] * rk, head0) + _head_sum(r * key[1] * rk, head0)
    for d in range(2):
        ld = log_decay[d]
        pieces = _split2(ld)
        csum = [sum(jnp.dot(tri[e], p, preferred_element_type=F32) for p in pieces) for e in range(2)]
        cum = csum[d]
        total = csum[0] + csum[1] - ld
        kt, rt, bh, kh, bt, kbt, gc = per_dir[d]
        kt[...] = (kk * jnp.exp(cum - ld)).astype(BF16)
        rt[...] = (r * jnp.exp(cum)).astype(BF16)
        inv = jnp.exp(-cum)
        bh[...] = (beta[d] * inv).astype(BF16)
        kh[...] = (key[d] * inv).astype(BF16)
        tail = jnp.exp(total - cum)
        b_t = jnp.transpose(beta[d] * tail)
        k_t = jnp.transpose(key[d] * tail)
        g_t = jnp.transpose(jnp.exp(total))
        for c in range(tm // LANES):
            cs = slice(c * LANES, (c + 1) * LANES)
            bt[c] = b_t[:, cs].astype(BF16)
            kbt[c] = k_t[:, cs].astype(BF16)
            gc[c] = g_t[:, cs]


def _rw_prep(k, r, v, hw, ha, w2, a2, w0, a0, k_k, k_a, r_k, *, bsz, t_len, l_len, lora):
    m, d = k.shape
    tm = 256
    npair = d // LANES
    lat_tiles = bsz * t_len // tm
    per_b = t_len // tm
    tt = t_len + l_len
    nblk = tt // tm

    def out_blk(i):
        is_lat = i < lat_tiles
        b = jnp.where(is_lat, i // per_b, i - lat_tiles)
        blk = jnp.where(is_lat, l_len // tm + i % per_b, 0)
        return b, blk

    tile = pl.BlockSpec((tm, LANES), lambda i, p: (i, p))
    lora_blk = pl.BlockSpec((tm, 2 * lora), lambda i, p: (i, 0))
    wl_blk = pl.BlockSpec((2, lora, LANES), lambda i, p: (0, 0, p))
    v2_blk = pl.BlockSpec((2, 1, LANES), lambda i, p: (0, 0, p))
    v1_blk = pl.BlockSpec((1, LANES), lambda i, p: (0, p))
    row_o = pl.BlockSpec((None, tm, LANES), lambda i, p: (out_blk(i)[0], out_blk(i)[1], p))
    tr_o = pl.BlockSpec((None, None, tm // LANES, LANES, LANES),
                        lambda i, p: (out_blk(i)[0], p, out_blk(i)[1], 0, 0))
    row_bf = jax.ShapeDtypeStruct((bsz, tt, d), BF16)
    row_f = jax.ShapeDtypeStruct((bsz, tt, d), F32)
    tr_bf = jax.ShapeDtypeStruct((bsz, npair, tt // LANES, LANES, LANES), BF16)
    tr_f = jax.ShapeDtypeStruct((bsz, npair, tt // LANES, LANES, LANES), F32)
    dir_specs = [row_o] * 4 + [tr_o] * 3
    dir_shapes = [row_bf] * 4 + [tr_bf, tr_bf, tr_f]
    return pl.pallas_call(
        functools.partial(_rw_prep_kernel, tm=tm, lora=lora),
        grid=(_div(m, tm), npair),
        in_specs=[tile, tile, tile, lora_blk, lora_blk, wl_blk, wl_blk, v2_blk, v2_blk, v1_blk, v1_blk, v1_blk],
        out_specs=[row_o, row_o] + dir_specs + dir_specs,
        out_shape=[row_bf, row_f] + dir_shapes + dir_shapes,
        compiler_params=_cparams("arbitrary", "arbitrary"),
        name="rwkv_prep",
    )(k, r, v, hw, ha, w2, a2, w0.reshape(2, 1, d), a0.reshape(2, 1, d), k_k.reshape(1, d),
      k_a.reshape(1, d), r_k.reshape(1, d))


def _chunk_summary_kernel(kt_ref, rt_ref, bh_ref, kh_ref, v_ref, bt_ref, kbt_ref,
                          m_o, c_o, rp_o, y0_o, *, rev, nheads):
    c = SCAN_CHUNK
    hd = RW_HEAD_DIM
    ri = lax.broadcasted_iota(jnp.int32, (c, c), 0)
    ci = lax.broadcasted_iota(jnp.int32, (c, c), 1)
    strict = (ci > ri) if rev else (ci < ri)
    incl = (ci >= ri) if rev else (ci <= ri)
    diag_blk = (ri // SOLVE_BLOCK) == (ci // SOLVE_BLOCK)
    eye = jnp.where(ri == ci, 1.0, 0.0)
    chains = [(half, hh) for half in range(LANES // c) for hh in range(nheads)]
    rows = [slice(half * c, (half + 1) * c) for half, _ in chains]
    ls = [slice(hh * hd, (hh + 1) * hd) for _, hh in chains]
    pair = [hh // 2 for _, hh in chains]
    sub = [slice((hh % 2) * hd, (hh % 2 + 1) * hd) for _, hh in chains]
    n = range(len(chains))
    dot = lambda a, b: jnp.dot(a, b, preferred_element_type=F32)
    kt = [kt_ref[rows[i], ls[i]] for i in n]
    rt = [rt_ref[rows[i], ls[i]] for i in n]
    vv = [v_ref[rows[i], ls[i]] for i in n]
    ktrt = [jnp.concatenate([kt[i], rt[i]], axis=0) for i in n]
    pb = [_nt(ktrt[i], bh_ref[rows[i], ls[i]]) for i in n]
    pk = [_nt(ktrt[i], kh_ref[rows[i], ls[i]]) for i in n]
    a1 = [jnp.where(strict, pb[i][:c], 0.0) for i in n]
    a3 = [jnp.where(incl, pb[i][c:], 0.0).astype(BF16) for i in n]
    ad = [jnp.where(diag_blk, a1[i], 0.0) for i in n]
    ao = [(a1[i] - ad[i]).astype(BF16) for i in n]
    tinv = [eye - ad[i] for i in n]
    pw = [_mm(ad[i], ad[i]) for i in n]
    lhs_v = [jnp.concatenate([jnp.where(strict, pk[i][:c], 0.0).astype(BF16),
                              jnp.where(incl, pk[i][c:], 0.0).astype(BF16),
                              kbt_ref[pair[i], 0, sub[i], rows[i]]], axis=0) for i in n]
    pv = [dot(lhs_v[i], vv[i]) for i in n]
    tinv = [tinv[i] + _mm(tinv[i], pw[i]) for i in n]
    for _ in range(int(math.log2(SOLVE_BLOCK)) - 2):
        pw = [_mm(pw[i], pw[i]) for i in n]
        tinv = [tinv[i] + _mm(tinv[i], pw[i]) for i in n]
    rhs = [jnp.concatenate([kt[i].astype(F32), pv[i][:c]], axis=1).astype(BF16) for i in n]
    tinv = [t.astype(BF16) for t in tinv]
    nmat = [dot(tinv[i], ao[i]).astype(BF16) for i in n]
    x1 = [dot(tinv[i], rhs[i]) for i in n]
    x = x1
    for _ in range(c // SOLVE_BLOCK - 1):
        x = [x1[i] - dot(nmat[i], x[i].astype(BF16)) for i in n]
    lhs_x = [jnp.concatenate([bt_ref[pair[i], 0, sub[i], rows[i]], a3[i]], axis=0) for i in n]
    px = [dot(lhs_x[i], x[i].astype(BF16)) for i in n]
    for i, (half, hh) in enumerate(chains):
        m_o[hh, 0, :, rows[i]] = (-px[i][:c, :hd]).astype(m_o.dtype)
        c_o[hh, 0, :, rows[i]] = pv[i][2 * c:] - px[i][:c, hd:]
        rp_o[rows[i], ls[i]] = (rt[i].astype(F32) - px[i][c:, :hd]).astype(rp_o.dtype)
        y0_o[rows[i], ls[i]] = pv[i][c:2 * c] - px[i][c:, hd:]


def _state_pass_kernel(mf_ref, cf_ref, gf_ref, mr_ref, cr_ref, gr_ref, sf_o, sr_o, *, ctx_groups, n_groups,
                       nheads):
    c = SCAN_CHUNK
    hd = RW_HEAD_DIM
    heads = range(nheads)
    sf_o[...] = jnp.zeros(sf_o.shape, sf_o.dtype)
    sr_o[...] = jnp.zeros(sr_o.shape, sr_o.dtype)
    dirs = ((mf_ref, cf_ref, gf_ref, sf_o, False), (mr_ref, cr_ref, gr_ref, sr_o, True))

    def group(i, carry):
        out = []
        g_of = [i, jnp.where(i < ctx_groups, ctx_groups - 1 - i, n_groups - 1 - (i - ctx_groups))]
        states = [list(carry[0]), list(carry[1])]
        for step in range(LANES // c):
            s0b, ms = [[], []], [[], []]
            for di, (m_ref, c_ref, g_ref, s_o, rev) in enumerate(dirs):
                half = (LANES // c - 1 - step) if rev else step
                ts = slice(half * c, (half + 1) * c)
                for hh in heads:
                    sb = states[di][hh].astype(BF16)
                    s0b[di].append(sb)
                    blk = slice((hh % 2) * hd, (hh % 2 + 1) * hd)
                    s_o[hh // 2, g_of[di] * (LANES // c) + half, blk, blk] = sb
            for di, (m_ref, c_ref, g_ref, s_o, rev) in enumerate(dirs):
                half = (LANES // c - 1 - step) if rev else step
                ts = slice(half * c, (half + 1) * c)
                ms[di] = [jnp.dot(m_ref[hh, g_of[di], :, ts], s0b[di][hh], preferred_element_type=F32)
                          for hh in heads]
            for di, (m_ref, c_ref, g_ref, s_o, rev) in enumerate(dirs):
                half = (LANES // c - 1 - step) if rev else step
                ts = slice(half * c, (half + 1) * c)
                for hh in heads:
                    blk = slice((hh % 2) * hd, (hh % 2 + 1) * hd)
                    gc = g_ref[hh // 2, g_of[di], blk, half * c:half * c + 1]
                    states[di][hh] = gc * states[di][hh] + ms[di][hh] + c_ref[hh, g_of[di], :, ts]
        return tuple(states[0]), tuple(states[1])

    zero = tuple(jnp.zeros((hd, hd), F32) for _ in heads)
    lax.fori_loop(0, n_groups, group, (zero, zero))


def _chunk_summaries(ops, v, *, rev):
    kt, rt, bh, kh, bt, kbt, _ = ops
    bsz, tt, d = kt.shape
    hd = RW_HEAD_DIM
    ng = _div(tt, LANES)
    heads = d // hd
    nheads = 8
    width = nheads * hd
    row = pl.BlockSpec((None, LANES, width), lambda b, p, g: (b, g, p))
    tr = pl.BlockSpec((None, width // LANES, 1, LANES, LANES), lambda b, p, g: (b, p, g, 0, 0))
    sq = pl.BlockSpec((None, nheads, 1, hd, LANES), lambda b, p, g: (b, p, g, 0, 0))
    sq_shape = lambda dt: jax.ShapeDtypeStruct((bsz, heads, ng, hd, LANES), dt)
    return pl.pallas_call(
        functools.partial(_chunk_summary_kernel, rev=rev, nheads=nheads),
        grid=(bsz, _div(d, width), ng),
        in_specs=[row, row, row, row, row, tr, tr],
        out_specs=[sq, sq, row, row],
        out_shape=[sq_shape(BF16), sq_shape(F32),
                   jax.ShapeDtypeStruct((bsz, tt, d), BF16), jax.ShapeDtypeStruct((bsz, tt, d), F32)],
        compiler_params=_cparams("arbitrary", "arbitrary", "arbitrary"),
        name="rwkv_chunk_rev" if rev else "rwkv_chunk_fwd",
    )(kt, rt, bh, kh, v, bt, kbt)


def _state_pass(m_f, c_f, g_f, m_r, c_r, g_r, *, l_len):
    bsz, heads, ng, hd, _ = m_f.shape
    nheads = 4
    npair = nheads // 2
    nc = ng * (LANES // SCAN_CHUNK)
    sq = pl.BlockSpec((None, nheads, ng, hd, LANES), lambda b, p: (b, p, 0, 0, 0))
    tr = pl.BlockSpec((None, npair, ng, LANES, LANES), lambda b, p: (b, p, 0, 0, 0))
    st = pl.BlockSpec((None, npair, nc, LANES, LANES), lambda b, p: (b, p, 0, 0, 0))
    st_shape = jax.ShapeDtypeStruct((bsz, heads // 2, nc, LANES, LANES), BF16)
    return pl.pallas_call(
        functools.partial(_state_pass_kernel, ctx_groups=l_len // LANES, n_groups=ng, nheads=nheads),
        grid=(bsz, _div(heads, nheads)),
        in_specs=[sq, sq, tr, sq, sq, tr],
        out_specs=[st, st],
        out_shape=[st_shape, st_shape],
        compiler_params=_cparams("arbitrary", "arbitrary"),
        name="rwkv_state",
    )(m_f, c_f, g_f, m_r, c_r, g_r)


def _readout_kernel(rpf_ref, y0f_ref, sf_ref, rpr_ref, y0r_ref, sr_ref, bc_ref, v_ref, g_ref, lng_ref,
                    lnb_ref, o_ref):
    c = SCAN_CHUNK
    lane = lax.broadcasted_iota(jnp.int32, (1, LANES), 1)
    head0 = lane < RW_HEAD_DIM
    inv_n = 1.0 / RW_HEAD_DIM
    tiles = [(cc, pp) for cc in range(o_ref.shape[0] // c) for pp in range(o_ref.shape[1] // LANES)]
    rows = [slice(cc * c, (cc + 1) * c) for cc, _ in tiles]
    ls = [slice(pp * LANES, (pp + 1) * LANES) for _, pp in tiles]
    n = range(len(tiles))
    dot = lambda a, b: jnp.dot(a, b, preferred_element_type=F32)
    yf = [dot(rpf_ref[rows[i], ls[i]], sf_ref[tiles[i][1], tiles[i][0]]) for i in n]
    yr = [dot(rpr_ref[rows[i], ls[i]], sr_ref[tiles[i][1], tiles[i][0]]) for i in n]
    for i in n:
        y = (yf[i] + y0f_ref[rows[i], ls[i]]) + (yr[i] + y0r_ref[rows[i], ls[i]])
        mean = _head_sum(y, head0) * inv_n
        yc = y - mean
        var = _head_sum(yc * yc, head0) * inv_n
        yn = yc * lax.rsqrt(var + RW_GN_EPS) * lng_ref[:, ls[i]] + lnb_ref[:, ls[i]]
        o_ref[rows[i], ls[i]] = ((yn + bc_ref[rows[i], ls[i]] * v_ref[rows[i], ls[i]])
                                 * g_ref[rows[i], ls[i]]).astype(o_ref.dtype)


def _rw_readout(rp_f, y0_f, s_f, rp_r, y0_r, s_r, bc, v, g, ln_g, ln_b, *, bsz, t_len, l_len):
    d = v.shape[1]
    tm, tn = 256, 512
    per_b = t_len // tm
    off = l_len // tm
    blk = lambda i: (i // per_b, off + i % per_b)
    scan_blk = pl.BlockSpec((None, tm, tn), lambda i, p: (*blk(i), p))
    st_blk = pl.BlockSpec((None, tn // LANES, tm // SCAN_CHUNK, LANES, LANES),
                          lambda i, p: (blk(i)[0], p, blk(i)[1], 0, 0))
    tile = pl.BlockSpec((tm, tn), lambda i, p: (i, p))
    vec = pl.BlockSpec((1, tn), lambda i, p: (0, p))
    return pl.pallas_call(
        _readout_kernel,
        grid=(_div(bsz * t_len, tm), _div(d, tn)),
        in_specs=[scan_blk, scan_blk, st_blk, scan_blk, scan_blk, st_blk, scan_blk, tile, tile, vec, vec],
        out_specs=tile,
        out_shape=jax.ShapeDtypeStruct((bsz * t_len, d), BF16),
        compiler_params=_cparams("arbitrary", "arbitrary"),
        name="rwkv_readout",
    )(rp_f, y0_f, s_f, rp_r, y0_r, s_r, bc, v, g, ln_g.reshape(1, d), ln_b.reshape(1, d))


def _pad_cols(w, n):
    return jnp.pad(w, ((0, 0), (0, n - w.shape[1])))


def _pad_rows(w, n):
    return jnp.pad(w, ((0, n - w.shape[0]), (0, 0)))


def kernel(x, c, ctx, c_ctx, norm1_g, norm2_g, w_mod, b_mod, ffn_w_gate, ffn_w_up, ffn_conv_w, ffn_conv_b, ffn_w_down, final_norm_g, pool_w, pool_b, pool_scale, na_w_qkv, na_rpb, na_w_o, sg_w_in, sg_b_in, sg_norm_g, sg_w_s, sg_b_s, sg_w_o, rw_mu, rw_w_rkv, rw_w0, rw_w1, rw_w2, rw_a0, rw_a1, rw_a2, rw_g1, rw_g2, rw_k_k, rw_k_a, rw_r_k, rw_ln_g, rw_ln_b, rw_w_o):
    bsz, t_len, d = x.shape
    l_len = ctx.shape[1]
    depth = norm1_g.shape[0]
    n_mixers = 4
    n_lat = bsz * t_len
    n_all = n_lat + bsz * l_len
    f = ffn_w_gate.shape[2]
    f_pad = -(-f // 512) * 512

    cvec = jnp.concatenate([c, c_ctx[None], jnp.zeros((SUBLANES - bsz - 1, d), F32)], axis=0)
    mods = _modulation(cvec, w_mod, b_mod)
    h = jnp.concatenate([x.reshape(n_lat, d), ctx.reshape(bsz * l_len, d)], axis=0)
    seg = dict(seq=t_len, nseg=bsz)
    edges = dict(n_lat_rows=n_lat, seq_lat=t_len, seq_ctx=l_len)

    for i in range(depth):
        m_kind, j = i % n_mixers, i // n_mixers
        last = i == depth - 1
        mod = mods[i].reshape(SUBLANES * 6, 1, d)
        rows = n_lat if last else n_all
        if m_kind == 0:
            u = _norm(h, norm1_g[i], mod, 0, 1, t_len, bsz, F32, rows=rows)
            pw = pool_w[j].astype(BF16)
            h = _pool_mix(u, pw, pool_b[j], pool_scale[j], h, mod, seq=t_len, row0=0, nseq=bsz,
                          seg0=lambda s: s)
            if not last:
                h = _pool_mix(u, pw, pool_b[j], pool_scale[j], h, mod, seq=l_len, row0=n_lat, nseq=bsz,
                              seg0=lambda s: bsz)
        elif m_kind == 1:
            u = _norm(h, norm1_g[i], mod, 0, 1, t_len, bsz, BF16)
            qkv = _matmul(u, na_w_qkv[j].astype(BF16), out_dtype=BF16)
            o = _neighbourhood_attention(qkv, na_rpb[j], bsz=bsz, t_len=t_len, l_len=l_len)
            h = _matmul(o, na_w_o[j].astype(BF16), res=h, mod=mod, k_gate=2, rows=rows, **seg)
        elif m_kind == 2:
            u = _norm(h, norm1_g[i], mod, 0, 1, t_len, bsz, BF16, rows=rows)
            z = _matmul(u, sg_w_in[j].astype(BF16), bias=sg_b_in[j], act="gelu", out_dtype=BF16)
            gated = _spatial_gate(z, sg_norm_g[j], sg_w_s[j], sg_b_s[j])
            h = _matmul(gated, sg_w_o[j].astype(BF16), res=h, mod=mod, k_gate=2, rows=rows, **seg)
        else:
            lora = LANES
            u = _norm(h, norm1_g[i], mod, 0, 1, t_len, bsz, F32)
            xr, xw, xk, xv, xa, xg = _token_shift(u, rw_mu[j], **edges)
            w_rkv = rw_w_rkv.astype(BF16)
            r = _matmul(xr, w_rkv, w_lead=(j, 0))
            k = _matmul(xk, w_rkv, w_lead=(j, 1))
            v = _matmul(xv, w_rkv, w_lead=(j, 2))
            w1 = jnp.concatenate([_pad_cols(rw_w1[j, e], lora) for e in range(2)], axis=1).astype(BF16)
            a1 = jnp.concatenate([_pad_cols(rw_a1[j, e], lora) for e in range(2)], axis=1).astype(BF16)
            w2 = jnp.stack([_pad_rows(rw_w2[j, e], lora) for e in range(2)]).astype(BF16)
            a2 = jnp.stack([_pad_rows(rw_a2[j, e], lora) for e in range(2)]).astype(BF16)
            hw = _matmul(xw, w1, act="tanh", out_dtype=BF16)
            ha = _matmul(xa, a1, out_dtype=BF16)
            hg = _matmul(xg, rw_g1[j].astype(BF16), act="sigmoid", out_dtype=BF16, rows=n_lat)
            g = _matmul(hg, rw_g2[j].astype(BF16))
            prep = _rw_prep(k, r, v, hw, ha, w2, a2, rw_w0[j], rw_a0[j], rw_k_k[j], rw_k_a[j], rw_r_k[j],
                            bsz=bsz, t_len=t_len, l_len=l_len, lora=lora)
            v_b, bc = prep[0], prep[1]
            ops_f, ops_r = prep[2:9], prep[9:16]
            m_f, c_f, rp_f, y0_f = _chunk_summaries(ops_f, v_b, rev=False)
            m_r, c_r, rp_r, y0_r = _chunk_summaries(ops_r, v_b, rev=True)
            s_f, s_r = _state_pass(m_f, c_f, ops_f[6], m_r, c_r, ops_r[6], l_len=l_len)
            o = _rw_readout(rp_f, y0_f, s_f, rp_r, y0_r, s_r, bc, v, g, rw_ln_g[j], rw_ln_b[j],
                            bsz=bsz, t_len=t_len, l_len=l_len)
            h = _matmul(o, rw_w_o[j].astype(BF16), res=h, mod=mod, k_gate=2, rows=n_lat, **seg)
        u2 = _norm(h, norm2_g[i], mod, 3, 4, t_len, bsz, BF16, rows=rows)
        wg = _pad_cols(ffn_w_gate[i].astype(BF16), f_pad)
        wu = _pad_cols(ffn_w_up[i].astype(BF16), f_pad)
        cw = _pad_cols(ffn_conv_w[i], f_pad)
        cb = _pad_cols(ffn_conv_b[i][None], f_pad)
        wd = _pad_rows(ffn_w_down[i].astype(BF16), f_pad)
        mid = _ffn1(u2, wg, wu, cw, cb, rows=rows, **edges)
        h = _matmul(mid, wd, res=h, mod=mod, k_gate=5, rows=rows, tm=1024, **seg)

    out = _norm(h, final_norm_g, None, 0, 0, t_len, bsz, F32, rows=n_lat)
    return out.reshape(bsz, t_len, d)
```

```python
import functools
import math

import jax
import jax.numpy as jnp
from jax import lax
from jax.experimental import pallas as pl
from jax.experimental.pallas import tpu as pltpu

F32 = jnp.float32
BF16 = jnp.bfloat16

GRID_W = 64
NORM_EPS = 1e-6
POOL_WINDOWS = (2, 4, 8, 16)
NA_HEAD_DIM = 64
NA_KH = 8
NA_KW = 16
ROPE_BASE = 10000.0
SG_CHUNK = 128
SG_GROUPS = 16
RW_HEAD_DIM = 64
RW_GN_EPS = 64e-5

LANES = 128
SUBLANES = 8
MXU_WIDTH = 256
VMEM_LIMIT = 56 * 1024 * 1024

SCAN_CHUNK = 64
SOLVE_BLOCK = 16


def _cparams(*sem):
    return pltpu.CompilerParams(dimension_semantics=sem, vmem_limit_bytes=VMEM_LIMIT)


def _div(a, b):
    assert a % b == 0, (a, b)
    return a // b


def _nt(a, b):
    return lax.dot_general(a, b, (((1,), (1,)), ((), ())), preferred_element_type=F32)


def _mm(a, b):
    return jnp.dot(a.astype(BF16), b.astype(BF16), preferred_element_type=F32)


def _silu(x):
    return x * jax.nn.sigmoid(x)


def _head_sum(x, lane_is_head0):
    s0 = jnp.sum(jnp.where(lane_is_head0, x, 0.0), axis=-1, keepdims=True)
    s1 = jnp.sum(jnp.where(lane_is_head0, 0.0, x), axis=-1, keepdims=True)
    return jnp.where(lane_is_head0, s0, s1)


def _seq_edges(i, tm, n_lat_rows, seq_lat, seq_ctx):
    row = lax.broadcasted_iota(jnp.int32, (tm, 1), 0)
    grow = i * tm + row
    slen = jnp.where(grow < n_lat_rows, seq_lat, seq_ctx)
    pos = grow & (slen - 1)
    return row, pos == 0, pos == slen - 1


def _shift_rows(x, halo_prev, halo_next, row, first, last, tm):
    xp = pltpu.roll(x, 1, axis=0)
    xp = jnp.where(row == 0, halo_prev[SUBLANES - 1:SUBLANES], xp)
    xp = jnp.where(first, 0.0, xp)
    xn = pltpu.roll(x, tm - 1, axis=0)
    xn = jnp.where(row == tm - 1, halo_next[0:1], xn)
    xn = jnp.where(last, 0.0, xn)
    return xp, xn


def _mod_kernel(c_ref, w_ref, b_ref, o_ref):
    s = _silu(c_ref[...]).astype(BF16)
    o_ref[0] = jnp.dot(s, w_ref[0].astype(BF16), preferred_element_type=F32) + b_ref[0]


def _modulation(cvec, w_mod, b_mod):
    depth, d, n = w_mod.shape
    tn = 1024
    return pl.pallas_call(
        _mod_kernel,
        grid=(depth, _div(n, tn)),
        in_specs=[pl.BlockSpec((SUBLANES, d), lambda l, j: (0, 0)),
                  pl.BlockSpec((1, d, tn), lambda l, j: (l, 0, j)),
                  pl.BlockSpec((1, 1, tn), lambda l, j: (l, 0, j))],
        out_specs=pl.BlockSpec((1, SUBLANES, tn), lambda l, j: (l, 0, j)),
        out_shape=jax.ShapeDtypeStruct((depth, SUBLANES, n), F32),
        compiler_params=_cparams("arbitrary", "arbitrary"),
        name="modulation",
    )(cvec, w_mod, b_mod.reshape(depth, 1, n))


def _norm_kernel(*refs, modulate):
    if modulate:
        h_ref, g_ref, sh_ref, sc_ref, o_ref = refs
    else:
        h_ref, g_ref, o_ref = refs
    x = h_ref[...]
    y = x * lax.rsqrt(jnp.mean(x * x, axis=-1, keepdims=True) + NORM_EPS) * g_ref[...]
    if modulate:
        y = y * (1.0 + sc_ref[0]) + sh_ref[0]
    o_ref[...] = y.astype(o_ref.dtype)


def _norm(h, g, mod, k_shift, k_scale, seq, nseg, out_dtype, rows=None):
    m, d = h.shape if rows is None else (rows, h.shape[1])
    tm = 512
    row_spec = pl.BlockSpec((tm, d), lambda i: (i, 0))
    in_specs = [row_spec, pl.BlockSpec((1, d), lambda i: (0, 0))]
    args = [h, g.reshape(1, d)]
    if mod is not None:
        seg = lambda i: jnp.minimum((i * tm) // seq, nseg)
        in_specs += [pl.BlockSpec((1, 1, d), lambda i: (seg(i) * 6 + k_shift, 0, 0)),
                     pl.BlockSpec((1, 1, d), lambda i: (seg(i) * 6 + k_scale, 0, 0))]
        args += [mod, mod]
    return pl.pallas_call(
        functools.partial(_norm_kernel, modulate=mod is not None),
        grid=(_div(m, tm),),
        in_specs=in_specs,
        out_specs=row_spec,
        out_shape=jax.ShapeDtypeStruct((m, d), out_dtype),
        compiler_params=_cparams("arbitrary"),
        name="rmsnorm_mod",
    )(*args)


def _matmul_kernel(*refs, act, has_bias, has_res):
    x_ref, w_ref = refs[0], refs[1]
    k = 2
    acc = jnp.dot(x_ref[...], w_ref[...], preferred_element_type=F32)
    if has_bias:
        acc = acc + refs[k][...]
        k += 1
    if act == "gelu":
        acc = jax.nn.gelu(acc, approximate=True)
    elif act == "tanh":
        acc = jnp.tanh(acc)
    elif act == "sigmoid":
        acc = jax.nn.sigmoid(acc)
    if has_res:
        acc = refs[k][...] + refs[k + 1][0] * acc
        k += 2
    o_ref = refs[k]
    o_ref[...] = acc.astype(o_ref.dtype)


def _matmul(x, w, *, bias=None, act=None, res=None, mod=None, k_gate=None, seq=None, nseg=None,
            out_dtype=F32, tm=1024, tn=512, rows=None, w_lead=()):
    m = x.shape[0] if rows is None else rows
    kdim, n = w.shape[-2:]
    tn = min(tn, n)
    tm = min(tm, m)
    in_specs = [pl.BlockSpec((tm, kdim), lambda i, j: (i, 0)),
                pl.BlockSpec((None,) * len(w_lead) + (kdim, tn), lambda i, j: (*w_lead, 0, j))]
    args = [x, w]
    if bias is not None:
        in_specs.append(pl.BlockSpec((1, tn), lambda i, j: (0, j)))
        args.append(bias.reshape(1, n))
    if res is not None:
        seg = lambda i: jnp.minimum((i * tm) // seq, nseg)
        in_specs += [pl.BlockSpec((tm, tn), lambda i, j: (i, j)),
                     pl.BlockSpec((1, 1, tn), lambda i, j: (seg(i) * 6 + k_gate, 0, j))]
        args += [res, mod]
    return pl.pallas_call(
        functools.partial(_matmul_kernel, act=act, has_bias=bias is not None, has_res=res is not None),
        grid=(_div(m, tm), _div(n, tn)),
        in_specs=in_specs,
        out_specs=pl.BlockSpec((tm, tn), lambda i, j: (i, j)),
        out_shape=jax.ShapeDtypeStruct((m, n), out_dtype),
        compiler_params=_cparams("arbitrary", "arbitrary"),
        name="matmul",
    )(*args)


def _ffn1_kernel(x_ref, xp_ref, xn_ref, wg_ref, wu_ref, cw_ref, cb_ref, o_ref, *, tm, n_lat_rows,
                 seq_lat, seq_ctx):
    i = pl.program_id(0)
    x = x_ref[...]
    wg = wg_ref[...]
    g = jnp.dot(x, wg, preferred_element_type=F32)
    gp = jnp.dot(xp_ref[...], wg, preferred_element_type=F32)
    gn = jnp.dot(xn_ref[...], wg, preferred_element_type=F32)
    row, first, last = _seq_edges(i, tm, n_lat_rows, seq_lat, seq_ctx)
    g_prev, g_next = _shift_rows(g, gp, gn, row, first, last, tm)
    gte = g_prev * cw_ref[0:1] + g * cw_ref[1:2] + g_next * cw_ref[2:3] + cb_ref[...]
    up = jnp.dot(x, wu_ref[...], preferred_element_type=F32)
    o_ref[...] = (_silu(gte) * up).astype(o_ref.dtype)


def _halo_specs(tm, kdim, m):
    nb = tm // SUBLANES
    prev = pl.BlockSpec((SUBLANES, kdim), lambda i, j: (jnp.maximum(i * nb - 1, 0), 0))
    nxt = pl.BlockSpec((SUBLANES, kdim), lambda i, j: (jnp.minimum((i + 1) * nb, m // SUBLANES - 1), 0))
    return prev, nxt


def _ffn1(u, wg, wu, cw, cb, *, rows, n_lat_rows, seq_lat, seq_ctx):
    kdim, f = wg.shape
    tm, tn = 1024, 512
    prev, nxt = _halo_specs(tm, kdim, rows)
    return pl.pallas_call(
        functools.partial(_ffn1_kernel, tm=tm, n_lat_rows=n_lat_rows, seq_lat=seq_lat, seq_ctx=seq_ctx),
        grid=(_div(rows, tm), _div(f, tn)),
        in_specs=[pl.BlockSpec((tm, kdim), lambda i, j: (i, 0)), prev, nxt,
                  pl.BlockSpec((kdim, tn), lambda i, j: (0, j)),
                  pl.BlockSpec((kdim, tn), lambda i, j: (0, j)),
                  pl.BlockSpec((3, tn), lambda i, j: (0, j)),
                  pl.BlockSpec((1, tn), lambda i, j: (0, j))],
        out_specs=pl.BlockSpec((tm, tn), lambda i, j: (i, j)),
        out_shape=jax.ShapeDtypeStruct((rows, f), BF16),
        compiler_params=_cparams("arbitrary", "arbitrary"),
        name="ffn_gate_up",
    )(u, u, u, wg, wu, cw, cb)


def _pool_kernel(u_ref, w_ref, b_ref, s_ref, h_ref, gate_ref, o_ref, xp_ref, *, seq):
    grp = pl.program_id(1)
    pad = SUBLANES
    zeros = jnp.zeros((pad, u_ref.shape[1]), F32)
    xp_ref[0:pad] = zeros
    xp_ref[pad:pad + seq] = u_ref[...]
    xp_ref[pad + seq:2 * pad + seq] = zeros
    rt = min(seq, 256)
    for gi, win in enumerate(POOL_WINDOWS):
        @pl.when(grp == gi)
        def _(win=win):
            left, right = win // 2, win - 1 - win // 2
            for rc in range(seq // rt):
                base = pad + rc * rt
                t = rc * rt + lax.broadcasted_iota(jnp.int32, (rt, 1), 0)
                acc = xp_ref[base - left:base - left + rt]
                for j in range(1, win):
                    acc = acc + xp_ref[base - left + j:base - left + j + rt]
                cnt = (jnp.minimum(t + right, seq - 1) - jnp.maximum(t - left, 0) + 1).astype(F32)
                p = acc / cnt - xp_ref[base:base + rt]
                y = jnp.dot(p.astype(BF16), w_ref[0], preferred_element_type=F32) + b_ref[...]
                rows = slice(rc * rt, (rc + 1) * rt)
                o_ref[rows] = h_ref[rows] + gate_ref[0] * (y * s_ref[...])


def _pool_mix(u, w, b, scale, h, mod, *, seq, row0, nseq, seg0):
    d = u.shape[1]
    cg = d // len(POOL_WINDOWS)
    blk0 = row0 // seq
    tile = pl.BlockSpec((seq, cg), lambda s, g: (blk0 + s, g))
    vec = pl.BlockSpec((1, cg), lambda s, g: (0, g))
    return pl.pallas_call(
        functools.partial(_pool_kernel, seq=seq),
        grid=(nseq, len(POOL_WINDOWS)),
        in_specs=[tile, pl.BlockSpec((1, cg, cg), lambda s, g: (g, 0, 0)), vec, vec, tile,
                  pl.BlockSpec((1, 1, cg), lambda s, g: ((seg0(s)) * 6 + 2, 0, g))],
        out_specs=tile,
        out_shape=jax.ShapeDtypeStruct(h.shape, F32),
        scratch_shapes=[pltpu.VMEM((seq + 2 * SUBLANES, cg), F32)],
        input_output_aliases={4: 0},
        compiler_params=_cparams("arbitrary", "arbitrary"),
        name="pool_mix",
    )(u, w, b.reshape(1, d), scale.reshape(1, d), h, mod)


def _na_lat_kernel(q_ref, k_ref, v_ref, kc_ref, vc_ref, cos_ref, sin_ref, bias_ref, o_ref,
                   q_s, k_s, *, rows, kh):
    hd = NA_HEAD_DIM
    lane = lax.broadcasted_iota(jnp.int32, (1, LANES), 1)
    low = (lane % 32) < 16
    head0 = lane < hd
    cos, sin = cos_ref[...], sin_ref[...]

    def rope(x):
        swapped = jnp.where(low, pltpu.roll(x, LANES - 16, axis=1), pltpu.roll(x, 16, axis=1))
        return x * cos + swapped * sin

    q_s[...] = (rope(q_ref[...].astype(F32)) * (hd ** -0.5)).astype(BF16)
    k_s[...] = rope(k_ref[...].astype(F32)).astype(BF16)
    nloc = kh * GRID_W
    rows_per_step = 8
    dot = lambda a, b: jnp.dot(a, b, preferred_element_type=F32)
    zero = jnp.zeros((), BF16)

    def body(step, carry):
        n = range(rows_per_step)
        r = [step * rows_per_step + dr for dr in n]
        rs = [jnp.clip(r[i] - kh // 2, 0, rows - kh) for i in n]
        q0 = [pl.multiple_of(r[i] * GRID_W, GRID_W) for i in n]
        k0 = [pl.multiple_of(rs[i] * GRID_W, GRID_W) for i in n]
        qr = [q_s[pl.ds(q0[i], GRID_W), :] for i in n]
        qm = [jnp.concatenate([jnp.where(head0, qr[i], zero), jnp.where(head0, zero, qr[i])], axis=0) for i in n]
        s_loc = [_nt(qm[i], k_s[pl.ds(k0[i], nloc), :]) + bias_ref[r[i] - rs[i]] for i in n]
        s_ctx = [_nt(qm[i], kc_ref[...]) for i in n]
        m = [jnp.maximum(jnp.max(s_loc[i], axis=-1, keepdims=True), jnp.max(s_ctx[i], axis=-1, keepdims=True))
             for i in n]
        p_loc = [jnp.exp(s_loc[i] - m[i]) for i in n]
        p_ctx = [jnp.exp(s_ctx[i] - m[i]) for i in n]
        den = [jnp.sum(p_loc[i], axis=-1, keepdims=True) + jnp.sum(p_ctx[i], axis=-1, keepdims=True) for i in n]
        o = [dot(p_loc[i].astype(BF16), v_ref[pl.ds(k0[i], nloc), :]) + dot(p_ctx[i].astype(BF16), vc_ref[...])
             for i in n]
        o = [o[i] / den[i] for i in n]
        for i in n:
            o_ref[pl.ds(q0[i], GRID_W), :] = jnp.where(head0, o[i][:GRID_W], o[i][GRID_W:]).astype(o_ref.dtype)
        return carry

    lax.fori_loop(0, _div(rows, rows_per_step), body, 0)


def _na_ctx_kernel(q_ref, k_ref, v_ref, o_alias_ref, o_ref):
    del o_alias_ref
    hd = NA_HEAD_DIM
    q = q_ref[...] * (hd ** -0.5)
    k = k_ref[...]
    v = v_ref[...]
    for hh in range(2):
        ls = slice(hh * hd, (hh + 1) * hd)
        s = _nt(q[:, ls].astype(BF16), k[:, ls].astype(BF16))
        m = jnp.max(s, axis=-1, keepdims=True)
        p = jnp.exp(s - m)
        den = jnp.sum(p, axis=-1, keepdims=True)
        o = jnp.dot(p.astype(BF16), v[:, ls].astype(BF16), preferred_element_type=F32)
        o_ref[:, ls] = (o / den).astype(o_ref.dtype)


def _rope_tables(t_len, hd):
    half = hd // 2
    pos = jnp.arange(t_len)
    inv_freq = ROPE_BASE ** (-jnp.arange(0, half, 2, dtype=F32) / half)
    d = jnp.arange(hd)
    p = jnp.where(d[None, :] < half, (pos // GRID_W)[:, None], (pos % GRID_W)[:, None]).astype(F32)
    ang = p * inv_freq[d % (half // 2)][None, :]
    cos = jnp.cos(ang)
    sin = jnp.where((d % half) < half // 2, -jnp.sin(ang), jnp.sin(ang))
    return jnp.tile(cos, (1, 2)), jnp.tile(sin, (1, 2))


def _na_bias_table(rpb, kh):
    col = jnp.arange(GRID_W)
    col_start = jnp.clip(col - NA_KW // 2, 0, GRID_W - NA_KW)
    col_ok = (col[None, :] >= col_start[:, None]) & (col[None, :] < col_start[:, None] + NA_KW)
    dc = jnp.clip(col[None, :] - col[:, None], -(NA_KW - 1), NA_KW - 1) + NA_KW - 1
    rpb_cols = rpb[:, :, dc]
    dr = jnp.arange(kh)[None, :] - jnp.arange(kh)[:, None] + NA_KH - 1
    tab = rpb_cols[:, dr]
    tab = jnp.where(col_ok[None, None, None], tab, -1e30)
    heads = rpb.shape[0]
    tab = tab.transpose(0, 1, 3, 2, 4).reshape(heads // 2, 2, kh, GRID_W, kh * GRID_W)
    return tab.transpose(0, 2, 1, 3, 4).reshape(heads // 2, kh, 2 * GRID_W, kh * GRID_W)


def _neighbourhood_attention(qkv, rpb, *, bsz, t_len, l_len):
    d = qkv.shape[1] // 3
    hd = NA_HEAD_DIM
    heads = d // hd
    npair = d // LANES
    rows = t_len // GRID_W
    kh = min(NA_KH, rows)
    n_lat = bsz * t_len
    n_all = qkv.shape[0]
    ctx0 = _div(n_lat, l_len)
    cos, sin = _rope_tables(t_len, hd)
    bias = _na_bias_table(rpb, kh)
    lat_blk = lambda off: pl.BlockSpec((t_len, LANES), lambda b, p: (b, off + p))
    ctx_blk = lambda off: pl.BlockSpec((l_len, LANES), lambda b, p: (ctx0 + b, off + p))
    tab_blk = pl.BlockSpec((t_len, LANES), lambda b, p: (0, 0))
    out = jax.ShapeDtypeStruct((n_all, d), BF16)
    o = pl.pallas_call(
        functools.partial(_na_lat_kernel, rows=rows, kh=kh),
        grid=(bsz, npair),
        in_specs=[lat_blk(0), lat_blk(npair), lat_blk(2 * npair), ctx_blk(npair), ctx_blk(2 * npair),
                  tab_blk, tab_blk,
                  pl.BlockSpec((None, kh, 2 * GRID_W, kh * GRID_W), lambda b, p: (p, 0, 0, 0))],
        out_specs=lat_blk(0),
        out_shape=out,
        scratch_shapes=[pltpu.VMEM((t_len, LANES), BF16), pltpu.VMEM((t_len, LANES), BF16)],
        compiler_params=_cparams("arbitrary", "arbitrary"),
        name="na_latent",
    )(qkv, qkv, qkv, qkv, qkv, cos, sin, bias)
    return pl.pallas_call(
        _na_ctx_kernel,
        grid=(bsz, npair),
        in_specs=[ctx_blk(0), ctx_blk(npair), ctx_blk(2 * npair), pl.BlockSpec(memory_space=pl.ANY)],
        out_specs=ctx_blk(0),
        out_shape=out,
        input_output_aliases={3: 0},
        compiler_params=_cparams("arbitrary", "arbitrary"),
        name="na_context",
    )(qkv, qkv, qkv, o)


def _sg_kernel(z_u_ref, z_v_ref, g_ref, ws_ref, bs_ref, o_ref, *, nchunk):
    zv = z_v_ref[...].astype(F32)
    zv = zv * lax.rsqrt(jnp.mean(zv * zv, axis=-1, keepdims=True) + NORM_EPS) * g_ref[...]
    zvb = zv.astype(BF16)
    for c in range(nchunk):
        rs = slice(c * SG_CHUNK, (c + 1) * SG_CHUNK)
        for g in range(SG_GROUPS):
            ls = slice(g * LANES, (g + 1) * LANES)
            mixed = jnp.dot(ws_ref[g], zvb[rs, ls], preferred_element_type=F32) + bs_ref[:, ls]
            o_ref[rs, ls] = (z_u_ref[rs, ls].astype(F32) * mixed).astype(o_ref.dtype)


def _spatial_gate(z, norm_g, w_s, b_s):
    m, two_w = z.shape
    width = two_w // 2
    nchunk = 2
    tm = nchunk * SG_CHUNK
    bs_full = jnp.repeat(b_s.T, width // SG_GROUPS, axis=1)
    return pl.pallas_call(
        functools.partial(_sg_kernel, nchunk=nchunk),
        grid=(_div(m, tm),),
        in_specs=[pl.BlockSpec((tm, width), lambda i: (i, 0)),
                  pl.BlockSpec((tm, width), lambda i: (i, 1)),
                  pl.BlockSpec((1, width), lambda i: (0, 0)),
                  pl.BlockSpec((SG_GROUPS, SG_CHUNK, SG_CHUNK), lambda i: (0, 0, 0)),
                  pl.BlockSpec((SG_CHUNK, width), lambda i: (0, 0))],
        out_specs=pl.BlockSpec((tm, width), lambda i: (i, 0)),
        out_shape=jax.ShapeDtypeStruct((m, width), BF16),
        compiler_params=_cparams("arbitrary"),
        name="spatial_gate",
    )(z, z, norm_g.reshape(1, width), w_s.astype(BF16), bs_full)


def _shift_kernel(u_ref, up_ref, un_ref, mu_ref, *o_refs, tm, n_lat_rows, seq_lat, seq_ctx):
    i = pl.program_id(0)
    u = u_ref[...]
    row, first, last = _seq_edges(i, tm, n_lat_rows, seq_lat, seq_ctx)
    u_prev, u_next = _shift_rows(u, up_ref[...], un_ref[...], row, first, last, tm)
    prev = u_prev - u
    nxt = u_next - u
    nmix = len(o_refs)
    for n in range(nmix):
        o_refs[n][...] = (u + prev * mu_ref[n:n + 1] + nxt * mu_ref[nmix + n:nmix + n + 1]).astype(BF16)


def _token_shift(u, mu, *, n_lat_rows, seq_lat, seq_ctx):
    m, d = u.shape
    nmix = mu.shape[1]
    tm, tn = 512, 512
    nb = tm // SUBLANES
    tile = pl.BlockSpec((tm, tn), lambda i, j: (i, j))
    prev = pl.BlockSpec((SUBLANES, tn), lambda i, j: (jnp.maximum(i * nb - 1, 0), j))
    nxt = pl.BlockSpec((SUBLANES, tn), lambda i, j: (jnp.minimum((i + 1) * nb, m // SUBLANES - 1), j))
    return pl.pallas_call(
        functools.partial(_shift_kernel, tm=tm, n_lat_rows=n_lat_rows, seq_lat=seq_lat, seq_ctx=seq_ctx),
        grid=(_div(m, tm), _div(d, tn)),
        in_specs=[tile, prev, nxt, pl.BlockSpec((2 * nmix, tn), lambda i, j: (0, j))],
        out_specs=[tile] * nmix,
        out_shape=[jax.ShapeDtypeStruct((m, d), BF16)] * nmix,
        compiler_params=_cparams("arbitrary", "arbitrary"),
        name="token_shift",
    )(u, u, u, mu.reshape(2 * nmix, d))


def _split2(x):
    hi = x.astype(BF16)
    lo = (x - hi.astype(F32)).astype(BF16)
    return hi, lo


def _chunk_chains(kt_ref, rt_ref, bh_ref, kh_ref, v_ref, bt_ref, kbt_ref, m_o, c_o, rp_o, y0_o, *, rev,
                  nheads):
    c = SCAN_CHUNK
    hd = RW_HEAD_DIM
    ri = lax.broadcasted_iota(jnp.int32, (c, c), 0)
    ci = lax.broadcasted_iota(jnp.int32, (c, c), 1)
    strict = (ci > ri) if rev else (ci < ri)
    incl = (ci >= ri) if rev else (ci <= ri)
    diag_blk = (ri // SOLVE_BLOCK) == (ci // SOLVE_BLOCK)
    eye = jnp.where(ri == ci, 1.0, 0.0)
    chains = [(half, hh) for half in range(LANES // c) for hh in range(nheads)]
    rows = [slice(half * c, (half + 1) * c) for half, _ in chains]
    ls = [slice(hh * hd, (hh + 1) * hd) for _, hh in chains]
    pair = [hh // 2 for _, hh in chains]
    sub = [slice((hh % 2) * hd, (hh % 2 + 1) * hd) for _, hh in chains]
    n = range(len(chains))
    dot = lambda a, b: jnp.dot(a, b, preferred_element_type=F32)
    kt = [kt_ref[rows[i], ls[i]] for i in n]
    rt = [rt_ref[rows[i], ls[i]] for i in n]
    vv = [v_ref[rows[i], ls[i]] for i in n]
    ktrt = [jnp.concatenate([kt[i], rt[i]], axis=0) for i in n]
    pb = [_nt(ktrt[i], bh_ref[rows[i], ls[i]]) for i in n]
    pk = [_nt(ktrt[i], kh_ref[rows[i], ls[i]]) for i in n]
    a1 = [jnp.where(strict, pb[i][:c], 0.0) for i in n]
    a3 = [jnp.where(incl, pb[i][c:], 0.0).astype(BF16) for i in n]
    ad = [jnp.where(diag_blk, a1[i], 0.0) for i in n]
    ao = [(a1[i] - ad[i]).astype(BF16) for i in n]
    tinv = [eye - ad[i] for i in n]
    pw = [_mm(ad[i], ad[i]) for i in n]
    lhs_v = [jnp.concatenate([jnp.where(strict, pk[i][:c], 0.0).astype(BF16),
                              jnp.where(incl, pk[i][c:], 0.0).astype(BF16),
                              kbt_ref[pair[i], sub[i], rows[i]]], axis=0) for i in n]
    pv = [dot(lhs_v[i], vv[i]) for i in n]
    tinv = [tinv[i] + _mm(tinv[i], pw[i]) for i in n]
    for _ in range(int(math.log2(SOLVE_BLOCK)) - 2):
        pw = [_mm(pw[i], pw[i]) for i in n]
        tinv = [tinv[i] + _mm(tinv[i], pw[i]) for i in n]
    rhs = [jnp.concatenate([kt[i].astype(F32), pv[i][:c]], axis=1).astype(BF16) for i in n]
    tinv = [t.astype(BF16) for t in tinv]
    nmat = [dot(tinv[i], ao[i]).astype(BF16) for i in n]
    x1 = [dot(tinv[i], rhs[i]) for i in n]
    x = x1
    for _ in range(c // SOLVE_BLOCK - 1):
        x = [x1[i] - dot(nmat[i], x[i].astype(BF16)) for i in n]
    lhs_x = [jnp.concatenate([bt_ref[pair[i], sub[i], rows[i]], a3[i]], axis=0) for i in n]
    px = [dot(lhs_x[i], x[i].astype(BF16)) for i in n]
    for i, (half, hh) in enumerate(chains):
        m_o[hh, 0, :, rows[i]] = (-px[i][:c, :hd]).astype(m_o.dtype)
        c_o[hh, 0, :, rows[i]] = pv[i][2 * c:] - px[i][:c, hd:]
        rp_o[rows[i], ls[i]] = (rt[i].astype(F32) - px[i][c:, :hd]).astype(rp_o.dtype)
        y0_o[rows[i], ls[i]] = pv[i][c:2 * c] - px[i][c:, hd:]


def _rw_chunk_kernel(k_ref, r_ref, v_ref, hw_ref, ha_ref, w2_ref, a2_ref, w0_ref, a0_ref, kk_ref, ka_ref,
                     rk_ref, tri_ref, bc_o, mf_o, cf_o, rpf_o, y0f_o, gf_o, mr_o, cr_o, rpr_o, y0r_o, gr_o,
                     kt_s, rt_s, bh_s, kh_s, v_s, bt_s, kbt_s, *, lora, nheads):
    npair = nheads // 2
    lane = lax.broadcasted_iota(jnp.int32, (1, LANES), 1)
    head0 = lane < RW_HEAD_DIM
    pairs = [slice(p * LANES, (p + 1) * LANES) for p in range(npair)]
    head_sum = lambda x: jnp.concatenate([_head_sum(x[:, ps], head0) for ps in pairs], axis=1)
    k = k_ref[...]
    r = r_ref[...]
    v_s[...] = v_ref[...].astype(BF16)
    kkv = k * kk_ref[...]
    kk = kkv * lax.rsqrt(jnp.maximum(head_sum(kkv * kkv), 1e-12))
    ka = ka_ref[...]
    rk = rk_ref[...]
    hw = hw_ref[...]
    ha = ha_ref[...]
    log_decay, beta, key = [], [], []
    for d in range(2):
        ls = slice(d * lora, (d + 1) * lora)
        pre = w0_ref[d] + jnp.dot(hw[:, ls], w2_ref[d], preferred_element_type=F32)
        z = -pre
        softplus = jnp.maximum(z, 0.0) + jnp.log(1.0 + jnp.exp(-jnp.abs(z)))
        log_decay.append(-jnp.exp(-softplus - 0.5))
        a = jax.nn.sigmoid(a0_ref[d] + jnp.dot(ha[:, ls], a2_ref[d], preferred_element_type=F32))
        beta.append(kk * a)
        key.append(k * (1.0 + (a - 1.0) * ka))
    bc_o[...] = head_sum(r * key[0] * rk) + head_sum(r * key[1] * rk)
    g_o = (gf_o, gr_o)
    for d in range(2):
        ld = log_decay[d]
        pieces = _split2(ld)
        csum = [sum(jnp.dot(tri_ref[e], p, preferred_element_type=F32) for p in pieces) for e in range(2)]
        cum = csum[d]
        total = csum[0] + csum[1] - ld
        kt_s[d] = (kk * jnp.exp(cum - ld)).astype(BF16)
        rt_s[d] = (r * jnp.exp(cum)).astype(BF16)
        inv = jnp.exp(-cum)
        bh_s[d] = (beta[d] * inv).astype(BF16)
        kh_s[d] = (key[d] * inv).astype(BF16)
        tail = jnp.exp(total - cum)
        bbar = beta[d] * tail
        kbar = key[d] * tail
        gdec = jnp.exp(total)
        for p, ps in enumerate(pairs):
            bt_s[d, p] = jnp.transpose(bbar[:, ps]).astype(BF16)
            kbt_s[d, p] = jnp.transpose(kbar[:, ps]).astype(BF16)
            g_o[d][p, 0] = jnp.transpose(gdec[:, ps])
    outs = ((mf_o, cf_o, rpf_o, y0f_o), (mr_o, cr_o, rpr_o, y0r_o))
    for d in range(2):
        _chunk_chains(kt_s.at[d], rt_s.at[d], bh_s.at[d], kh_s.at[d], v_s, bt_s.at[d], kbt_s.at[d],
                      *outs[d], rev=d == 1, nheads=nheads)


def _rw_chunks(k, r, v, hw, ha, w2, a2, w0, a0, k_k, k_a, r_k, *, bsz, t_len, l_len, lora):
    m, d = k.shape
    hd = RW_HEAD_DIM
    nheads = 8
    width = nheads * hd
    npair = nheads // 2
    tt = t_len + l_len
    ng = _div(tt, LANES)
    ctx_groups = _div(l_len, LANES)
    ctx0 = _div(bsz * t_len, LANES)

    def in_blk(b, g):
        return jnp.where(g < ctx_groups, ctx0 + b * ctx_groups + g,
                         b * (t_len // LANES) + g - ctx_groups)

    ri = jnp.arange(LANES)[:, None]
    ci = jnp.arange(LANES)[None, :]
    same = (ri // SCAN_CHUNK) == (ci // SCAN_CHUNK)
    tri = jnp.stack([same & (ci <= ri), same & (ci >= ri)]).astype(BF16)
    tile = pl.BlockSpec((LANES, width), lambda b, p, g: (in_blk(b, g), p))
    lora_blk = pl.BlockSpec((LANES, 2 * lora), lambda b, p, g: (in_blk(b, g), 0))
    wl_blk = pl.BlockSpec((2, lora, width), lambda b, p, g: (0, 0, p))
    v2_blk = pl.BlockSpec((2, 1, width), lambda b, p, g: (0, 0, p))
    v1_blk = pl.BlockSpec((1, width), lambda b, p, g: (0, p))
    tri_blk = pl.BlockSpec((2, LANES, LANES), lambda b, p, g: (0, 0, 0))
    row_o = pl.BlockSpec((None, LANES, width), lambda b, p, g: (b, g, p))
    sq_o = pl.BlockSpec((None, nheads, 1, hd, LANES), lambda b, p, g: (b, p, g, 0, 0))
    tr_o = pl.BlockSpec((None, npair, 1, LANES, LANES), lambda b, p, g: (b, p, g, 0, 0))
    row_shape = lambda dt: jax.ShapeDtypeStruct((bsz, tt, d), dt)
    sq_shape = lambda dt: jax.ShapeDtypeStruct((bsz, d // hd, ng, hd, LANES), dt)
    tr_shape = jax.ShapeDtypeStruct((bsz, d // LANES, ng, LANES, LANES), F32)
    dir_specs = [sq_o, sq_o, row_o, row_o, tr_o]
    dir_shapes = [sq_shape(BF16), sq_shape(F32), row_shape(BF16), row_shape(F32), tr_shape]
    operand = lambda: pltpu.VMEM((2, LANES, width), BF16)
    transposed = lambda: pltpu.VMEM((2, npair, LANES, LANES), BF16)
    out = pl.pallas_call(
        functools.partial(_rw_chunk_kernel, lora=lora, nheads=nheads),
        grid=(bsz, _div(d, width), ng),
        in_specs=[tile, tile, tile, lora_blk, lora_blk, wl_blk, wl_blk, v2_blk, v2_blk, v1_blk, v1_blk, v1_blk,
                  tri_blk],
        out_specs=[row_o] + dir_specs + dir_specs,
        out_shape=[row_shape(F32)] + dir_shapes + dir_shapes,
        scratch_shapes=[operand(), operand(), operand(), operand(), pltpu.VMEM((LANES, width), BF16),
                        transposed(), transposed()],
        compiler_params=_cparams("arbitrary", "arbitrary", "arbitrary"),
        name="rwkv_chunks",
    )(k, r, v, hw, ha, w2, a2, w0.reshape(2, 1, d), a0.reshape(2, 1, d), k_k.reshape(1, d),
      k_a.reshape(1, d), r_k.reshape(1, d), tri)
    return out[0], out[1:6], out[6:11]


def _state_pass_kernel(mf_ref, cf_ref, gf_ref, mr_ref, cr_ref, gr_ref, sf_o, sr_o, *, ctx_groups, n_groups,
                       nheads):
    c = SCAN_CHUNK
    hd = RW_HEAD_DIM
    heads = range(nheads)
    sf_o[...] = jnp.zeros(sf_o.shape, sf_o.dtype)
    sr_o[...] = jnp.zeros(sr_o.shape, sr_o.dtype)
    dirs = ((mf_ref, cf_ref, gf_ref, sf_o, False), (mr_ref, cr_ref, gr_ref, sr_o, True))

    def group(i, carry):
        out = []
        g_of = [i, jnp.where(i < ctx_groups, ctx_groups - 1 - i, n_groups - 1 - (i - ctx_groups))]
        states = [list(carry[0]), list(carry[1])]
        for step in range(LANES // c):
            s0b, ms = [[], []], [[], []]
            for di, (m_ref, c_ref, g_ref, s_o, rev) in enumerate(dirs):
                half = (LANES // c - 1 - step) if rev else step
                ts = slice(half * c, (half + 1) * c)
                for hh in heads:
                    sb = states[di][hh].astype(BF16)
                    s0b[di].append(sb)
                    blk = slice((hh % 2) * hd, (hh % 2 + 1) * hd)
                    s_o[hh // 2, g_of[di] * (LANES // c) + half, blk, blk] = sb
            for di, (m_ref, c_ref, g_ref, s_o, rev) in enumerate(dirs):
                half = (LANES // c - 1 - step) if rev else step
                ts = slice(half * c, (half + 1) * c)
                ms[di] = [jnp.dot(m_ref[hh, g_of[di], :, ts], s0b[di][hh], preferred_element_type=F32)
                          for hh in heads]
            for di, (m_ref, c_ref, g_ref, s_o, rev) in enumerate(dirs):
                half = (LANES // c - 1 - step) if rev else step
                ts = slice(half * c, (half + 1) * c)
                for hh in heads:
                    blk = slice((hh % 2) * hd, (hh % 2 + 1) * hd)
                    gc = g_ref[hh // 2, g_of[di], blk, half * c:half * c + 1]
                    states[di][hh] = gc * states[di][hh] + ms[di][hh] + c_ref[hh, g_of[di], :, ts]
        return tuple(states[0]), tuple(states[1])

    zero = tuple(jnp.zeros((hd, hd), F32) for _ in heads)
    lax.fori_loop(0, n_groups, group, (zero, zero))


def _state_pass(m_f, c_f, g_f, m_r, c_r, g_r, *, l_len):
    bsz, heads, ng, hd, _ = m_f.shape
    nheads = 4
    npair = nheads // 2
    nc = ng * (LANES // SCAN_CHUNK)
    sq = pl.BlockSpec((None, nheads, ng, hd, LANES), lambda b, p: (b, p, 0, 0, 0))
    tr = pl.BlockSpec((None, npair, ng, LANES, LANES), lambda b, p: (b, p, 0, 0, 0))
    st = pl.BlockSpec((None, npair, nc, LANES, LANES), lambda b, p: (b, p, 0, 0, 0))
    st_shape = jax.ShapeDtypeStruct((bsz, heads // 2, nc, LANES, LANES), BF16)
    return pl.pallas_call(
        functools.partial(_state_pass_kernel, ctx_groups=l_len // LANES, n_groups=ng, nheads=nheads),
        grid=(bsz, _div(heads, nheads)),
        in_specs=[sq, sq, tr, sq, sq, tr],
        out_specs=[st, st],
        out_shape=[st_shape, st_shape],
        compiler_params=_cparams("arbitrary", "arbitrary"),
        name="rwkv_state",
    )(m_f, c_f, g_f, m_r, c_r, g_r)


def _readout_kernel(rpf_ref, y0f_ref, sf_ref, rpr_ref, y0r_ref, sr_ref, bc_ref, v_ref, g_ref, lng_ref,
                    lnb_ref, o_ref):
    c = SCAN_CHUNK
    lane = lax.broadcasted_iota(jnp.int32, (1, LANES), 1)
    head0 = lane < RW_HEAD_DIM
    inv_n = 1.0 / RW_HEAD_DIM
    tiles = [(cc, pp) for cc in range(o_ref.shape[0] // c) for pp in range(o_ref.shape[1] // LANES)]
    rows = [slice(cc * c, (cc + 1) * c) for cc, _ in tiles]
    ls = [slice(pp * LANES, (pp + 1) * LANES) for _, pp in tiles]
    n = range(len(tiles))
    dot = lambda a, b: jnp.dot(a, b, preferred_element_type=F32)
    yf = [dot(rpf_ref[rows[i], ls[i]], sf_ref[tiles[i][1], tiles[i][0]]) for i in n]
    yr = [dot(rpr_ref[rows[i], ls[i]], sr_ref[tiles[i][1], tiles[i][0]]) for i in n]
    for i in n:
        y = (yf[i] + y0f_ref[rows[i], ls[i]]) + (yr[i] + y0r_ref[rows[i], ls[i]])
        mean = _head_sum(y, head0) * inv_n
        yc = y - mean
        var = _head_sum(yc * yc, head0) * inv_n
        yn = yc * lax.rsqrt(var + RW_GN_EPS) * lng_ref[:, ls[i]] + lnb_ref[:, ls[i]]
        o_ref[rows[i], ls[i]] = ((yn + bc_ref[rows[i], ls[i]] * v_ref[rows[i], ls[i]])
                                 * g_ref[rows[i], ls[i]]).astype(o_ref.dtype)


def _rw_readout(rp_f, y0_f, s_f, rp_r, y0_r, s_r, bc, v, g, ln_g, ln_b, *, bsz, t_len, l_len):
    d = v.shape[1]
    tm, tn = 256, 512
    per_b = t_len // tm
    off = l_len // tm
    blk = lambda i: (i // per_b, off + i % per_b)
    scan_blk = pl.BlockSpec((None, tm, tn), lambda i, p: (*blk(i), p))
    st_blk = pl.BlockSpec((None, tn // LANES, tm // SCAN_CHUNK, LANES, LANES),
                          lambda i, p: (blk(i)[0], p, blk(i)[1], 0, 0))
    tile = pl.BlockSpec((tm, tn), lambda i, p: (i, p))
    vec = pl.BlockSpec((1, tn), lambda i, p: (0, p))
    return pl.pallas_call(
        _readout_kernel,
        grid=(_div(bsz * t_len, tm), _div(d, tn)),
        in_specs=[scan_blk, scan_blk, st_blk, scan_blk, scan_blk, st_blk, scan_blk, tile, tile, vec, vec],
        out_specs=tile,
        out_shape=jax.ShapeDtypeStruct((bsz * t_len, d), BF16),
        compiler_params=_cparams("arbitrary", "arbitrary"),
        name="rwkv_readout",
    )(rp_f, y0_f, s_f, rp_r, y0_r, s_r, bc, v, g, ln_g.reshape(1, d), ln_b.reshape(1, d))


def _pad_cols(w, n):
    return jnp.pad(w, ((0, 0), (0, n - w.shape[1])))


def _pad_rows(w, n):
    return jnp.pad(w, ((0, n - w.shape[0]), (0, 0)))


def kernel(x, c, ctx, c_ctx, norm1_g, norm2_g, w_mod, b_mod, ffn_w_gate, ffn_w_up, ffn_conv_w, ffn_conv_b, ffn_w_down, final_norm_g, pool_w, pool_b, pool_scale, na_w_qkv, na_rpb, na_w_o, sg_w_in, sg_b_in, sg_norm_g, sg_w_s, sg_b_s, sg_w_o, rw_mu, rw_w_rkv, rw_w0, rw_w1, rw_w2, rw_a0, rw_a1, rw_a2, rw_g1, rw_g2, rw_k_k, rw_k_a, rw_r_k, rw_ln_g, rw_ln_b, rw_w_o):
    bsz, t_len, d = x.shape
    l_len = ctx.shape[1]
    depth = norm1_g.shape[0]
    n_mixers = 4
    n_lat = bsz * t_len
    n_all = n_lat + bsz * l_len
    f = ffn_w_gate.shape[2]
    f_pad = -(-f // 512) * 512

    cvec = jnp.concatenate([c, c_ctx[None], jnp.zeros((SUBLANES - bsz - 1, d), F32)], axis=0)
    mods = _modulation(cvec, w_mod, b_mod)
    h = jnp.concatenate([x.reshape(n_lat, d), ctx.reshape(bsz * l_len, d)], axis=0)
    seg = dict(seq=t_len, nseg=bsz)
    edges = dict(n_lat_rows=n_lat, seq_lat=t_len, seq_ctx=l_len)

    for i in range(depth):
        m_kind, j = i % n_mixers, i // n_mixers
        last = i == depth - 1
        mod = mods[i].reshape(SUBLANES * 6, 1, d)
        rows = n_lat if last else n_all
        if m_kind == 0:
            u = _norm(h, norm1_g[i], mod, 0, 1, t_len, bsz, F32, rows=rows)
            pw = pool_w[j].astype(BF16)
            h = _pool_mix(u, pw, pool_b[j], pool_scale[j], h, mod, seq=t_len, row0=0, nseq=bsz,
                          seg0=lambda s: s)
            if not last:
                h = _pool_mix(u, pw, pool_b[j], pool_scale[j], h, mod, seq=l_len, row0=n_lat, nseq=bsz,
                              seg0=lambda s: bsz)
        elif m_kind == 1:
            u = _norm(h, norm1_g[i], mod, 0, 1, t_len, bsz, BF16)
            qkv = _matmul(u, na_w_qkv[j].astype(BF16), out_dtype=BF16)
            o = _neighbourhood_attention(qkv, na_rpb[j], bsz=bsz, t_len=t_len, l_len=l_len)
            h = _matmul(o, na_w_o[j].astype(BF16), res=h, mod=mod, k_gate=2, rows=rows, **seg)
        elif m_kind == 2:
            u = _norm(h, norm1_g[i], mod, 0, 1, t_len, bsz, BF16, rows=rows)
            z = _matmul(u, sg_w_in[j].astype(BF16), bias=sg_b_in[j], act="gelu", out_dtype=BF16)
            gated = _spatial_gate(z, sg_norm_g[j], sg_w_s[j], sg_b_s[j])
            h = _matmul(gated, sg_w_o[j].astype(BF16), res=h, mod=mod, k_gate=2, rows=rows, **seg)
        else:
            lora = LANES
            u = _norm(h, norm1_g[i], mod, 0, 1, t_len, bsz, F32)
            xr, xw, xk, xv, xa, xg = _token_shift(u, rw_mu[j], **edges)
            w_rkv = rw_w_rkv.astype(BF16)
            r = _matmul(xr, w_rkv, w_lead=(j, 0))
            k = _matmul(xk, w_rkv, w_lead=(j, 1))
            v = _matmul(xv, w_rkv, w_lead=(j, 2))
            w1 = jnp.concatenate([_pad_cols(rw_w1[j, e], lora) for e in range(2)], axis=1).astype(BF16)
            a1 = jnp.concatenate([_pad_cols(rw_a1[j, e], lora) for e in range(2)], axis=1).astype(BF16)
            w2 = jnp.stack([_pad_rows(rw_w2[j, e], lora) for e in range(2)]).astype(BF16)
            a2 = jnp.stack([_pad_rows(rw_a2[j, e], lora) for e in range(2)]).astype(BF16)
            hw = _matmul(xw, w1, act="tanh", out_dtype=BF16)
            ha = _matmul(xa, a1, out_dtype=BF16)
            hg = _matmul(xg, rw_g1[j].astype(BF16), act="sigmoid", out_dtype=BF16, rows=n_lat)
            g = _matmul(hg, rw_g2[j].astype(BF16))
            bc, (m_f, c_f, rp_f, y0_f, g_f), (m_r, c_r, rp_r, y0_r, g_r) = _rw_chunks(
                k, r, v, hw, ha, w2, a2, rw_w0[j], rw_a0[j], rw_k_k[j], rw_k_a[j], rw_r_k[j],
                bsz=bsz, t_len=t_len, l_len=l_len, lora=lora)
            s_f, s_r = _state_pass(m_f, c_f, g_f, m_r, c_r, g_r, l_len=l_len)
            o = _rw_readout(rp_f, y0_f, s_f, rp_r, y0_r, s_r, bc, v, g, rw_ln_g[j], rw_ln_b[j],
                            bsz=bsz, t_len=t_len, l_len=l_len)
            h = _matmul(o, rw_w_o[j].astype(BF16), res=h, mod=mod, k_gate=2, rows=n_lat, **seg)
        u2 = _norm(h, norm2_g[i], mod, 3, 4, t_len, bsz, BF16, rows=rows)
        wg = _pad_cols(ffn_w_gate[i].astype(BF16), f_pad)
        wu = _pad_cols(ffn_w_up[i].astype(BF16), f_pad)
        cw = _pad_cols(ffn_conv_w[i], f_pad)
        cb = _pad_cols(ffn_conv_b[i][None], f_pad)
        wd = _pad_rows(ffn_w_down[i].astype(BF16), f_pad)
        mid = _ffn1(u2, wg, wu, cw, cb, rows=rows, **edges)
        h = _matmul(mid, wd, res=h, mod=mod, k_gate=5, rows=rows, tm=1024, **seg)

    out = _norm(h, final_norm_g, None, 0, 0, t_len, bsz, F32, rows=n_lat)
    return out.reshape(bsz, t_len, d)
```

```python
import functools
import math

import jax
import jax.numpy as jnp
from jax import lax
from jax.experimental import pallas as pl
from jax.experimental.pallas import tpu as pltpu

F32 = jnp.float32
BF16 = jnp.bfloat16

GRID_W = 64
NORM_EPS = 1e-6
POOL_WINDOWS = (2, 4, 8, 16)
NA_HEAD_DIM = 64
NA_KH = 8
NA_KW = 16
ROPE_BASE = 10000.0
SG_CHUNK = 128
SG_GROUPS = 16
RW_HEAD_DIM = 64
RW_GN_EPS = 64e-5

LANES = 128
SUBLANES = 8
MXU_WIDTH = 256
VMEM_LIMIT = 56 * 1024 * 1024

SCAN_CHUNK = 64
SOLVE_BLOCK = 16


def _cparams(*sem):
    return pltpu.CompilerParams(dimension_semantics=sem, vmem_limit_bytes=VMEM_LIMIT)


def _div(a, b):
    assert a % b == 0, (a, b)
    return a // b


def _nt(a, b):
    return lax.dot_general(a, b, (((1,), (1,)), ((), ())), preferred_element_type=F32)


def _mm(a, b):
    return jnp.dot(a.astype(BF16), b.astype(BF16), preferred_element_type=F32)


def _silu(x):
    return x * jax.nn.sigmoid(x)


def _head_sum(x, lane_is_head0):
    s0 = jnp.sum(jnp.where(lane_is_head0, x, 0.0), axis=-1, keepdims=True)
    s1 = jnp.sum(jnp.where(lane_is_head0, 0.0, x), axis=-1, keepdims=True)
    return jnp.where(lane_is_head0, s0, s1)


def _seq_edges(i, tm, n_lat_rows, seq_lat, seq_ctx):
    row = lax.broadcasted_iota(jnp.int32, (tm, 1), 0)
    grow = i * tm + row
    slen = jnp.where(grow < n_lat_rows, seq_lat, seq_ctx)
    pos = grow & (slen - 1)
    return row, pos == 0, pos == slen - 1


def _shift_rows(x, halo_prev, halo_next, row, first, last, tm):
    xp = pltpu.roll(x, 1, axis=0)
    xp = jnp.where(row == 0, halo_prev[SUBLANES - 1:SUBLANES], xp)
    xp = jnp.where(first, 0.0, xp)
    xn = pltpu.roll(x, tm - 1, axis=0)
    xn = jnp.where(row == tm - 1, halo_next[0:1], xn)
    xn = jnp.where(last, 0.0, xn)
    return xp, xn


def _mod_kernel(c_ref, w_ref, b_ref, o_ref):
    s = _silu(c_ref[...]).astype(BF16)
    o_ref[0] = jnp.dot(s, w_ref[0].astype(BF16), preferred_element_type=F32) + b_ref[0]


def _modulation(cvec, w_mod, b_mod):
    depth, d, n = w_mod.shape
    tn = 1024
    return pl.pallas_call(
        _mod_kernel,
        grid=(depth, _div(n, tn)),
        in_specs=[pl.BlockSpec((SUBLANES, d), lambda l, j: (0, 0)),
                  pl.BlockSpec((1, d, tn), lambda l, j: (l, 0, j)),
                  pl.BlockSpec((1, 1, tn), lambda l, j: (l, 0, j))],
        out_specs=pl.BlockSpec((1, SUBLANES, tn), lambda l, j: (l, 0, j)),
        out_shape=jax.ShapeDtypeStruct((depth, SUBLANES, n), F32),
        compiler_params=_cparams("arbitrary", "arbitrary"),
        name="modulation",
    )(cvec, w_mod, b_mod.reshape(depth, 1, n))


def _norm_kernel(*refs, modulate):
    if modulate:
        h_ref, g_ref, sh_ref, sc_ref, o_ref = refs
    else:
        h_ref, g_ref, o_ref = refs
    x = h_ref[...]
    y = x * lax.rsqrt(jnp.mean(x * x, axis=-1, keepdims=True) + NORM_EPS) * g_ref[...]
    if modulate:
        y = y * (1.0 + sc_ref[0]) + sh_ref[0]
    o_ref[...] = y.astype(o_ref.dtype)


def _norm(h, g, mod, k_shift, k_scale, seq, nseg, out_dtype, rows=None):
    m, d = h.shape if rows is None else (rows, h.shape[1])
    tm = 512
    row_spec = pl.BlockSpec((tm, d), lambda i: (i, 0))
    in_specs = [row_spec, pl.BlockSpec((1, d), lambda i: (0, 0))]
    args = [h, g.reshape(1, d)]
    if mod is not None:
        seg = lambda i: jnp.minimum((i * tm) // seq, nseg)
        in_specs += [pl.BlockSpec((1, 1, d), lambda i: (seg(i) * 6 + k_shift, 0, 0)),
                     pl.BlockSpec((1, 1, d), lambda i: (seg(i) * 6 + k_scale, 0, 0))]
        args += [mod, mod]
    return pl.pallas_call(
        functools.partial(_norm_kernel, modulate=mod is not None),
        grid=(_div(m, tm),),
        in_specs=in_specs,
        out_specs=row_spec,
        out_shape=jax.ShapeDtypeStruct((m, d), out_dtype),
        compiler_params=_cparams("arbitrary"),
        name="rmsnorm_mod",
    )(*args)


def _matmul_kernel(*refs, act, has_bias, has_res):
    x_ref, w_ref = refs[0], refs[1]
    k = 2
    acc = jnp.dot(x_ref[:, :w_ref.shape[0]], w_ref[...], preferred_element_type=F32)
    if has_bias:
        acc = acc + refs[k][...]
        k += 1
    if act == "gelu":
        acc = jax.nn.gelu(acc, approximate=True)
    elif act == "tanh":
        acc = jnp.tanh(acc)
    elif act == "sigmoid":
        acc = jax.nn.sigmoid(acc)
    if has_res:
        acc = refs[k][...] + refs[k + 1][0] * acc
        k += 2
    o_ref = refs[k]
    o_ref[...] = acc.astype(o_ref.dtype)


def _matmul(x, w, *, bias=None, act=None, res=None, mod=None, k_gate=None, seq=None, nseg=None,
            out_dtype=F32, tm=1024, tn=512, rows=None, w_lead=()):
    m = x.shape[0] if rows is None else rows
    kdim, n = w.shape[-2:]
    tn = min(tn, n)
    tm = min(tm, m)
    assert x.shape[1] >= kdim and (x.shape[1] == kdim or kdim % LANES == 0)
    in_specs = [pl.BlockSpec((tm, x.shape[1]), lambda i, j: (i, 0)),
                pl.BlockSpec((None,) * len(w_lead) + (kdim, tn), lambda i, j: (*w_lead, 0, j))]
    args = [x, w]
    if bias is not None:
        in_specs.append(pl.BlockSpec((1, tn), lambda i, j: (0, j)))
        args.append(bias.reshape(1, n))
    if res is not None:
        seg = lambda i: jnp.minimum((i * tm) // seq, nseg)
        in_specs += [pl.BlockSpec((tm, tn), lambda i, j: (i, j)),
                     pl.BlockSpec((1, 1, tn), lambda i, j: (seg(i) * 6 + k_gate, 0, j))]
        args += [res, mod]
    return pl.pallas_call(
        functools.partial(_matmul_kernel, act=act, has_bias=bias is not None, has_res=res is not None),
        grid=(_div(m, tm), _div(n, tn)),
        in_specs=in_specs,
        out_specs=pl.BlockSpec((tm, tn), lambda i, j: (i, j)),
        out_shape=jax.ShapeDtypeStruct((m, n), out_dtype),
        compiler_params=_cparams("arbitrary", "arbitrary"),
        name="matmul",
    )(*args)


def _ffn1_kernel(x_ref, xp_ref, xn_ref, wg_ref, wu_ref, cw_ref, cb_ref, o_ref, *, tm, n_lat_rows,
                 seq_lat, seq_ctx):
    i = pl.program_id(0)
    x = x_ref[...]
    wg = wg_ref[...]
    g = jnp.dot(x, wg, preferred_element_type=F32)
    gp = jnp.dot(xp_ref[...], wg, preferred_element_type=F32)
    gn = jnp.dot(xn_ref[...], wg, preferred_element_type=F32)
    row, first, last = _seq_edges(i, tm, n_lat_rows, seq_lat, seq_ctx)
    g_prev, g_next = _shift_rows(g, gp, gn, row, first, last, tm)
    gte = g_prev * cw_ref[0:1] + g * cw_ref[1:2] + g_next * cw_ref[2:3] + cb_ref[...]
    up = jnp.dot(x, wu_ref[...], preferred_element_type=F32)
    o_ref[...] = (_silu(gte) * up).astype(o_ref.dtype)


def _halo_specs(tm, kdim, m):
    nb = tm // SUBLANES
    prev = pl.BlockSpec((SUBLANES, kdim), lambda i, j: (jnp.maximum(i * nb - 1, 0), 0))
    nxt = pl.BlockSpec((SUBLANES, kdim), lambda i, j: (jnp.minimum((i + 1) * nb, m // SUBLANES - 1), 0))
    return prev, nxt


def _cast_pad_kernel(w_ref, o_ref):
    n = w_ref.shape[1]
    o_ref[:, :n] = w_ref[...].astype(o_ref.dtype)
    o_ref[:, n:] = jnp.zeros((o_ref.shape[0], o_ref.shape[1] - n), o_ref.dtype)


def _cast_pad_cols(w, n_pad):
    layers, kdim, n = w.shape
    tr = 256
    return pl.pallas_call(
        _cast_pad_kernel,
        grid=(layers, _div(kdim, tr)),
        in_specs=[pl.BlockSpec((None, tr, n), lambda l, i: (l, i, 0))],
        out_specs=pl.BlockSpec((None, tr, n_pad), lambda l, i: (l, i, 0)),
        out_shape=jax.ShapeDtypeStruct((layers, kdim, n_pad), BF16),
        compiler_params=_cparams("arbitrary", "arbitrary"),
        name="cast_pad",
    )(w)


def _ffn1(u, wg, wu, cw, cb, layer, *, rows, n_lat_rows, seq_lat, seq_ctx):
    _, kdim, f = wg.shape
    tm, tn = 1024, 512
    prev, nxt = _halo_specs(tm, kdim, rows)
    return pl.pallas_call(
        functools.partial(_ffn1_kernel, tm=tm, n_lat_rows=n_lat_rows, seq_lat=seq_lat, seq_ctx=seq_ctx),
        grid=(_div(rows, tm), _div(f, tn)),
        in_specs=[pl.BlockSpec((tm, kdim), lambda i, j: (i, 0)), prev, nxt,
                  pl.BlockSpec((None, kdim, tn), lambda i, j: (layer, 0, j)),
                  pl.BlockSpec((None, kdim, tn), lambda i, j: (layer, 0, j)),
                  pl.BlockSpec((3, tn), lambda i, j: (0, j)),
                  pl.BlockSpec((1, tn), lambda i, j: (0, j))],
        out_specs=pl.BlockSpec((tm, tn), lambda i, j: (i, j)),
        out_shape=jax.ShapeDtypeStruct((rows, f), BF16),
        compiler_params=_cparams("arbitrary", "arbitrary"),
        name="ffn_gate_up",
    )(u, u, u, wg, wu, cw, cb)


def _pool_kernel(u_ref, w_ref, b_ref, s_ref, h_ref, gate_ref, o_ref, xp_ref, *, seq):
    grp = pl.program_id(1)
    pad = SUBLANES
    zeros = jnp.zeros((pad, u_ref.shape[1]), F32)
    xp_ref[0:pad] = zeros
    xp_ref[pad:pad + seq] = u_ref[...]
    xp_ref[pad + seq:2 * pad + seq] = zeros
    rt = min(seq, 256)
    for gi, win in enumerate(POOL_WINDOWS):
        @pl.when(grp == gi)
        def _(win=win):
            left, right = win // 2, win - 1 - win // 2
            for rc in range(seq // rt):
                base = pad + rc * rt
                t = rc * rt + lax.broadcasted_iota(jnp.int32, (rt, 1), 0)
                acc = xp_ref[base - left:base - left + rt]
                for j in range(1, win):
                    acc = acc + xp_ref[base - left + j:base - left + j + rt]
                cnt = (jnp.minimum(t + right, seq - 1) - jnp.maximum(t - left, 0) + 1).astype(F32)
                p = acc / cnt - xp_ref[base:base + rt]
                y = jnp.dot(p.astype(BF16), w_ref[0], preferred_element_type=F32) + b_ref[...]
                rows = slice(rc * rt, (rc + 1) * rt)
                o_ref[rows] = h_ref[rows] + gate_ref[0] * (y * s_ref[...])


def _pool_mix(u, w, b, scale, h, mod, *, seq, row0, nseq, seg0):
    d = u.shape[1]
    cg = d // len(POOL_WINDOWS)
    blk0 = row0 // seq
    tile = pl.BlockSpec((seq, cg), lambda s, g: (blk0 + s, g))
    vec = pl.BlockSpec((1, cg), lambda s, g: (0, g))
    return pl.pallas_call(
        functools.partial(_pool_kernel, seq=seq),
        grid=(nseq, len(POOL_WINDOWS)),
        in_specs=[tile, pl.BlockSpec((1, cg, cg), lambda s, g: (g, 0, 0)), vec, vec, tile,
                  pl.BlockSpec((1, 1, cg), lambda s, g: ((seg0(s)) * 6 + 2, 0, g))],
        out_specs=tile,
        out_shape=jax.ShapeDtypeStruct(h.shape, F32),
        scratch_shapes=[pltpu.VMEM((seq + 2 * SUBLANES, cg), F32)],
        input_output_aliases={4: 0},
        compiler_params=_cparams("arbitrary", "arbitrary"),
        name="pool_mix",
    )(u, w, b.reshape(1, d), scale.reshape(1, d), h, mod)


def _na_lat_kernel(q_ref, k_ref, v_ref, kc_ref, vc_ref, cos_ref, sin_ref, bias_ref, o_ref,
                   q_s, k_s, *, rows, kh):
    hd = NA_HEAD_DIM
    lane = lax.broadcasted_iota(jnp.int32, (1, LANES), 1)
    low = (lane % 32) < 16
    head0 = lane < hd
    cos, sin = cos_ref[...], sin_ref[...]

    def rope(x):
        swapped = jnp.where(low, pltpu.roll(x, LANES - 16, axis=1), pltpu.roll(x, 16, axis=1))
        return x * cos + swapped * sin

    q_s[...] = (rope(q_ref[...].astype(F32)) * (hd ** -0.5)).astype(BF16)
    k_s[...] = rope(k_ref[...].astype(F32)).astype(BF16)
    nloc = kh * GRID_W
    rows_per_step = 8
    dot = lambda a, b: jnp.dot(a, b, preferred_element_type=F32)
    zero = jnp.zeros((), BF16)

    def body(step, carry):
        n = range(rows_per_step)
        r = [step * rows_per_step + dr for dr in n]
        rs = [jnp.clip(r[i] - kh // 2, 0, rows - kh) for i in n]
        q0 = [pl.multiple_of(r[i] * GRID_W, GRID_W) for i in n]
        k0 = [pl.multiple_of(rs[i] * GRID_W, GRID_W) for i in n]
        qr = [q_s[pl.ds(q0[i], GRID_W), :] for i in n]
        qm = [jnp.concatenate([jnp.where(head0, qr[i], zero), jnp.where(head0, zero, qr[i])], axis=0) for i in n]
        s_loc = [_nt(qm[i], k_s[pl.ds(k0[i], nloc), :]) + bias_ref[r[i] - rs[i]] for i in n]
        s_ctx = [_nt(qm[i], kc_ref[...]) for i in n]
        m = [jnp.maximum(jnp.max(s_loc[i], axis=-1, keepdims=True), jnp.max(s_ctx[i], axis=-1, keepdims=True))
             for i in n]
        p_loc = [jnp.exp(s_loc[i] - m[i]) for i in n]
        p_ctx = [jnp.exp(s_ctx[i] - m[i]) for i in n]
        den = [jnp.sum(p_loc[i], axis=-1, keepdims=True) + jnp.sum(p_ctx[i], axis=-1, keepdims=True) for i in n]
        o = [dot(p_loc[i].astype(BF16), v_ref[pl.ds(k0[i], nloc), :]) + dot(p_ctx[i].astype(BF16), vc_ref[...])
             for i in n]
        o = [o[i] / den[i] for i in n]
        for i in n:
            o_ref[pl.ds(q0[i], GRID_W), :] = jnp.where(head0, o[i][:GRID_W], o[i][GRID_W:]).astype(o_ref.dtype)
        return carry

    lax.fori_loop(0, _div(rows, rows_per_step), body, 0)


def _na_ctx_kernel(q_ref, k_ref, v_ref, o_alias_ref, o_ref):
    del o_alias_ref
    hd = NA_HEAD_DIM
    q = q_ref[...] * (hd ** -0.5)
    k = k_ref[...]
    v = v_ref[...]
    for hh in range(2):
        ls = slice(hh * hd, (hh + 1) * hd)
        s = _nt(q[:, ls].astype(BF16), k[:, ls].astype(BF16))
        m = jnp.max(s, axis=-1, keepdims=True)
        p = jnp.exp(s - m)
        den = jnp.sum(p, axis=-1, keepdims=True)
        o = jnp.dot(p.astype(BF16), v[:, ls].astype(BF16), preferred_element_type=F32)
        o_ref[:, ls] = (o / den).astype(o_ref.dtype)


def _rope_tables(t_len, hd):
    half = hd // 2
    pos = jnp.arange(t_len)
    inv_freq = ROPE_BASE ** (-jnp.arange(0, half, 2, dtype=F32) / half)
    d = jnp.arange(hd)
    p = jnp.where(d[None, :] < half, (pos // GRID_W)[:, None], (pos % GRID_W)[:, None]).astype(F32)
    ang = p * inv_freq[d % (half // 2)][None, :]
    cos = jnp.cos(ang)
    sin = jnp.where((d % half) < half // 2, -jnp.sin(ang), jnp.sin(ang))
    return jnp.tile(cos, (1, 2)), jnp.tile(sin, (1, 2))


def _na_bias_table(rpb, kh):
    col = jnp.arange(GRID_W)
    col_start = jnp.clip(col - NA_KW // 2, 0, GRID_W - NA_KW)
    col_ok = (col[None, :] >= col_start[:, None]) & (col[None, :] < col_start[:, None] + NA_KW)
    dc = jnp.clip(col[None, :] - col[:, None], -(NA_KW - 1), NA_KW - 1) + NA_KW - 1
    rpb_cols = rpb[:, :, dc]
    dr = jnp.arange(kh)[None, :] - jnp.arange(kh)[:, None] + NA_KH - 1
    tab = rpb_cols[:, dr]
    tab = jnp.where(col_ok[None, None, None], tab, -1e30)
    heads = rpb.shape[0]
    tab = tab.transpose(0, 1, 3, 2, 4).reshape(heads // 2, 2, kh, GRID_W, kh * GRID_W)
    return tab.transpose(0, 2, 1, 3, 4).reshape(heads // 2, kh, 2 * GRID_W, kh * GRID_W)


def _neighbourhood_attention(qkv, rpb, *, bsz, t_len, l_len):
    d = qkv.shape[1] // 3
    hd = NA_HEAD_DIM
    heads = d // hd
    npair = d // LANES
    rows = t_len // GRID_W
    kh = min(NA_KH, rows)
    n_lat = bsz * t_len
    n_all = qkv.shape[0]
    ctx0 = _div(n_lat, l_len)
    cos, sin = _rope_tables(t_len, hd)
    bias = _na_bias_table(rpb, kh)
    lat_blk = lambda off: pl.BlockSpec((t_len, LANES), lambda b, p: (b, off + p))
    ctx_blk = lambda off: pl.BlockSpec((l_len, LANES), lambda b, p: (ctx0 + b, off + p))
    tab_blk = pl.BlockSpec((t_len, LANES), lambda b, p: (0, 0))
    out = jax.ShapeDtypeStruct((n_all, d), BF16)
    o = pl.pallas_call(
        functools.partial(_na_lat_kernel, rows=rows, kh=kh),
        grid=(bsz, npair),
        in_specs=[lat_blk(0), lat_blk(npair), lat_blk(2 * npair), ctx_blk(npair), ctx_blk(2 * npair),
                  tab_blk, tab_blk,
                  pl.BlockSpec((None, kh, 2 * GRID_W, kh * GRID_W), lambda b, p: (p, 0, 0, 0))],
        out_specs=lat_blk(0),
        out_shape=out,
        scratch_shapes=[pltpu.VMEM((t_len, LANES), BF16), pltpu.VMEM((t_len, LANES), BF16)],
        compiler_params=_cparams("arbitrary", "arbitrary"),
        name="na_latent",
    )(qkv, qkv, qkv, qkv, qkv, cos, sin, bias)
    return pl.pallas_call(
        _na_ctx_kernel,
        grid=(bsz, npair),
        in_specs=[ctx_blk(0), ctx_blk(npair), ctx_blk(2 * npair), pl.BlockSpec(memory_space=pl.ANY)],
        out_specs=ctx_blk(0),
        out_shape=out,
        input_output_aliases={3: 0},
        compiler_params=_cparams("arbitrary", "arbitrary"),
        name="na_context",
    )(qkv, qkv, qkv, o)


def _sg_kernel(z_u_ref, z_v_ref, g_ref, ws_ref, bs_ref, o_ref, *, nchunk):
    zv = z_v_ref[...].astype(F32)
    zv = zv * lax.rsqrt(jnp.mean(zv * zv, axis=-1, keepdims=True) + NORM_EPS) * g_ref[...]
    zvb = zv.astype(BF16)
    for c in range(nchunk):
        rs = slice(c * SG_CHUNK, (c + 1) * SG_CHUNK)
        for g in range(SG_GROUPS):
            ls = slice(g * LANES, (g + 1) * LANES)
            mixed = jnp.dot(ws_ref[g], zvb[rs, ls], preferred_element_type=F32) + bs_ref[:, ls]
            o_ref[rs, ls] = (z_u_ref[rs, ls].astype(F32) * mixed).astype(o_ref.dtype)


def _spatial_gate(z, norm_g, w_s, b_s):
    m, two_w = z.shape
    width = two_w // 2
    nchunk = 2
    tm = nchunk * SG_CHUNK
    bs_full = jnp.repeat(b_s.T, width // SG_GROUPS, axis=1)
    return pl.pallas_call(
        functools.partial(_sg_kernel, nchunk=nchunk),
        grid=(_div(m, tm),),
        in_specs=[pl.BlockSpec((tm, width), lambda i: (i, 0)),
                  pl.BlockSpec((tm, width), lambda i: (i, 1)),
                  pl.BlockSpec((1, width), lambda i: (0, 0)),
                  pl.BlockSpec((SG_GROUPS, SG_CHUNK, SG_CHUNK), lambda i: (0, 0, 0)),
                  pl.BlockSpec((SG_CHUNK, width), lambda i: (0, 0))],
        out_specs=pl.BlockSpec((tm, width), lambda i: (i, 0)),
        out_shape=jax.ShapeDtypeStruct((m, width), BF16),
        compiler_params=_cparams("arbitrary"),
        name="spatial_gate",
    )(z, z, norm_g.reshape(1, width), w_s.astype(BF16), bs_full)


def _shift_kernel(u_ref, up_ref, un_ref, mu_ref, *o_refs, tm, n_lat_rows, seq_lat, seq_ctx):
    i = pl.program_id(0)
    u = u_ref[...]
    row, first, last = _seq_edges(i, tm, n_lat_rows, seq_lat, seq_ctx)
    u_prev, u_next = _shift_rows(u, up_ref[...], un_ref[...], row, first, last, tm)
    prev = u_prev - u
    nxt = u_next - u
    nmix = len(o_refs)
    for n in range(nmix):
        o_refs[n][...] = (u + prev * mu_ref[n:n + 1] + nxt * mu_ref[nmix + n:nmix + n + 1]).astype(BF16)


def _token_shift(u, mu, *, n_lat_rows, seq_lat, seq_ctx):
    m, d = u.shape
    nmix = mu.shape[1]
    tm, tn = 512, 512
    nb = tm // SUBLANES
    tile = pl.BlockSpec((tm, tn), lambda i, j: (i, j))
    prev = pl.BlockSpec((SUBLANES, tn), lambda i, j: (jnp.maximum(i * nb - 1, 0), j))
    nxt = pl.BlockSpec((SUBLANES, tn), lambda i, j: (jnp.minimum((i + 1) * nb, m // SUBLANES - 1), j))
    return pl.pallas_call(
        functools.partial(_shift_kernel, tm=tm, n_lat_rows=n_lat_rows, seq_lat=seq_lat, seq_ctx=seq_ctx),
        grid=(_div(m, tm), _div(d, tn)),
        in_specs=[tile, prev, nxt, pl.BlockSpec((2 * nmix, tn), lambda i, j: (0, j))],
        out_specs=[tile] * nmix,
        out_shape=[jax.ShapeDtypeStruct((m, d), BF16)] * nmix,
        compiler_params=_cparams("arbitrary", "arbitrary"),
        name="token_shift",
    )(u, u, u, mu.reshape(2 * nmix, d))


def _split2(x):
    hi = x.astype(BF16)
    lo = (x - hi.astype(F32)).astype(BF16)
    return hi, lo


def _chunk_chains(kt_ref, rt_ref, bh_ref, kh_ref, v_ref, bt_ref, kbt_ref, m_o, c_o, rp_o, y0_o, *, rev,
                  nheads):
    c = SCAN_CHUNK
    hd = RW_HEAD_DIM
    ri = lax.broadcasted_iota(jnp.int32, (c, c), 0)
    ci = lax.broadcasted_iota(jnp.int32, (c, c), 1)
    strict = (ci > ri) if rev else (ci < ri)
    incl = (ci >= ri) if rev else (ci <= ri)
    diag_blk = (ri // SOLVE_BLOCK) == (ci // SOLVE_BLOCK)
    eye = jnp.where(ri == ci, 1.0, 0.0)
    chains = [(half, hh) for half in range(LANES // c) for hh in range(nheads)]
    rows = [slice(half * c, (half + 1) * c) for half, _ in chains]
    ls = [slice(hh * hd, (hh + 1) * hd) for _, hh in chains]
    pair = [hh // 2 for _, hh in chains]
    sub = [slice((hh % 2) * hd, (hh % 2 + 1) * hd) for _, hh in chains]
    n = range(len(chains))
    dot = lambda a, b: jnp.dot(a, b, preferred_element_type=F32)
    kt = [kt_ref[rows[i], ls[i]] for i in n]
    rt = [rt_ref[rows[i], ls[i]] for i in n]
    vv = [v_ref[rows[i], ls[i]] for i in n]
    ktrt = [jnp.concatenate([kt[i], rt[i]], axis=0) for i in n]
    pb = [_nt(ktrt[i], bh_ref[rows[i], ls[i]]) for i in n]
    pk = [_nt(ktrt[i], kh_ref[rows[i], ls[i]]) for i in n]
    a1 = [jnp.where(strict, pb[i][:c], 0.0) for i in n]
    a3 = [jnp.where(incl, pb[i][c:], 0.0).astype(BF16) for i in n]
    ad = [jnp.where(diag_blk, a1[i], 0.0) for i in n]
    ao = [(a1[i] - ad[i]).astype(BF16) for i in n]
    tinv = [eye - ad[i] for i in n]
    pw = [_mm(ad[i], ad[i]) for i in n]
    lhs_v = [jnp.concatenate([jnp.where(strict, pk[i][:c], 0.0).astype(BF16),
                              jnp.where(incl, pk[i][c:], 0.0).astype(BF16),
                              kbt_ref[pair[i], sub[i], rows[i]]], axis=0) for i in n]
    pv = [dot(lhs_v[i], vv[i]) for i in n]
    tinv = [tinv[i] + _mm(tinv[i], pw[i]) for i in n]
    for _ in range(int(math.log2(SOLVE_BLOCK)) - 2):
        pw = [_mm(pw[i], pw[i]) for i in n]
        tinv = [tinv[i] + _mm(tinv[i], pw[i]) for i in n]
    rhs = [jnp.concatenate([kt[i].astype(F32), pv[i][:c]], axis=1).astype(BF16) for i in n]
    tinv = [t.astype(BF16) for t in tinv]
    nmat = [dot(tinv[i], ao[i]).astype(BF16) for i in n]
    x1 = [dot(tinv[i], rhs[i]) for i in n]
    x = x1
    for _ in range(c // SOLVE_BLOCK - 1):
        x = [x1[i] - dot(nmat[i], x[i].astype(BF16)) for i in n]
    lhs_x = [jnp.concatenate([bt_ref[pair[i], sub[i], rows[i]], a3[i]], axis=0) for i in n]
    px = [dot(lhs_x[i], x[i].astype(BF16)) for i in n]
    for i, (half, hh) in enumerate(chains):
        m_o[hh, 0, :, rows[i]] = (-px[i][:c, :hd]).astype(m_o.dtype)
        c_o[hh, 0, :, rows[i]] = pv[i][2 * c:] - px[i][:c, hd:]
        rp_o[rows[i], ls[i]] = (rt[i].astype(F32) - px[i][c:, :hd]).astype(rp_o.dtype)
        y0_o[rows[i], ls[i]] = pv[i][c:2 * c] - px[i][c:, hd:]


def _rw_chunk_kernel(k_ref, r_ref, v_ref, hw_ref, ha_ref, w2_ref, a2_ref, w0_ref, a0_ref, kk_ref, ka_ref,
                     rk_ref, tri_ref, bc_o, mf_o, cf_o, rpf_o, y0f_o, gf_o, mr_o, cr_o, rpr_o, y0r_o, gr_o,
                     kt_s, rt_s, bh_s, kh_s, v_s, bt_s, kbt_s, *, lora, nheads):
    npair = nheads // 2
    lane = lax.broadcasted_iota(jnp.int32, (1, LANES), 1)
    head0 = lane < RW_HEAD_DIM
    pairs = [slice(p * LANES, (p + 1) * LANES) for p in range(npair)]
    head_sum = lambda x: jnp.concatenate([_head_sum(x[:, ps], head0) for ps in pairs], axis=1)
    k = k_ref[...]
    r = r_ref[...]
    v_s[...] = v_ref[...].astype(BF16)
    kkv = k * kk_ref[...]
    kk = kkv * lax.rsqrt(jnp.maximum(head_sum(kkv * kkv), 1e-12))
    ka = ka_ref[...]
    rk = rk_ref[...]
    hw = hw_ref[...]
    ha = ha_ref[...]
    log_decay, beta, key = [], [], []
    for d in range(2):
        ls = slice(d * lora, (d + 1) * lora)
        pre = w0_ref[d] + jnp.dot(hw[:, ls], w2_ref[d], preferred_element_type=F32)
        z = -pre
        softplus = jnp.maximum(z, 0.0) + jnp.log(1.0 + jnp.exp(-jnp.abs(z)))
        log_decay.append(-jnp.exp(-softplus - 0.5))
        a = jax.nn.sigmoid(a0_ref[d] + jnp.dot(ha[:, ls], a2_ref[d], preferred_element_type=F32))
        beta.append(kk * a)
        key.append(k * (1.0 + (a - 1.0) * ka))
    bc_o[...] = head_sum(r * key[0] * rk) + head_sum(r * key[1] * rk)
    g_o = (gf_o, gr_o)
    for d in range(2):
        ld = log_decay[d]
        pieces = _split2(ld)
        csum = [sum(jnp.dot(tri_ref[e], p, preferred_element_type=F32) for p in pieces) for e in range(2)]
        cum = csum[d]
        total = csum[0] + csum[1] - ld
        kt_s[d] = (kk * jnp.exp(cum - ld)).astype(BF16)
        rt_s[d] = (r * jnp.exp(cum)).astype(BF16)
        inv = jnp.exp(-cum)
        bh_s[d] = (beta[d] * inv).astype(BF16)
        kh_s[d] = (key[d] * inv).astype(BF16)
        tail = jnp.exp(total - cum)
        bbar = beta[d] * tail
        kbar = key[d] * tail
        gdec = jnp.exp(total)
        for p, ps in enumerate(pairs):
            bt_s[d, p] = jnp.transpose(bbar[:, ps]).astype(BF16)
            kbt_s[d, p] = jnp.transpose(kbar[:, ps]).astype(BF16)
            g_o[d][p, 0] = jnp.transpose(gdec[:, ps])
    outs = ((mf_o, cf_o, rpf_o, y0f_o), (mr_o, cr_o, rpr_o, y0r_o))
    for d in range(2):
        _chunk_chains(kt_s.at[d], rt_s.at[d], bh_s.at[d], kh_s.at[d], v_s, bt_s.at[d], kbt_s.at[d],
                      *outs[d], rev=d == 1, nheads=nheads)


def _rw_chunks(k, r, v, hw, ha, w2, a2, w0, a0, k_k, k_a, r_k, *, bsz, t_len, l_len, lora):
    m, d = k.shape
    hd = RW_HEAD_DIM
    nheads = 8
    width = nheads * hd
    npair = nheads // 2
    tt = t_len + l_len
    ng = _div(tt, LANES)
    ctx_groups = _div(l_len, LANES)
    ctx0 = _div(bsz * t_len, LANES)

    def in_blk(b, g):
        return jnp.where(g < ctx_groups, ctx0 + b * ctx_groups + g,
                         b * (t_len // LANES) + g - ctx_groups)

    ri = jnp.arange(LANES)[:, None]
    ci = jnp.arange(LANES)[None, :]
    same = (ri // SCAN_CHUNK) == (ci // SCAN_CHUNK)
    tri = jnp.stack([same & (ci <= ri), same & (ci >= ri)]).astype(BF16)
    tile = pl.BlockSpec((LANES, width), lambda b, p, g: (in_blk(b, g), p))
    lora_blk = pl.BlockSpec((LANES, 2 * lora), lambda b, p, g: (in_blk(b, g), 0))
    wl_blk = pl.BlockSpec((2, lora, width), lambda b, p, g: (0, 0, p))
    v2_blk = pl.BlockSpec((2, 1, width), lambda b, p, g: (0, 0, p))
    v1_blk = pl.BlockSpec((1, width), lambda b, p, g: (0, p))
    tri_blk = pl.BlockSpec((2, LANES, LANES), lambda b, p, g: (0, 0, 0))
    row_o = pl.BlockSpec((None, LANES, width), lambda b, p, g: (b, g, p))
    sq_o = pl.BlockSpec((None, nheads, 1, hd, LANES), lambda b, p, g: (b, p, g, 0, 0))
    tr_o = pl.BlockSpec((None, npair, 1, LANES, LANES), lambda b, p, g: (b, p, g, 0, 0))
    row_shape = lambda dt: jax.ShapeDtypeStruct((bsz, tt, d), dt)
    sq_shape = lambda dt: jax.ShapeDtypeStruct((bsz, d // hd, ng, hd, LANES), dt)
    tr_shape = jax.ShapeDtypeStruct((bsz, d // LANES, ng, LANES, LANES), F32)
    dir_specs = [sq_o, sq_o, row_o, row_o, tr_o]
    dir_shapes = [sq_shape(BF16), sq_shape(F32), row_shape(BF16), row_shape(F32), tr_shape]
    operand = lambda: pltpu.VMEM((2, LANES, width), BF16)
    transposed = lambda: pltpu.VMEM((2, npair, LANES, LANES), BF16)
    out = pl.pallas_call(
        functools.partial(_rw_chunk_kernel, lora=lora, nheads=nheads),
        grid=(bsz, _div(d, width), ng),
        in_specs=[tile, tile, tile, lora_blk, lora_blk, wl_blk, wl_blk, v2_blk, v2_blk, v1_blk, v1_blk, v1_blk,
                  tri_blk],
        out_specs=[row_o] + dir_specs + dir_specs,
        out_shape=[row_shape(F32)] + dir_shapes + dir_shapes,
        scratch_shapes=[operand(), operand(), operand(), operand(), pltpu.VMEM((LANES, width), BF16),
                        transposed(), transposed()],
        compiler_params=_cparams("arbitrary", "arbitrary", "arbitrary"),
        name="rwkv_chunks",
    )(k, r, v, hw, ha, w2, a2, w0.reshape(2, 1, d), a0.reshape(2, 1, d), k_k.reshape(1, d),
      k_a.reshape(1, d), r_k.reshape(1, d), tri)
    return out[0], out[1:6], out[6:11]


def _state_pass_kernel(mf_ref, cf_ref, gf_ref, mr_ref, cr_ref, gr_ref, sf_o, sr_o, *, ctx_groups, n_groups,
                       nheads):
    c = SCAN_CHUNK
    hd = RW_HEAD_DIM
    heads = range(nheads)
    sf_o[...] = jnp.zeros(sf_o.shape, sf_o.dtype)
    sr_o[...] = jnp.zeros(sr_o.shape, sr_o.dtype)
    dirs = ((mf_ref, cf_ref, gf_ref, sf_o, False), (mr_ref, cr_ref, gr_ref, sr_o, True))

    def group(i, carry):
        out = []
        g_of = [i, jnp.where(i < ctx_groups, ctx_groups - 1 - i, n_groups - 1 - (i - ctx_groups))]
        states = [list(carry[0]), list(carry[1])]
        for step in range(LANES // c):
            s0b, ms = [[], []], [[], []]
            for di, (m_ref, c_ref, g_ref, s_o, rev) in enumerate(dirs):
                half = (LANES // c - 1 - step) if rev else step
                ts = slice(half * c, (half + 1) * c)
                for hh in heads:
                    sb = states[di][hh].astype(BF16)
                    s0b[di].append(sb)
                    blk = slice((hh % 2) * hd, (hh % 2 + 1) * hd)
                    s_o[hh // 2, g_of[di] * (LANES // c) + half, blk, blk] = sb
            for di, (m_ref, c_ref, g_ref, s_o, rev) in enumerate(dirs):
                half = (LANES // c - 1 - step) if rev else step
                ts = slice(half * c, (half + 1) * c)
                ms[di] = [jnp.dot(m_ref[hh, g_of[di], :, ts], s0b[di][hh], preferred_element_type=F32)
                          for hh in heads]
            for di, (m_ref, c_ref, g_ref, s_o, rev) in enumerate(dirs):
                half = (LANES // c - 1 - step) if rev else step
                ts = slice(half * c, (half + 1) * c)
                for hh in heads:
                    blk = slice((hh % 2) * hd, (hh % 2 + 1) * hd)
                    gc = g_ref[hh // 2, g_of[di], blk, half * c:half * c + 1]
                    states[di][hh] = gc * states[di][hh] + ms[di][hh] + c_ref[hh, g_of[di], :, ts]
        return tuple(states[0]), tuple(states[1])

    zero = tuple(jnp.zeros((hd, hd), F32) for _ in heads)
    lax.fori_loop(0, n_groups, group, (zero, zero))


def _state_pass(m_f, c_f, g_f, m_r, c_r, g_r, *, l_len):
    bsz, heads, ng, hd, _ = m_f.shape
    nheads = 4
    npair = nheads // 2
    nc = ng * (LANES // SCAN_CHUNK)
    sq = pl.BlockSpec((None, nheads, ng, hd, LANES), lambda b, p: (b, p, 0, 0, 0))
    tr = pl.BlockSpec((None, npair, ng, LANES, LANES), lambda b, p: (b, p, 0, 0, 0))
    st = pl.BlockSpec((None, npair, nc, LANES, LANES), lambda b, p: (b, p, 0, 0, 0))
    st_shape = jax.ShapeDtypeStruct((bsz, heads // 2, nc, LANES, LANES), BF16)
    return pl.pallas_call(
        functools.partial(_state_pass_kernel, ctx_groups=l_len // LANES, n_groups=ng, nheads=nheads),
        grid=(bsz, _div(heads, nheads)),
        in_specs=[sq, sq, tr, sq, sq, tr],
        out_specs=[st, st],
        out_shape=[st_shape, st_shape],
        compiler_params=_cparams("arbitrary", "arbitrary"),
        name="rwkv_state",
    )(m_f, c_f, g_f, m_r, c_r, g_r)


def _readout_kernel(rpf_ref, y0f_ref, sf_ref, rpr_ref, y0r_ref, sr_ref, bc_ref, v_ref, g_ref, lng_ref,
                    lnb_ref, o_ref):
    c = SCAN_CHUNK
    lane = lax.broadcasted_iota(jnp.int32, (1, LANES), 1)
    head0 = lane < RW_HEAD_DIM
    inv_n = 1.0 / RW_HEAD_DIM
    tiles = [(cc, pp) for cc in range(o_ref.shape[0] // c) for pp in range(o_ref.shape[1] // LANES)]
    rows = [slice(cc * c, (cc + 1) * c) for cc, _ in tiles]
    ls = [slice(pp * LANES, (pp + 1) * LANES) for _, pp in tiles]
    n = range(len(tiles))
    dot = lambda a, b: jnp.dot(a, b, preferred_element_type=F32)
    yf = [dot(rpf_ref[rows[i], ls[i]], sf_ref[tiles[i][1], tiles[i][0]]) for i in n]
    yr = [dot(rpr_ref[rows[i], ls[i]], sr_ref[tiles[i][1], tiles[i][0]]) for i in n]
    for i in n:
        y = (yf[i] + y0f_ref[rows[i], ls[i]]) + (yr[i] + y0r_ref[rows[i], ls[i]])
        mean = _head_sum(y, head0) * inv_n
        yc = y - mean
        var = _head_sum(yc * yc, head0) * inv_n
        yn = yc * lax.rsqrt(var + RW_GN_EPS) * lng_ref[:, ls[i]] + lnb_ref[:, ls[i]]
        o_ref[rows[i], ls[i]] = ((yn + bc_ref[rows[i], ls[i]] * v_ref[rows[i], ls[i]])
                                 * g_ref[rows[i], ls[i]]).astype(o_ref.dtype)


def _rw_readout(rp_f, y0_f, s_f, rp_r, y0_r, s_r, bc, v, g, ln_g, ln_b, *, bsz, t_len, l_len):
    d = v.shape[1]
    tm, tn = 256, 512
    per_b = t_len // tm
    off = l_len // tm
    blk = lambda i: (i // per_b, off + i % per_b)
    scan_blk = pl.BlockSpec((None, tm, tn), lambda i, p: (*blk(i), p))
    st_blk = pl.BlockSpec((None, tn // LANES, tm // SCAN_CHUNK, LANES, LANES),
                          lambda i, p: (blk(i)[0], p, blk(i)[1], 0, 0))
    tile = pl.BlockSpec((tm, tn), lambda i, p: (i, p))
    vec = pl.BlockSpec((1, tn), lambda i, p: (0, p))
    return pl.pallas_call(
        _readout_kernel,
        grid=(_div(bsz * t_len, tm), _div(d, tn)),
        in_specs=[scan_blk, scan_blk, st_blk, scan_blk, scan_blk, st_blk, scan_blk, tile, tile, vec, vec],
        out_specs=tile,
        out_shape=jax.ShapeDtypeStruct((bsz * t_len, d), BF16),
        compiler_params=_cparams("arbitrary", "arbitrary"),
        name="rwkv_readout",
    )(rp_f, y0_f, s_f, rp_r, y0_r, s_r, bc, v, g, ln_g.reshape(1, d), ln_b.reshape(1, d))


def _pad_cols(w, n):
    return jnp.pad(w, ((0, 0), (0, n - w.shape[1])))


def _pad_rows(w, n):
    return jnp.pad(w, ((0, n - w.shape[0]), (0, 0)))


def kernel(x, c, ctx, c_ctx, norm1_g, norm2_g, w_mod, b_mod, ffn_w_gate, ffn_w_up, ffn_conv_w, ffn_conv_b, ffn_w_down, final_norm_g, pool_w, pool_b, pool_scale, na_w_qkv, na_rpb, na_w_o, sg_w_in, sg_b_in, sg_norm_g, sg_w_s, sg_b_s, sg_w_o, rw_mu, rw_w_rkv, rw_w0, rw_w1, rw_w2, rw_a0, rw_a1, rw_a2, rw_g1, rw_g2, rw_k_k, rw_k_a, rw_r_k, rw_ln_g, rw_ln_b, rw_w_o):
    bsz, t_len, d = x.shape
    l_len = ctx.shape[1]
    depth = norm1_g.shape[0]
    n_mixers = 4
    n_lat = bsz * t_len
    n_all = n_lat + bsz * l_len
    f = ffn_w_gate.shape[2]
    f_pad = -(-f // 512) * 512
    wg_all = _cast_pad_cols(ffn_w_gate, f_pad)
    wu_all = _cast_pad_cols(ffn_w_up, f_pad)
    wd_all = ffn_w_down.astype(BF16)

    cvec = jnp.concatenate([c, c_ctx[None], jnp.zeros((SUBLANES - bsz - 1, d), F32)], axis=0)
    mods = _modulation(cvec, w_mod, b_mod)
    h = jnp.concatenate([x.reshape(n_lat, d), ctx.reshape(bsz * l_len, d)], axis=0)
    seg = dict(seq=t_len, nseg=bsz)
    edges = dict(n_lat_rows=n_lat, seq_lat=t_len, seq_ctx=l_len)

    for i in range(depth):
        m_kind, j = i % n_mixers, i // n_mixers
        last = i == depth - 1
        mod = mods[i].reshape(SUBLANES * 6, 1, d)
        rows = n_lat if last else n_all
        if m_kind == 0:
            u = _norm(h, norm1_g[i], mod, 0, 1, t_len, bsz, F32, rows=rows)
            pw = pool_w[j].astype(BF16)
            h = _pool_mix(u, pw, pool_b[j], pool_scale[j], h, mod, seq=t_len, row0=0, nseq=bsz,
                          seg0=lambda s: s)
            if not last:
                h = _pool_mix(u, pw, pool_b[j], pool_scale[j], h, mod, seq=l_len, row0=n_lat, nseq=bsz,
                              seg0=lambda s: bsz)
        elif m_kind == 1:
            u = _norm(h, norm1_g[i], mod, 0, 1, t_len, bsz, BF16)
            qkv = _matmul(u, na_w_qkv[j].astype(BF16), out_dtype=BF16)
            o = _neighbourhood_attention(qkv, na_rpb[j], bsz=bsz, t_len=t_len, l_len=l_len)
            h = _matmul(o, na_w_o[j].astype(BF16), res=h, mod=mod, k_gate=2, rows=rows, **seg)
        elif m_kind == 2:
            u = _norm(h, norm1_g[i], mod, 0, 1, t_len, bsz, BF16, rows=rows)
            z = _matmul(u, sg_w_in[j].astype(BF16), bias=sg_b_in[j], act="gelu", out_dtype=BF16)
            gated = _spatial_gate(z, sg_norm_g[j], sg_w_s[j], sg_b_s[j])
            h = _matmul(gated, sg_w_o[j].astype(BF16), res=h, mod=mod, k_gate=2, rows=rows, **seg)
        else:
            lora = LANES
            u = _norm(h, norm1_g[i], mod, 0, 1, t_len, bsz, F32)
            xr, xw, xk, xv, xa, xg = _token_shift(u, rw_mu[j], **edges)
            w_rkv = rw_w_rkv.astype(BF16)
            r = _matmul(xr, w_rkv, w_lead=(j, 0))
            k = _matmul(xk, w_rkv, w_lead=(j, 1))
            v = _matmul(xv, w_rkv, w_lead=(j, 2))
            w1 = jnp.concatenate([_pad_cols(rw_w1[j, e], lora) for e in range(2)], axis=1).astype(BF16)
            a1 = jnp.concatenate([_pad_cols(rw_a1[j, e], lora) for e in range(2)], axis=1).astype(BF16)
            w2 = jnp.stack([_pad_rows(rw_w2[j, e], lora) for e in range(2)]).astype(BF16)
            a2 = jnp.stack([_pad_rows(rw_a2[j, e], lora) for e in range(2)]).astype(BF16)
            hw = _matmul(xw, w1, act="tanh", out_dtype=BF16)
            ha = _matmul(xa, a1, out_dtype=BF16)
            hg = _matmul(xg, rw_g1[j].astype(BF16), act="sigmoid", out_dtype=BF16, rows=n_lat)
            g = _matmul(hg, rw_g2[j].astype(BF16))
            bc, (m_f, c_f, rp_f, y0_f, g_f), (m_r, c_r, rp_r, y0_r, g_r) = _rw_chunks(
                k, r, v, hw, ha, w2, a2, rw_w0[j], rw_a0[j], rw_k_k[j], rw_k_a[j], rw_r_k[j],
                bsz=bsz, t_len=t_len, l_len=l_len, lora=lora)
            s_f, s_r = _state_pass(m_f, c_f, g_f, m_r, c_r, g_r, l_len=l_len)
            o = _rw_readout(rp_f, y0_f, s_f, rp_r, y0_r, s_r, bc, v, g, rw_ln_g[j], rw_ln_b[j],
                            bsz=bsz, t_len=t_len, l_len=l_len)
            h = _matmul(o, rw_w_o[j].astype(BF16), res=h, mod=mod, k_gate=2, rows=n_lat, **seg)
        u2 = _norm(h, norm2_g[i], mod, 3, 4, t_len, bsz, BF16, rows=rows)
        cw = _pad_cols(ffn_conv_w[i], f_pad)
        cb = _pad_cols(ffn_conv_b[i][None], f_pad)
        mid = _ffn1(u2, wg_all, wu_all, cw, cb, i, rows=rows, **edges)
        h = _matmul(mid, wd_all, w_lead=(i,), res=h, mod=mod, k_gate=5, rows=rows, tm=1024, **seg)

    out = _norm(h, final_norm_g, None, 0, 0, t_len, bsz, F32, rows=n_lat)
    return out.reshape(bsz, t_len, d)
```

```python
import functools
import math

import jax
import jax.numpy as jnp
from jax import lax
from jax.experimental import pallas as pl
from jax.experimental.pallas import tpu as pltpu

F32 = jnp.float32
BF16 = jnp.bfloat16

GRID_W = 64
NORM_EPS = 1e-6
POOL_WINDOWS = (2, 4, 8, 16)
NA_HEAD_DIM = 64
NA_KH = 8
NA_KW = 16
ROPE_BASE = 10000.0
SG_CHUNK = 128
SG_GROUPS = 16
RW_HEAD_DIM = 64
RW_GN_EPS = 64e-5

LANES = 128
SUBLANES = 8
BF16_ROWS = 16
VMEM_LIMIT = 56 * 1024 * 1024

SCAN_CHUNK = 64
SOLVE_BLOCK = 16


def _cparams(*sem):
    return pltpu.CompilerParams(dimension_semantics=sem, vmem_limit_bytes=VMEM_LIMIT)


def _div(a, b):
    assert a % b == 0, (a, b)
    return a // b


def _nt(a, b):
    return lax.dot_general(a, b, (((1,), (1,)), ((), ())), preferred_element_type=F32)


def _mm(a, b):
    return jnp.dot(a.astype(BF16), b.astype(BF16), preferred_element_type=F32)


def _silu(x):
    return x * jax.nn.sigmoid(x)


def _head_sum(x, lane_is_head0):
    s0 = jnp.sum(jnp.where(lane_is_head0, x, 0.0), axis=-1, keepdims=True)
    s1 = jnp.sum(jnp.where(lane_is_head0, 0.0, x), axis=-1, keepdims=True)
    return jnp.where(lane_is_head0, s0, s1)


def _seq_edges(i, tm, n_lat_rows, seq_lat, seq_ctx):
    row = lax.broadcasted_iota(jnp.int32, (tm, 1), 0)
    grow = i * tm + row
    slen = jnp.where(grow < n_lat_rows, seq_lat, seq_ctx)
    pos = grow & (slen - 1)
    return row, pos == 0, pos == slen - 1


def _shift_rows(x, halo_prev, halo_next, row, first, last, tm):
    xp = pltpu.roll(x, 1, axis=0)
    xp = jnp.where(row == 0, halo_prev[SUBLANES - 1:SUBLANES], xp)
    xp = jnp.where(first, 0.0, xp)
    xn = pltpu.roll(x, tm - 1, axis=0)
    xn = jnp.where(row == tm - 1, halo_next[0:1], xn)
    xn = jnp.where(last, 0.0, xn)
    return xp, xn


def _shift_rows_tile(x, halo_prev, halo_next, i, tm, n_lat_rows, seq_lat, seq_ctx, interior):
    r0 = i * tm
    slen = jnp.where(r0 < n_lat_rows, seq_lat, seq_ctx)
    starts = (r0 & (slen - 1)) == 0
    ends = ((r0 + tm) & (slen - 1)) == 0
    row8 = lax.broadcasted_iota(jnp.int32, (SUBLANES, 1), 0)
    xp = pltpu.roll(x, 1, axis=0)
    top = jnp.where(row8 == 0, jnp.where(starts, 0.0, halo_prev[SUBLANES - 1:SUBLANES]), xp[:SUBLANES])
    xp = jnp.concatenate([top, xp[SUBLANES:]], axis=0)
    xn = pltpu.roll(x, tm - 1, axis=0)
    bot = jnp.where(row8 == SUBLANES - 1, jnp.where(ends, 0.0, halo_next[0:1]), xn[tm - SUBLANES:])
    xn = jnp.concatenate([xn[:tm - SUBLANES], bot], axis=0)
    if interior:
        _, first, last = _seq_edges(i, tm, n_lat_rows, seq_lat, seq_ctx)
        xp = jnp.where(first, 0.0, xp)
        xn = jnp.where(last, 0.0, xn)
    return xp, xn


def _mod_kernel(c_ref, w_ref, b_ref, o_ref):
    s = _silu(c_ref[...]).astype(BF16)
    o_ref[0] = jnp.dot(s, w_ref[0].astype(BF16), preferred_element_type=F32) + b_ref[0]


def _modulation(cvec, w_mod, b_mod):
    depth, d, n = w_mod.shape
    tn = 1024
    return pl.pallas_call(
        _mod_kernel,
        grid=(depth, _div(n, tn)),
        in_specs=[pl.BlockSpec((SUBLANES, d), lambda l, j: (0, 0)),
                  pl.BlockSpec((1, d, tn), lambda l, j: (l, 0, j)),
                  pl.BlockSpec((1, 1, tn), lambda l, j: (l, 0, j))],
        out_specs=pl.BlockSpec((1, SUBLANES, tn), lambda l, j: (l, 0, j)),
        out_shape=jax.ShapeDtypeStruct((depth, SUBLANES, n), F32),
        compiler_params=_cparams("arbitrary", "arbitrary"),
        name="modulation",
    )(cvec, w_mod, b_mod.reshape(depth, 1, n))


def _norm_kernel(*refs, modulate):
    if modulate:
        h_ref, g_ref, sh_ref, sc_ref, o_ref = refs
    else:
        h_ref, g_ref, o_ref = refs
    x = h_ref[...]
    y = x * lax.rsqrt(jnp.mean(x * x, axis=-1, keepdims=True) + NORM_EPS) * g_ref[...]
    if modulate:
        y = y * (1.0 + sc_ref[0]) + sh_ref[0]
    o_ref[...] = y.astype(o_ref.dtype)


def _norm(h, g, mod, k_shift, k_scale, seq, nseg, out_dtype, rows=None):
    m, d = h.shape if rows is None else (rows, h.shape[1])
    tm = 1024
    row_spec = pl.BlockSpec((tm, d), lambda i: (i, 0))
    in_specs = [row_spec, pl.BlockSpec((1, d), lambda i: (0, 0))]
    args = [h, g.reshape(1, d)]
    if mod is not None:
        seg = lambda i: jnp.minimum((i * tm) // seq, nseg)
        in_specs += [pl.BlockSpec((1, 1, d), lambda i: (seg(i) * 6 + k_shift, 0, 0)),
                     pl.BlockSpec((1, 1, d), lambda i: (seg(i) * 6 + k_scale, 0, 0))]
        args += [mod, mod]
    return pl.pallas_call(
        functools.partial(_norm_kernel, modulate=mod is not None),
        grid=(_div(m, tm),),
        in_specs=in_specs,
        out_specs=row_spec,
        out_shape=jax.ShapeDtypeStruct((m, d), out_dtype),
        compiler_params=_cparams("arbitrary"),
        name="rmsnorm_mod",
    )(*args)


def _matmul_kernel(*refs, act, has_bias, has_res):
    x_ref, w_ref = refs[0], refs[1]
    k = 2
    acc = jnp.dot(x_ref[:, :w_ref.shape[0]], w_ref[...], preferred_element_type=F32)
    if has_bias:
        acc = acc + refs[k][...]
        k += 1
    if act == "gelu":
        acc = jax.nn.gelu(acc, approximate=True)
    elif act == "tanh":
        acc = jnp.tanh(acc)
    elif act == "sigmoid":
        acc = jax.nn.sigmoid(acc)
    if has_res:
        acc = refs[k][...] + refs[k + 1][0] * acc
        k += 2
    o_ref = refs[k]
    o_ref[...] = acc.astype(o_ref.dtype)


def _matmul(x, w, *, bias=None, act=None, res=None, mod=None, k_gate=None, seq=None, nseg=None,
            out_dtype=F32, tm=1024, tn=512, rows=None, w_lead=()):
    m = x.shape[0] if rows is None else rows
    kdim, n = w.shape[-2:]
    tn = min(tn, n)
    tm = min(tm, m)
    assert x.shape[1] >= kdim and (x.shape[1] == kdim or kdim % LANES == 0)
    in_specs = [pl.BlockSpec((tm, x.shape[1]), lambda i, j: (i, 0)),
                pl.BlockSpec((None,) * len(w_lead) + (kdim, tn), lambda i, j: (*w_lead, 0, j))]
    args = [x, w]
    if bias is not None:
        in_specs.append(pl.BlockSpec((1, tn), lambda i, j: (0, j)))
        args.append(bias.reshape(1, n))
    if res is not None:
        seg = lambda i: jnp.minimum((i * tm) // seq, nseg)
        in_specs += [pl.BlockSpec((tm, tn), lambda i, j: (i, j)),
                     pl.BlockSpec((1, 1, tn), lambda i, j: (seg(i) * 6 + k_gate, 0, j))]
        args += [res, mod]
    return pl.pallas_call(
        functools.partial(_matmul_kernel, act=act, has_bias=bias is not None, has_res=res is not None),
        grid=(_div(m, tm), _div(n, tn)),
        in_specs=in_specs,
        out_specs=pl.BlockSpec((tm, tn), lambda i, j: (i, j)),
        out_shape=jax.ShapeDtypeStruct((m, n), out_dtype),
        compiler_params=_cparams("arbitrary", "arbitrary"),
        name="matmul",
    )(*args)


def _ffn1_kernel(x_ref, xp_ref, xn_ref, wg_ref, wu_ref, cw_ref, cb_ref, o_ref, *, tm, n_lat_rows,
                 seq_lat, seq_ctx):
    i = pl.program_id(0)
    x = x_ref[...]
    hr = xp_ref.shape[0]
    g_ext = jnp.dot(jnp.concatenate([xp_ref[...], x, xn_ref[...]], axis=0), wg_ref[...],
                    preferred_element_type=F32)
    g = g_ext[hr:hr + tm]
    gp = g_ext[hr - SUBLANES:hr]
    gn = g_ext[hr + tm:hr + tm + SUBLANES]
    g_prev, g_next = _shift_rows_tile(g, gp, gn, i, tm, n_lat_rows, seq_lat, seq_ctx,
                                      interior=min(seq_lat, seq_ctx) < tm)
    gte = g_prev * cw_ref[0:1] + g * cw_ref[1:2] + g_next * cw_ref[2:3] + cb_ref[...]
    up = jnp.dot(x, wu_ref[...], preferred_element_type=F32)
    o_ref[...] = (_silu(gte) * up).astype(o_ref.dtype)


def _halo_specs(tm, kdim, m):
    hr = BF16_ROWS
    nb = tm // hr
    prev = pl.BlockSpec((hr, kdim), lambda i, j: (jnp.maximum(i * nb - 1, 0), 0))
    nxt = pl.BlockSpec((hr, kdim), lambda i, j: (jnp.minimum((i + 1) * nb, m // hr - 1), 0))
    return prev, nxt


def _cast_pad_kernel(w_ref, o_ref):
    n = w_ref.shape[1]
    o_ref[:, :n] = w_ref[...].astype(o_ref.dtype)
    o_ref[:, n:] = jnp.zeros((o_ref.shape[0], o_ref.shape[1] - n), o_ref.dtype)


def _cast_pad_cols(w, n_pad):
    layers, kdim, n = w.shape
    tr = 256
    return pl.pallas_call(
        _cast_pad_kernel,
        grid=(layers, _div(kdim, tr)),
        in_specs=[pl.BlockSpec((None, tr, n), lambda l, i: (l, i, 0))],
        out_specs=pl.BlockSpec((None, tr, n_pad), lambda l, i: (l, i, 0)),
        out_shape=jax.ShapeDtypeStruct((layers, kdim, n_pad), BF16),
        compiler_params=_cparams("arbitrary", "arbitrary"),
        name="cast_pad",
    )(w)


def _ffn1(u, wg, wu, cw, cb, layer, *, rows, n_lat_rows, seq_lat, seq_ctx):
    _, kdim, f = wg.shape
    tm, tn = 1024, 512
    assert n_lat_rows % tm == 0 and all(s % tm == 0 or tm % s == 0 for s in (seq_lat, seq_ctx))
    prev, nxt = _halo_specs(tm, kdim, rows)
    return pl.pallas_call(
        functools.partial(_ffn1_kernel, tm=tm, n_lat_rows=n_lat_rows, seq_lat=seq_lat, seq_ctx=seq_ctx),
        grid=(_div(rows, tm), _div(f, tn)),
        in_specs=[pl.BlockSpec((tm, kdim), lambda i, j: (i, 0)), prev, nxt,
                  pl.BlockSpec((None, kdim, tn), lambda i, j: (layer, 0, j)),
                  pl.BlockSpec((None, kdim, tn), lambda i, j: (layer, 0, j)),
                  pl.BlockSpec((3, tn), lambda i, j: (0, j)),
                  pl.BlockSpec((1, tn), lambda i, j: (0, j))],
        out_specs=pl.BlockSpec((tm, tn), lambda i, j: (i, j)),
        out_shape=jax.ShapeDtypeStruct((rows, f), BF16),
        compiler_params=_cparams("arbitrary", "arbitrary"),
        name="ffn_gate_up",
    )(u, u, u, wg, wu, cw, cb)


def _pool_kernel(u_ref, w_ref, b_ref, s_ref, h_ref, gate_ref, o_ref, xp_ref, *, seq):
    grp = pl.program_id(1)
    pad = SUBLANES
    zeros = jnp.zeros((pad, u_ref.shape[1]), F32)
    xp_ref[0:pad] = zeros
    xp_ref[pad:pad + seq] = u_ref[...]
    xp_ref[pad + seq:2 * pad + seq] = zeros
    rt = min(seq, 256)
    for gi, win in enumerate(POOL_WINDOWS):
        @pl.when(grp == gi)
        def _(win=win):
            left, right = win // 2, win - 1 - win // 2
            for rc in range(seq // rt):
                base = pad + rc * rt
                t = rc * rt + lax.broadcasted_iota(jnp.int32, (rt, 1), 0)
                acc = xp_ref[base - left:base - left + rt]
                for j in range(1, win):
                    acc = acc + xp_ref[base - left + j:base - left + j + rt]
                cnt = (jnp.minimum(t + right, seq - 1) - jnp.maximum(t - left, 0) + 1).astype(F32)
                p = acc / cnt - xp_ref[base:base + rt]
                y = jnp.dot(p.astype(BF16), w_ref[0], preferred_element_type=F32) + b_ref[...]
                rows = slice(rc * rt, (rc + 1) * rt)
                o_ref[rows] = h_ref[rows] + gate_ref[0] * (y * s_ref[...])


def _pool_mix(u, w, b, scale, h, mod, *, seq, row0, nseq, seg0):
    d = u.shape[1]
    cg = d // len(POOL_WINDOWS)
    blk0 = row0 // seq
    tile = pl.BlockSpec((seq, cg), lambda s, g: (blk0 + s, g))
    vec = pl.BlockSpec((1, cg), lambda s, g: (0, g))
    return pl.pallas_call(
        functools.partial(_pool_kernel, seq=seq),
        grid=(nseq, len(POOL_WINDOWS)),
        in_specs=[tile, pl.BlockSpec((1, cg, cg), lambda s, g: (g, 0, 0)), vec, vec, tile,
                  pl.BlockSpec((1, 1, cg), lambda s, g: ((seg0(s)) * 6 + 2, 0, g))],
        out_specs=tile,
        out_shape=jax.ShapeDtypeStruct(h.shape, F32),
        scratch_shapes=[pltpu.VMEM((seq + 2 * SUBLANES, cg), F32)],
        input_output_aliases={4: 0},
        compiler_params=_cparams("arbitrary", "arbitrary"),
        name="pool_mix",
    )(u, w, b.reshape(1, d), scale.reshape(1, d), h, mod)


def _na_lat_kernel(q_ref, k_ref, v_ref, kc_ref, vc_ref, cos_ref, sin_ref, bias_ref, o_ref,
                   q_s, k_s, *, rows, kh):
    hd = NA_HEAD_DIM
    lane = lax.broadcasted_iota(jnp.int32, (1, LANES), 1)
    low = (lane % 32) < 16
    head0 = lane < hd
    cos, sin = cos_ref[...], sin_ref[...]

    def rope(x):
        swapped = jnp.where(low, pltpu.roll(x, LANES - 16, axis=1), pltpu.roll(x, 16, axis=1))
        return x * cos + swapped * sin

    q_s[...] = (rope(q_ref[...].astype(F32)) * (hd ** -0.5)).astype(BF16)
    k_s[...] = rope(k_ref[...].astype(F32)).astype(BF16)
    nloc = kh * GRID_W
    rows_per_step = 8
    dot = lambda a, b: jnp.dot(a, b, preferred_element_type=F32)
    zero = jnp.zeros((), BF16)

    def body(step, carry):
        n = range(rows_per_step)
        r = [step * rows_per_step + dr for dr in n]
        rs = [jnp.clip(r[i] - kh // 2, 0, rows - kh) for i in n]
        q0 = [pl.multiple_of(r[i] * GRID_W, GRID_W) for i in n]
        k0 = [pl.multiple_of(rs[i] * GRID_W, GRID_W) for i in n]
        qr = [q_s[pl.ds(q0[i], GRID_W), :] for i in n]
        qm = [jnp.concatenate([jnp.where(head0, qr[i], zero), jnp.where(head0, zero, qr[i])], axis=0) for i in n]
        s_loc = [_nt(qm[i], k_s[pl.ds(k0[i], nloc), :]) + bias_ref[r[i] - rs[i]] for i in n]
        s_ctx = [_nt(qm[i], kc_ref[...]) for i in n]
        m = [jnp.maximum(jnp.max(s_loc[i], axis=-1, keepdims=True), jnp.max(s_ctx[i], axis=-1, keepdims=True))
             for i in n]
        p_loc = [jnp.exp(s_loc[i] - m[i]) for i in n]
        p_ctx = [jnp.exp(s_ctx[i] - m[i]) for i in n]
        den = [jnp.sum(p_loc[i], axis=-1, keepdims=True) + jnp.sum(p_ctx[i], axis=-1, keepdims=True) for i in n]
        o = [dot(p_loc[i].astype(BF16), v_ref[pl.ds(k0[i], nloc), :]) + dot(p_ctx[i].astype(BF16), vc_ref[...])
             for i in n]
        o = [o[i] / den[i] for i in n]
        for i in n:
            o_ref[pl.ds(q0[i], GRID_W), :] = jnp.where(head0, o[i][:GRID_W], o[i][GRID_W:]).astype(o_ref.dtype)
        return carry

    lax.fori_loop(0, _div(rows, rows_per_step), body, 0)


def _na_ctx_kernel(q_ref, k_ref, v_ref, o_alias_ref, o_ref):
    del o_alias_ref
    hd = NA_HEAD_DIM
    q = q_ref[...] * (hd ** -0.5)
    k = k_ref[...]
    v = v_ref[...]
    for hh in range(2):
        ls = slice(hh * hd, (hh + 1) * hd)
        s = _nt(q[:, ls].astype(BF16), k[:, ls].astype(BF16))
        m = jnp.max(s, axis=-1, keepdims=True)
        p = jnp.exp(s - m)
        den = jnp.sum(p, axis=-1, keepdims=True)
        o = jnp.dot(p.astype(BF16), v[:, ls].astype(BF16), preferred_element_type=F32)
        o_ref[:, ls] = (o / den).astype(o_ref.dtype)


def _rope_tables(t_len, hd):
    half = hd // 2
    pos = jnp.arange(t_len)
    inv_freq = ROPE_BASE ** (-jnp.arange(0, half, 2, dtype=F32) / half)
    d = jnp.arange(hd)
    p = jnp.where(d[None, :] < half, (pos // GRID_W)[:, None], (pos % GRID_W)[:, None]).astype(F32)
    ang = p * inv_freq[d % (half // 2)][None, :]
    cos = jnp.cos(ang)
    sin = jnp.where((d % half) < half // 2, -jnp.sin(ang), jnp.sin(ang))
    return jnp.tile(cos, (1, 2)), jnp.tile(sin, (1, 2))


def _na_bias_table(rpb, kh):
    col = jnp.arange(GRID_W)
    col_start = jnp.clip(col - NA_KW // 2, 0, GRID_W - NA_KW)
    col_ok = (col[None, :] >= col_start[:, None]) & (col[None, :] < col_start[:, None] + NA_KW)
    dc = jnp.clip(col[None, :] - col[:, None], -(NA_KW - 1), NA_KW - 1) + NA_KW - 1
    rpb_cols = rpb[:, :, dc]
    dr = jnp.arange(kh)[None, :] - jnp.arange(kh)[:, None] + NA_KH - 1
    tab = rpb_cols[:, dr]
    tab = jnp.where(col_ok[None, None, None], tab, -1e30)
    heads = rpb.shape[0]
    tab = tab.transpose(0, 1, 3, 2, 4).reshape(heads // 2, 2, kh, GRID_W, kh * GRID_W)
    return tab.transpose(0, 2, 1, 3, 4).reshape(heads // 2, kh, 2 * GRID_W, kh * GRID_W)


def _neighbourhood_attention(qkv, rpb, *, bsz, t_len, l_len):
    d = qkv.shape[1] // 3
    hd = NA_HEAD_DIM
    heads = d // hd
    npair = d // LANES
    rows = t_len // GRID_W
    kh = min(NA_KH, rows)
    n_lat = bsz * t_len
    n_all = qkv.shape[0]
    ctx0 = _div(n_lat, l_len)
    cos, sin = _rope_tables(t_len, hd)
    bias = _na_bias_table(rpb, kh)
    lat_blk = lambda off: pl.BlockSpec((t_len, LANES), lambda b, p: (b, off + p))
    ctx_blk = lambda off: pl.BlockSpec((l_len, LANES), lambda b, p: (ctx0 + b, off + p))
    tab_blk = pl.BlockSpec((t_len, LANES), lambda b, p: (0, 0))
    out = jax.ShapeDtypeStruct((n_all, d), BF16)
    o = pl.pallas_call(
        functools.partial(_na_lat_kernel, rows=rows, kh=kh),
        grid=(bsz, npair),
        in_specs=[lat_blk(0), lat_blk(npair), lat_blk(2 * npair), ctx_blk(npair), ctx_blk(2 * npair),
                  tab_blk, tab_blk,
                  pl.BlockSpec((None, kh, 2 * GRID_W, kh * GRID_W), lambda b, p: (p, 0, 0, 0))],
        out_specs=lat_blk(0),
        out_shape=out,
        scratch_shapes=[pltpu.VMEM((t_len, LANES), BF16), pltpu.VMEM((t_len, LANES), BF16)],
        compiler_params=_cparams("arbitrary", "arbitrary"),
        name="na_latent",
    )(qkv, qkv, qkv, qkv, qkv, cos, sin, bias)
    return pl.pallas_call(
        _na_ctx_kernel,
        grid=(bsz, npair),
        in_specs=[ctx_blk(0), ctx_blk(npair), ctx_blk(2 * npair), pl.BlockSpec(memory_space=pl.ANY)],
        out_specs=ctx_blk(0),
        out_shape=out,
        input_output_aliases={3: 0},
        compiler_params=_cparams("arbitrary", "arbitrary"),
        name="na_context",
    )(qkv, qkv, qkv, o)


def _sg_kernel(z_u_ref, z_v_ref, g_ref, ws_ref, bs_ref, o_ref, *, nchunk):
    zv = z_v_ref[...].astype(F32)
    zv = zv * lax.rsqrt(jnp.mean(zv * zv, axis=-1, keepdims=True) + NORM_EPS) * g_ref[...]
    zvb = zv.astype(BF16)
    for c in range(nchunk):
        rs = slice(c * SG_CHUNK, (c + 1) * SG_CHUNK)
        for g in range(SG_GROUPS):
            ls = slice(g * LANES, (g + 1) * LANES)
            mixed = jnp.dot(ws_ref[g], zvb[rs, ls], preferred_element_type=F32) + bs_ref[:, ls]
            o_ref[rs, ls] = (z_u_ref[rs, ls].astype(F32) * mixed).astype(o_ref.dtype)


def _spatial_gate(z, norm_g, w_s, b_s):
    m, two_w = z.shape
    width = two_w // 2
    nchunk = 2
    tm = nchunk * SG_CHUNK
    bs_full = jnp.repeat(b_s.T, width // SG_GROUPS, axis=1)
    return pl.pallas_call(
        functools.partial(_sg_kernel, nchunk=nchunk),
        grid=(_div(m, tm),),
        in_specs=[pl.BlockSpec((tm, width), lambda i: (i, 0)),
                  pl.BlockSpec((tm, width), lambda i: (i, 1)),
                  pl.BlockSpec((1, width), lambda i: (0, 0)),
                  pl.BlockSpec((SG_GROUPS, SG_CHUNK, SG_CHUNK), lambda i: (0, 0, 0)),
                  pl.BlockSpec((SG_CHUNK, width), lambda i: (0, 0))],
        out_specs=pl.BlockSpec((tm, width), lambda i: (i, 0)),
        out_shape=jax.ShapeDtypeStruct((m, width), BF16),
        compiler_params=_cparams("arbitrary"),
        name="spatial_gate",
    )(z, z, norm_g.reshape(1, width), w_s.astype(BF16), bs_full)


def _shift_kernel(u_ref, up_ref, un_ref, mu_ref, *o_refs, tm, n_lat_rows, seq_lat, seq_ctx):
    i = pl.program_id(0)
    u = u_ref[...]
    row, first, last = _seq_edges(i, tm, n_lat_rows, seq_lat, seq_ctx)
    u_prev, u_next = _shift_rows(u, up_ref[...], un_ref[...], row, first, last, tm)
    prev = u_prev - u
    nxt = u_next - u
    nmix = len(o_refs)
    for n in range(nmix):
        o_refs[n][...] = (u + prev * mu_ref[n:n + 1] + nxt * mu_ref[nmix + n:nmix + n + 1]).astype(BF16)


def _token_shift(u, mu, *, n_lat_rows, seq_lat, seq_ctx):
    m, d = u.shape
    nmix = mu.shape[1]
    tm, tn = 512, 512
    nb = tm // SUBLANES
    tile = pl.BlockSpec((tm, tn), lambda i, j: (i, j))
    prev = pl.BlockSpec((SUBLANES, tn), lambda i, j: (jnp.maximum(i * nb - 1, 0), j))
    nxt = pl.BlockSpec((SUBLANES, tn), lambda i, j: (jnp.minimum((i + 1) * nb, m // SUBLANES - 1), j))
    return pl.pallas_call(
        functools.partial(_shift_kernel, tm=tm, n_lat_rows=n_lat_rows, seq_lat=seq_lat, seq_ctx=seq_ctx),
        grid=(_div(m, tm), _div(d, tn)),
        in_specs=[tile, prev, nxt, pl.BlockSpec((2 * nmix, tn), lambda i, j: (0, j))],
        out_specs=[tile] * nmix,
        out_shape=[jax.ShapeDtypeStruct((m, d), BF16)] * nmix,
        compiler_params=_cparams("arbitrary", "arbitrary"),
        name="token_shift",
    )(u, u, u, mu.reshape(2 * nmix, d))


def _split2(x):
    hi = x.astype(BF16)
    lo = (x - hi.astype(F32)).astype(BF16)
    return hi, lo


def _chunk_chains(dirs, v_ref, *, nheads):
    c = SCAN_CHUNK
    hd = RW_HEAD_DIM
    ri = lax.broadcasted_iota(jnp.int32, (c, c), 0)
    ci = lax.broadcasted_iota(jnp.int32, (c, c), 1)
    strict_d = [(ci > ri) if d[-1] else (ci < ri) for d in dirs]
    incl_d = [(ci >= ri) if d[-1] else (ci <= ri) for d in dirs]
    diag_blk = (ri // SOLVE_BLOCK) == (ci // SOLVE_BLOCK)
    eye = jnp.where(ri == ci, 1.0, 0.0)
    chains = [(di, half, hh) for di in range(len(dirs)) for half in range(LANES // c) for hh in range(nheads)]
    src = [dirs[di] for di, _, _ in chains]
    strict = [strict_d[di] for di, _, _ in chains]
    incl = [incl_d[di] for di, _, _ in chains]
    rows = [slice(half * c, (half + 1) * c) for _, half, _ in chains]
    ls = [slice(hh * hd, (hh + 1) * hd) for _, _, hh in chains]
    pair = [hh // 2 for _, _, hh in chains]
    sub = [slice((hh % 2) * hd, (hh % 2 + 1) * hd) for _, _, hh in chains]
    n = range(len(chains))
    dot = lambda a, b: jnp.dot(a, b, preferred_element_type=F32)
    kt = [src[i][0][rows[i], ls[i]] for i in n]
    rt = [src[i][1][rows[i], ls[i]] for i in n]
    vv = [v_ref[rows[i], ls[i]] for i in n]
    ktrt = [jnp.concatenate([kt[i], rt[i]], axis=0) for i in n]
    pb = [_nt(ktrt[i], src[i][2][rows[i], ls[i]]) for i in n]
    pk = [_nt(ktrt[i], src[i][3][rows[i], ls[i]]) for i in n]
    a1 = [jnp.where(strict[i], pb[i][:c], 0.0) for i in n]
    a3 = [jnp.where(incl[i], pb[i][c:], 0.0).astype(BF16) for i in n]
    ad = [jnp.where(diag_blk, a1[i], 0.0) for i in n]
    ao = [(a1[i] - ad[i]).astype(BF16) for i in n]
    tinv = [eye - ad[i] for i in n]
    pw = [_mm(ad[i], ad[i]) for i in n]
    lhs_v = [jnp.concatenate([jnp.where(strict[i], pk[i][:c], 0.0).astype(BF16),
                              jnp.where(incl[i], pk[i][c:], 0.0).astype(BF16),
                              src[i][5][pair[i], sub[i], rows[i]]], axis=0) for i in n]
    pv = [dot(lhs_v[i], vv[i]) for i in n]
    tinv = [tinv[i] + _mm(tinv[i], pw[i]) for i in n]
    for _ in range(int(math.log2(SOLVE_BLOCK)) - 2):
        pw = [_mm(pw[i], pw[i]) for i in n]
        tinv = [tinv[i] + _mm(tinv[i], pw[i]) for i in n]
    rhs = [jnp.concatenate([kt[i].astype(F32), pv[i][:c]], axis=1).astype(BF16) for i in n]
    tinv = [t.astype(BF16) for t in tinv]
    nmat = [dot(tinv[i], ao[i]).astype(BF16) for i in n]
    x1 = [dot(tinv[i], rhs[i]) for i in n]
    x = x1
    for _ in range(c // SOLVE_BLOCK - 1):
        x = [x1[i] - dot(nmat[i], x[i].astype(BF16)) for i in n]
    lhs_x = [jnp.concatenate([src[i][4][pair[i], sub[i], rows[i]], a3[i]], axis=0) for i in n]
    px = [dot(lhs_x[i], x[i].astype(BF16)) for i in n]
    for i, (_, half, hh) in enumerate(chains):
        m_o, c_o, rp_o, y0_o = src[i][6]
        m_o[hh, 0, :, rows[i]] = (-px[i][:c, :hd]).astype(m_o.dtype)
        c_o[hh, 0, :, rows[i]] = pv[i][2 * c:] - px[i][:c, hd:]
        rp_o[rows[i], ls[i]] = (rt[i].astype(F32) - px[i][c:, :hd]).astype(rp_o.dtype)
        y0_o[rows[i], ls[i]] = pv[i][c:2 * c] - px[i][c:, hd:]


def _rw_chunk_kernel(k_ref, r_ref, v_ref, hw_ref, ha_ref, w2_ref, a2_ref, w0_ref, a0_ref, kk_ref, ka_ref,
                     rk_ref, tri_ref, bc_o, mf_o, cf_o, rpf_o, y0f_o, gf_o, mr_o, cr_o, rpr_o, y0r_o, gr_o,
                     kt_s, rt_s, bh_s, kh_s, v_s, bt_s, kbt_s, *, lora, nheads):
    npair = nheads // 2
    lane = lax.broadcasted_iota(jnp.int32, (1, LANES), 1)
    head0 = lane < RW_HEAD_DIM
    pairs = [slice(p * LANES, (p + 1) * LANES) for p in range(npair)]
    head_sum = lambda x: jnp.concatenate([_head_sum(x[:, ps], head0) for ps in pairs], axis=1)
    k = k_ref[...]
    r = r_ref[...]
    v_s[...] = v_ref[...].astype(BF16)
    kkv = k * kk_ref[...]
    kk = kkv * lax.rsqrt(jnp.maximum(head_sum(kkv * kkv), 1e-12))
    ka = ka_ref[...]
    rk = rk_ref[...]
    hw = hw_ref[...]
    ha = ha_ref[...]
    log_decay, beta, key = [], [], []
    for d in range(2):
        ls = slice(d * lora, (d + 1) * lora)
        pre = w0_ref[d] + jnp.dot(hw[:, ls], w2_ref[d], preferred_element_type=F32)
        z = -pre
        softplus = jnp.maximum(z, 0.0) + jnp.log(1.0 + jnp.exp(-jnp.abs(z)))
        log_decay.append(-jnp.exp(-softplus - 0.5))
        a = jax.nn.sigmoid(a0_ref[d] + jnp.dot(ha[:, ls], a2_ref[d], preferred_element_type=F32))
        beta.append(kk * a)
        key.append(k * (1.0 + (a - 1.0) * ka))
    bc_o[...] = head_sum(r * key[0] * rk) + head_sum(r * key[1] * rk)
    g_o = (gf_o, gr_o)
    for d in range(2):
        ld = log_decay[d]
        pieces = _split2(ld)
        csum = [sum(jnp.dot(tri_ref[e], p, preferred_element_type=F32) for p in pieces) for e in range(2)]
        cum = csum[d]
        total = csum[0] + csum[1] - ld
        kt_s[d] = (kk * jnp.exp(cum - ld)).astype(BF16)
        rt_s[d] = (r * jnp.exp(cum)).astype(BF16)
        inv = jnp.exp(-cum)
        bh_s[d] = (beta[d] * inv).astype(BF16)
        kh_s[d] = (key[d] * inv).astype(BF16)
        tail = jnp.exp(total - cum)
        bbar = beta[d] * tail
        kbar = key[d] * tail
        gdec = jnp.exp(total)
        for p, ps in enumerate(pairs):
            bt_s[d, p] = jnp.transpose(bbar[:, ps]).astype(BF16)
            kbt_s[d, p] = jnp.transpose(kbar[:, ps]).astype(BF16)
            g_o[d][p, 0] = jnp.transpose(gdec[:, ps])
    outs = ((mf_o, cf_o, rpf_o, y0f_o), (mr_o, cr_o, rpr_o, y0r_o))
    _chunk_chains([(kt_s.at[d], rt_s.at[d], bh_s.at[d], kh_s.at[d], bt_s.at[d], kbt_s.at[d], outs[d], d == 1)
                   for d in range(2)], v_s, nheads=nheads)


def _rw_chunks(k, r, v, hw, ha, w2, a2, w0, a0, k_k, k_a, r_k, *, bsz, t_len, l_len, lora):
    m, d = k.shape
    hd = RW_HEAD_DIM
    nheads = 8
    width = nheads * hd
    npair = nheads // 2
    tt = t_len + l_len
    ng = _div(tt, LANES)
    ctx_groups = _div(l_len, LANES)
    ctx0 = _div(bsz * t_len, LANES)

    def in_blk(b, g):
        return jnp.where(g < ctx_groups, ctx0 + b * ctx_groups + g,
                         b * (t_len // LANES) + g - ctx_groups)

    ri = jnp.arange(LANES)[:, None]
    ci = jnp.arange(LANES)[None, :]
    same = (ri // SCAN_CHUNK) == (ci // SCAN_CHUNK)
    tri = jnp.stack([same & (ci <= ri), same & (ci >= ri)]).astype(BF16)
    tile = pl.BlockSpec((LANES, width), lambda b, p, g: (in_blk(b, g), p))
    lora_blk = pl.BlockSpec((LANES, 2 * lora), lambda b, p, g: (in_blk(b, g), 0))
    wl_blk = pl.BlockSpec((2, lora, width), lambda b, p, g: (0, 0, p))
    v2_blk = pl.BlockSpec((2, 1, width), lambda b, p, g: (0, 0, p))
    v1_blk = pl.BlockSpec((1, width), lambda b, p, g: (0, p))
    tri_blk = pl.BlockSpec((2, LANES, LANES), lambda b, p, g: (0, 0, 0))
    row_o = pl.BlockSpec((None, LANES, width), lambda b, p, g: (b, g, p))
    sq_o = pl.BlockSpec((None, nheads, 1, hd, LANES), lambda b, p, g: (b, p, g, 0, 0))
    tr_o = pl.BlockSpec((None, npair, 1, LANES, LANES), lambda b, p, g: (b, p, g, 0, 0))
    row_shape = lambda dt: jax.ShapeDtypeStruct((bsz, tt, d), dt)
    sq_shape = lambda dt: jax.ShapeDtypeStruct((bsz, d // hd, ng, hd, LANES), dt)
    tr_shape = jax.ShapeDtypeStruct((bsz, d // LANES, ng, LANES, LANES), F32)
    dir_specs = [sq_o, sq_o, row_o, row_o, tr_o]
    dir_shapes = [sq_shape(BF16), sq_shape(F32), row_shape(BF16), row_shape(F32), tr_shape]
    operand = lambda: pltpu.VMEM((2, LANES, width), BF16)
    transposed = lambda: pltpu.VMEM((2, npair, LANES, LANES), BF16)
    out = pl.pallas_call(
        functools.partial(_rw_chunk_kernel, lora=lora, nheads=nheads),
        grid=(bsz, _div(d, width), ng),
        in_specs=[tile, tile, tile, lora_blk, lora_blk, wl_blk, wl_blk, v2_blk, v2_blk, v1_blk, v1_blk, v1_blk,
                  tri_blk],
        out_specs=[row_o] + dir_specs + dir_specs,
        out_shape=[row_shape(F32)] + dir_shapes + dir_shapes,
        scratch_shapes=[operand(), operand(), operand(), operand(), pltpu.VMEM((LANES, width), BF16),
                        transposed(), transposed()],
        compiler_params=_cparams("arbitrary", "arbitrary", "arbitrary"),
        name="rwkv_chunks",
    )(k, r, v, hw, ha, w2, a2, w0.reshape(2, 1, d), a0.reshape(2, 1, d), k_k.reshape(1, d),
      k_a.reshape(1, d), r_k.reshape(1, d), tri)
    return out[0], out[1:6], out[6:11]


def _state_pass_kernel(mf_ref, cf_ref, gf_ref, mr_ref, cr_ref, gr_ref, sf_o, sr_o, *, ctx_groups, n_groups,
                       nheads):
    c = SCAN_CHUNK
    hd = RW_HEAD_DIM
    heads = range(nheads)
    sf_o[...] = jnp.zeros(sf_o.shape, sf_o.dtype)
    sr_o[...] = jnp.zeros(sr_o.shape, sr_o.dtype)
    dirs = ((mf_ref, cf_ref, gf_ref, sf_o, False), (mr_ref, cr_ref, gr_ref, sr_o, True))

    def group(i, carry):
        out = []
        g_of = [i, jnp.where(i < ctx_groups, ctx_groups - 1 - i, n_groups - 1 - (i - ctx_groups))]
        states = [list(carry[0]), list(carry[1])]
        for step in range(LANES // c):
            s0b, ms = [[], []], [[], []]
            for di, (m_ref, c_ref, g_ref, s_o, rev) in enumerate(dirs):
                half = (LANES // c - 1 - step) if rev else step
                ts = slice(half * c, (half + 1) * c)
                for hh in heads:
                    sb = states[di][hh].astype(BF16)
                    s0b[di].append(sb)
                    blk = slice((hh % 2) * hd, (hh % 2 + 1) * hd)
                    s_o[hh // 2, g_of[di] * (LANES // c) + half, blk, blk] = sb
            for di, (m_ref, c_ref, g_ref, s_o, rev) in enumerate(dirs):
                half = (LANES // c - 1 - step) if rev else step
                ts = slice(half * c, (half + 1) * c)
                ms[di] = [jnp.dot(m_ref[hh, g_of[di], :, ts], s0b[di][hh], preferred_element_type=F32)
                          for hh in heads]
            for di, (m_ref, c_ref, g_ref, s_o, rev) in enumerate(dirs):
                half = (LANES // c - 1 - step) if rev else step
                ts = slice(half * c, (half + 1) * c)
                for hh in heads:
                    blk = slice((hh % 2) * hd, (hh % 2 + 1) * hd)
                    gc = g_ref[hh // 2, g_of[di], blk, half * c:half * c + 1]
                    states[di][hh] = gc * states[di][hh] + ms[di][hh] + c_ref[hh, g_of[di], :, ts]
        return tuple(states[0]), tuple(states[1])

    zero = tuple(jnp.zeros((hd, hd), F32) for _ in heads)
    lax.fori_loop(0, n_groups, group, (zero, zero))


def _state_pass(m_f, c_f, g_f, m_r, c_r, g_r, *, l_len):
    bsz, heads, ng, hd, _ = m_f.shape
    nheads = 4
    npair = nheads // 2
    nc = ng * (LANES // SCAN_CHUNK)
    sq = pl.BlockSpec((None, nheads, ng, hd, LANES), lambda b, p: (b, p, 0, 0, 0))
    tr = pl.BlockSpec((None, npair, ng, LANES, LANES), lambda b, p: (b, p, 0, 0, 0))
    st = pl.BlockSpec((None, npair, nc, LANES, LANES), lambda b, p: (b, p, 0, 0, 0))
    st_shape = jax.ShapeDtypeStruct((bsz, heads // 2, nc, LANES, LANES), BF16)
    return pl.pallas_call(
        functools.partial(_state_pass_kernel, ctx_groups=l_len // LANES, n_groups=ng, nheads=nheads),
        grid=(bsz, _div(heads, nheads)),
        in_specs=[sq, sq, tr, sq, sq, tr],
        out_specs=[st, st],
        out_shape=[st_shape, st_shape],
        compiler_params=_cparams("arbitrary", "arbitrary"),
        name="rwkv_state",
    )(m_f, c_f, g_f, m_r, c_r, g_r)


def _readout_kernel(rpf_ref, y0f_ref, sf_ref, rpr_ref, y0r_ref, sr_ref, bc_ref, v_ref, g_ref, lng_ref,
                    lnb_ref, o_ref):
    c = SCAN_CHUNK
    lane = lax.broadcasted_iota(jnp.int32, (1, LANES), 1)
    head0 = lane < RW_HEAD_DIM
    inv_n = 1.0 / RW_HEAD_DIM
    tiles = [(cc, pp) for cc in range(o_ref.shape[0] // c) for pp in range(o_ref.shape[1] // LANES)]
    rows = [slice(cc * c, (cc + 1) * c) for cc, _ in tiles]
    ls = [slice(pp * LANES, (pp + 1) * LANES) for _, pp in tiles]
    n = range(len(tiles))
    dot = lambda a, b: jnp.dot(a, b, preferred_element_type=F32)
    yf = [dot(rpf_ref[rows[i], ls[i]], sf_ref[tiles[i][1], tiles[i][0]]) for i in n]
    yr = [dot(rpr_ref[rows[i], ls[i]], sr_ref[tiles[i][1], tiles[i][0]]) for i in n]
    for i in n:
        y = (yf[i] + y0f_ref[rows[i], ls[i]]) + (yr[i] + y0r_ref[rows[i], ls[i]])
        mean = _head_sum(y, head0) * inv_n
        yc = y - mean
        var = _head_sum(yc * yc, head0) * inv_n
        yn = yc * lax.rsqrt(var + RW_GN_EPS) * lng_ref[:, ls[i]] + lnb_ref[:, ls[i]]
        o_ref[rows[i], ls[i]] = ((yn + bc_ref[rows[i], ls[i]] * v_ref[rows[i], ls[i]])
                                 * g_ref[rows[i], ls[i]]).astype(o_ref.dtype)


def _rw_readout(rp_f, y0_f, s_f, rp_r, y0_r, s_r, bc, v, g, ln_g, ln_b, *, bsz, t_len, l_len):
    d = v.shape[1]
    tm, tn = 256, 512
    per_b = t_len // tm
    off = l_len // tm
    blk = lambda i: (i // per_b, off + i % per_b)
    scan_blk = pl.BlockSpec((None, tm, tn), lambda i, p: (*blk(i), p))
    st_blk = pl.BlockSpec((None, tn // LANES, tm // SCAN_CHUNK, LANES, LANES),
                          lambda i, p: (blk(i)[0], p, blk(i)[1], 0, 0))
    tile = pl.BlockSpec((tm, tn), lambda i, p: (i, p))
    vec = pl.BlockSpec((1, tn), lambda i, p: (0, p))
    return pl.pallas_call(
        _readout_kernel,
        grid=(_div(bsz * t_len, tm), _div(d, tn)),
        in_specs=[scan_blk, scan_blk, st_blk, scan_blk, scan_blk, st_blk, scan_blk, tile, tile, vec, vec],
        out_specs=tile,
        out_shape=jax.ShapeDtypeStruct((bsz * t_len, d), BF16),
        compiler_params=_cparams("arbitrary", "arbitrary"),
        name="rwkv_readout",
    )(rp_f, y0_f, s_f, rp_r, y0_r, s_r, bc, v, g, ln_g.reshape(1, d), ln_b.reshape(1, d))


def _pad_cols(w, n):
    return jnp.pad(w, ((0, 0), (0, n - w.shape[1])))


def _pad_rows(w, n):
    return jnp.pad(w, ((0, n - w.shape[0]), (0, 0)))


def kernel(x, c, ctx, c_ctx, norm1_g, norm2_g, w_mod, b_mod, ffn_w_gate, ffn_w_up, ffn_conv_w, ffn_conv_b, ffn_w_down, final_norm_g, pool_w, pool_b, pool_scale, na_w_qkv, na_rpb, na_w_o, sg_w_in, sg_b_in, sg_norm_g, sg_w_s, sg_b_s, sg_w_o, rw_mu, rw_w_rkv, rw_w0, rw_w1, rw_w2, rw_a0, rw_a1, rw_a2, rw_g1, rw_g2, rw_k_k, rw_k_a, rw_r_k, rw_ln_g, rw_ln_b, rw_w_o):
    bsz, t_len, d = x.shape
    l_len = ctx.shape[1]
    depth = norm1_g.shape[0]
    n_mixers = 4
    n_lat = bsz * t_len
    n_all = n_lat + bsz * l_len
    f = ffn_w_gate.shape[2]
    f_pad = -(-f // 512) * 512
    wg_all = _cast_pad_cols(ffn_w_gate, f_pad)
    wu_all = _cast_pad_cols(ffn_w_up, f_pad)
    wd_all = ffn_w_down.astype(BF16)

    cvec = jnp.concatenate([c, c_ctx[None], jnp.zeros((SUBLANES - bsz - 1, d), F32)], axis=0)
    mods = _modulation(cvec, w_mod, b_mod)
    h = jnp.concatenate([x.reshape(n_lat, d), ctx.reshape(bsz * l_len, d)], axis=0)
    seg = dict(seq=t_len, nseg=bsz)
    edges = dict(n_lat_rows=n_lat, seq_lat=t_len, seq_ctx=l_len)

    for i in range(depth):
        m_kind, j = i % n_mixers, i // n_mixers
        last = i == depth - 1
        mod = mods[i].reshape(SUBLANES * 6, 1, d)
        rows = n_lat if last else n_all
        if m_kind == 0:
            u = _norm(h, norm1_g[i], mod, 0, 1, t_len, bsz, F32, rows=rows)
            pw = pool_w[j].astype(BF16)
            h = _pool_mix(u, pw, pool_b[j], pool_scale[j], h, mod, seq=t_len, row0=0, nseq=bsz,
                          seg0=lambda s: s)
            if not last:
                h = _pool_mix(u, pw, pool_b[j], pool_scale[j], h, mod, seq=l_len, row0=n_lat, nseq=bsz,
                              seg0=lambda s: bsz)
        elif m_kind == 1:
            u = _norm(h, norm1_g[i], mod, 0, 1, t_len, bsz, BF16)
            qkv = _matmul(u, na_w_qkv[j].astype(BF16), out_dtype=BF16)
            o = _neighbourhood_attention(qkv, na_rpb[j], bsz=bsz, t_len=t_len, l_len=l_len)
            h = _matmul(o, na_w_o[j].astype(BF16), res=h, mod=mod, k_gate=2, rows=rows, **seg)
        elif m_kind == 2:
            u = _norm(h, norm1_g[i], mod, 0, 1, t_len, bsz, BF16, rows=rows)
            z = _matmul(u, sg_w_in[j].astype(BF16), bias=sg_b_in[j], act="gelu", out_dtype=BF16)
            gated = _spatial_gate(z, sg_norm_g[j], sg_w_s[j], sg_b_s[j])
            h = _matmul(gated, sg_w_o[j].astype(BF16), res=h, mod=mod, k_gate=2, rows=rows, **seg)
        else:
            lora = LANES
            u = _norm(h, norm1_g[i], mod, 0, 1, t_len, bsz, F32)
            xr, xw, xk, xv, xa, xg = _token_shift(u, rw_mu[j], **edges)
            w_rkv = rw_w_rkv.astype(BF16)
            r = _matmul(xr, w_rkv, w_lead=(j, 0))
            k = _matmul(xk, w_rkv, w_lead=(j, 1))
            v = _matmul(xv, w_rkv, w_lead=(j, 2))
            w1 = jnp.concatenate([_pad_cols(rw_w1[j, e], lora) for e in range(2)], axis=1).astype(BF16)
            a1 = jnp.concatenate([_pad_cols(rw_a1[j, e], lora) for e in range(2)], axis=1).astype(BF16)
            w2 = jnp.stack([_pad_rows(rw_w2[j, e], lora) for e in range(2)]).astype(BF16)
            a2 = jnp.stack([_pad_rows(rw_a2[j, e], lora) for e in range(2)]).astype(BF16)
            hw = _matmul(xw, w1, act="tanh", out_dtype=BF16)
            ha = _matmul(xa, a1, out_dtype=BF16)
            hg = _matmul(xg, rw_g1[j].astype(BF16), act="sigmoid", out_dtype=BF16, rows=n_lat)
            g = _matmul(hg, rw_g2[j].astype(BF16))
            bc, (m_f, c_f, rp_f, y0_f, g_f), (m_r, c_r, rp_r, y0_r, g_r) = _rw_chunks(
                k, r, v, hw, ha, w2, a2, rw_w0[j], rw_a0[j], rw_k_k[j], rw_k_a[j], rw_r_k[j],
                bsz=bsz, t_len=t_len, l_len=l_len, lora=lora)
            s_f, s_r = _state_pass(m_f, c_f, g_f, m_r, c_r, g_r, l_len=l_len)
            o = _rw_readout(rp_f, y0_f, s_f, rp_r, y0_r, s_r, bc, v, g, rw_ln_g[j], rw_ln_b[j],
                            bsz=bsz, t_len=t_len, l_len=l_len)
            h = _matmul(o, rw_w_o[j].astype(BF16), res=h, mod=mod, k_gate=2, rows=n_lat, **seg)
        u2 = _norm(h, norm2_g[i], mod, 3, 4, t_len, bsz, BF16, rows=rows)
        cw = _pad_cols(ffn_conv_w[i], f_pad)
        cb = _pad_cols(ffn_conv_b[i][None], f_pad)
        mid = _ffn1(u2, wg_all, wu_all, cw, cb, i, rows=rows, **edges)
        h = _matmul(mid, wd_all, w_lead=(i,), res=h, mod=mod, k_gate=5, rows=rows, tm=1024, **seg)

    out = _norm(h, final_norm_g, None, 0, 0, t_len, bsz, F32, rows=n_lat)
    return out.reshape(bsz, t_len, d)
```

```python
import functools
import math

import jax
import jax.numpy as jnp
from jax import lax
from jax.experimental import pallas as pl
from jax.experimental.pallas import tpu as pltpu

F32 = jnp.float32
BF16 = jnp.bfloat16

GRID_W = 64
NORM_EPS = 1e-6
POOL_WINDOWS = (2, 4, 8, 16)
NA_HEAD_DIM = 64
NA_KH = 8
NA_KW = 16
ROPE_BASE = 10000.0
SG_CHUNK = 128
SG_GROUPS = 16
RW_HEAD_DIM = 64
RW_GN_EPS = 64e-5

LANES = 128
SUBLANES = 8
BF16_ROWS = 16
VMEM_LIMIT = 56 * 1024 * 1024

SCAN_CHUNK = 64
SOLVE_BLOCK = 16


def _cparams(*sem):
    return pltpu.CompilerParams(dimension_semantics=sem, vmem_limit_bytes=VMEM_LIMIT)


def _div(a, b):
    assert a % b == 0, (a, b)
    return a // b


def _nt(a, b):
    return lax.dot_general(a, b, (((1,), (1,)), ((), ())), preferred_element_type=F32)


def _mm(a, b):
    return jnp.dot(a.astype(BF16), b.astype(BF16), preferred_element_type=F32)


def _silu(x):
    return x * jax.nn.sigmoid(x)


def _head_sum(x, lane_is_head0):
    s0 = jnp.sum(jnp.where(lane_is_head0, x, 0.0), axis=-1, keepdims=True)
    s1 = jnp.sum(jnp.where(lane_is_head0, 0.0, x), axis=-1, keepdims=True)
    return jnp.where(lane_is_head0, s0, s1)


def _seq_edges(i, tm, n_lat_rows, seq_lat, seq_ctx):
    row = lax.broadcasted_iota(jnp.int32, (tm, 1), 0)
    grow = i * tm + row
    slen = jnp.where(grow < n_lat_rows, seq_lat, seq_ctx)
    pos = grow & (slen - 1)
    return row, pos == 0, pos == slen - 1


def _shift_rows(x, halo_prev, halo_next, row, first, last, tm):
    xp = pltpu.roll(x, 1, axis=0)
    xp = jnp.where(row == 0, halo_prev[SUBLANES - 1:SUBLANES], xp)
    xp = jnp.where(first, 0.0, xp)
    xn = pltpu.roll(x, tm - 1, axis=0)
    xn = jnp.where(row == tm - 1, halo_next[0:1], xn)
    xn = jnp.where(last, 0.0, xn)
    return xp, xn


def _shift_rows_tile(x, halo_prev, halo_next, i, tm, n_lat_rows, seq_lat, seq_ctx, interior):
    r0 = i * tm
    slen = jnp.where(r0 < n_lat_rows, seq_lat, seq_ctx)
    starts = (r0 & (slen - 1)) == 0
    ends = ((r0 + tm) & (slen - 1)) == 0
    row8 = lax.broadcasted_iota(jnp.int32, (SUBLANES, 1), 0)
    xp = pltpu.roll(x, 1, axis=0)
    top = jnp.where(row8 == 0, jnp.where(starts, 0.0, halo_prev[SUBLANES - 1:SUBLANES]), xp[:SUBLANES])
    xp = jnp.concatenate([top, xp[SUBLANES:]], axis=0)
    xn = pltpu.roll(x, tm - 1, axis=0)
    bot = jnp.where(row8 == SUBLANES - 1, jnp.where(ends, 0.0, halo_next[0:1]), xn[tm - SUBLANES:])
    xn = jnp.concatenate([xn[:tm - SUBLANES], bot], axis=0)
    if interior:
        _, first, last = _seq_edges(i, tm, n_lat_rows, seq_lat, seq_ctx)
        xp = jnp.where(first, 0.0, xp)
        xn = jnp.where(last, 0.0, xn)
    return xp, xn


def _mod_kernel(c_ref, w_ref, b_ref, o_ref):
    s = _silu(c_ref[...]).astype(BF16)
    o_ref[0] = jnp.dot(s, w_ref[0].astype(BF16), preferred_element_type=F32) + b_ref[0]


def _modulation(cvec, w_mod, b_mod):
    depth, d, n = w_mod.shape
    tn = 1024
    return pl.pallas_call(
        _mod_kernel,
        grid=(depth, _div(n, tn)),
        in_specs=[pl.BlockSpec((SUBLANES, d), lambda l, j: (0, 0)),
                  pl.BlockSpec((1, d, tn), lambda l, j: (l, 0, j)),
                  pl.BlockSpec((1, 1, tn), lambda l, j: (l, 0, j))],
        out_specs=pl.BlockSpec((1, SUBLANES, tn), lambda l, j: (l, 0, j)),
        out_shape=jax.ShapeDtypeStruct((depth, SUBLANES, n), F32),
        compiler_params=_cparams("arbitrary", "arbitrary"),
        name="modulation",
    )(cvec, w_mod, b_mod.reshape(depth, 1, n))


def _norm_kernel(*refs, modulate):
    if modulate:
        h_ref, g_ref, sh_ref, sc_ref, o_ref = refs
    else:
        h_ref, g_ref, o_ref = refs
    x = h_ref[...]
    y = x * lax.rsqrt(jnp.mean(x * x, axis=-1, keepdims=True) + NORM_EPS) * g_ref[...]
    if modulate:
        y = y * (1.0 + sc_ref[0]) + sh_ref[0]
    o_ref[...] = y.astype(o_ref.dtype)


def _norm(h, g, mod, k_shift, k_scale, seq, nseg, out_dtype, rows=None):
    m, d = h.shape if rows is None else (rows, h.shape[1])
    tm = 1024
    row_spec = pl.BlockSpec((tm, d), lambda i: (i, 0))
    in_specs = [row_spec, pl.BlockSpec((1, d), lambda i: (0, 0))]
    args = [h, g.reshape(1, d)]
    if mod is not None:
        seg = lambda i: jnp.minimum((i * tm) // seq, nseg)
        in_specs += [pl.BlockSpec((1, 1, d), lambda i: (seg(i) * 6 + k_shift, 0, 0)),
                     pl.BlockSpec((1, 1, d), lambda i: (seg(i) * 6 + k_scale, 0, 0))]
        args += [mod, mod]
    return pl.pallas_call(
        functools.partial(_norm_kernel, modulate=mod is not None),
        grid=(_div(m, tm),),
        in_specs=in_specs,
        out_specs=row_spec,
        out_shape=jax.ShapeDtypeStruct((m, d), out_dtype),
        compiler_params=_cparams("arbitrary"),
        name="rmsnorm_mod",
    )(*args)


def _norm_mod_rows(src_ref, dst_ref, dst0, nrows, gain, shift, step=128):
    for r0 in range(0, nrows, step):
        x = src_ref[r0:r0 + min(step, nrows)]
        y = x * lax.rsqrt(jnp.mean(x * x, axis=-1, keepdims=True) + NORM_EPS)
        dst_ref[dst0 + r0:dst0 + r0 + x.shape[0]] = (y * gain + shift).astype(dst_ref.dtype)


def _matmul_kernel(*refs, act, has_bias, has_res, has_norm):
    x_ref, w_ref = refs[0], refs[1]
    k = 2
    if has_norm:
        ng_ref, sh_ref, sc_ref = refs[k:k + 3]
        k += 3
        u_s = refs[-1]

        @pl.when(pl.program_id(1) == 0)
        def _():
            _norm_mod_rows(x_ref, u_s, 0, x_ref.shape[0], ng_ref[...] * (1.0 + sc_ref[0]), sh_ref[0])
        x_ref = u_s
    acc = jnp.dot(x_ref[:, :w_ref.shape[0]], w_ref[...], preferred_element_type=F32)
    if has_bias:
        acc = acc + refs[k][...]
        k += 1
    if act == "gelu":
        acc = jax.nn.gelu(acc, approximate=True)
    elif act == "tanh":
        acc = jnp.tanh(acc)
    elif act == "sigmoid":
        acc = jax.nn.sigmoid(acc)
    if has_res:
        acc = refs[k][...] + refs[k + 1][0] * acc
        k += 2
    o_ref = refs[k]
    o_ref[...] = acc.astype(o_ref.dtype)


def _matmul(x, w, *, bias=None, act=None, res=None, mod=None, k_gate=None, seq=None, nseg=None,
            out_dtype=F32, tm=1024, tn=512, rows=None, w_lead=(), norm=None):
    m = x.shape[0] if rows is None else rows
    kdim, n = w.shape[-2:]
    tn = min(tn, n)
    tm = min(tm, m)
    assert x.shape[1] >= kdim and (x.shape[1] == kdim or kdim % LANES == 0)
    in_specs = [pl.BlockSpec((tm, x.shape[1]), lambda i, j: (i, 0)),
                pl.BlockSpec((None,) * len(w_lead) + (kdim, tn), lambda i, j: (*w_lead, 0, j))]
    args = [x, w]
    scratch = []
    if norm is not None:
        norm_g, norm_mod, k_shift, k_scale = norm
        nseg_of = lambda i: jnp.minimum((i * tm) // seq, nseg)
        in_specs += [pl.BlockSpec((1, kdim), lambda i, j: (0, 0)),
                     pl.BlockSpec((1, 1, kdim), lambda i, j: (nseg_of(i) * 6 + k_shift, 0, 0)),
                     pl.BlockSpec((1, 1, kdim), lambda i, j: (nseg_of(i) * 6 + k_scale, 0, 0))]
        args += [norm_g.reshape(1, kdim), norm_mod, norm_mod]
        scratch = [pltpu.VMEM((tm, kdim), BF16)]
    if bias is not None:
        in_specs.append(pl.BlockSpec((1, tn), lambda i, j: (0, j)))
        args.append(bias.reshape(1, n))
    if res is not None:
        seg = lambda i: jnp.minimum((i * tm) // seq, nseg)
        in_specs += [pl.BlockSpec((tm, tn), lambda i, j: (i, j)),
                     pl.BlockSpec((1, 1, tn), lambda i, j: (seg(i) * 6 + k_gate, 0, j))]
        args += [res, mod]
    return pl.pallas_call(
        functools.partial(_matmul_kernel, act=act, has_bias=bias is not None, has_res=res is not None,
                          has_norm=norm is not None),
        grid=(_div(m, tm), _div(n, tn)),
        in_specs=in_specs,
        out_specs=pl.BlockSpec((tm, tn), lambda i, j: (i, j)),
        out_shape=jax.ShapeDtypeStruct((m, n), out_dtype),
        scratch_shapes=scratch,
        compiler_params=_cparams("arbitrary", "arbitrary"),
        name="matmul",
    )(*args)


def _ffn1_kernel(h_ref, hp_ref, hn_ref, ng_ref, sh_ref, sc_ref, wg_ref, wu_ref, cw_ref, cb_ref, o_ref, u_s,
                 *, tm, n_lat_rows, seq_lat, seq_ctx):
    i = pl.program_id(0)
    hr = hp_ref.shape[0]

    @pl.when(pl.program_id(1) == 0)
    def _():
        gain = ng_ref[...] * (1.0 + sc_ref[0])
        shift = sh_ref[0]
        _norm_mod_rows(hp_ref, u_s, 0, hr, gain, shift)
        _norm_mod_rows(h_ref, u_s, hr, tm, gain, shift)
        _norm_mod_rows(hn_ref, u_s, hr + tm, hr, gain, shift)

    x = u_s[hr:hr + tm]
    g_ext = jnp.dot(u_s[...], wg_ref[...], preferred_element_type=F32)
    g = g_ext[hr:hr + tm]
    gp = g_ext[hr - SUBLANES:hr]
    gn = g_ext[hr + tm:hr + tm + SUBLANES]
    g_prev, g_next = _shift_rows_tile(g, gp, gn, i, tm, n_lat_rows, seq_lat, seq_ctx,
                                      interior=min(seq_lat, seq_ctx) < tm)
    gte = g_prev * cw_ref[0:1] + g * cw_ref[1:2] + g_next * cw_ref[2:3] + cb_ref[...]
    up = jnp.dot(x, wu_ref[...], preferred_element_type=F32)
    o_ref[...] = (_silu(gte) * up).astype(o_ref.dtype)


def _halo_specs(tm, kdim, m):
    hr = BF16_ROWS
    nb = tm // hr
    prev = pl.BlockSpec((hr, kdim), lambda i, j: (jnp.maximum(i * nb - 1, 0), 0))
    nxt = pl.BlockSpec((hr, kdim), lambda i, j: (jnp.minimum((i + 1) * nb, m // hr - 1), 0))
    return prev, nxt


def _cast_pad_kernel(w_ref, o_ref):
    n = w_ref.shape[1]
    o_ref[:, :n] = w_ref[...].astype(o_ref.dtype)
    o_ref[:, n:] = jnp.zeros((o_ref.shape[0], o_ref.shape[1] - n), o_ref.dtype)


def _cast_pad_cols(w, n_pad):
    layers, kdim, n = w.shape
    tr = 256
    return pl.pallas_call(
        _cast_pad_kernel,
        grid=(layers, _div(kdim, tr)),
        in_specs=[pl.BlockSpec((None, tr, n), lambda l, i: (l, i, 0))],
        out_specs=pl.BlockSpec((None, tr, n_pad), lambda l, i: (l, i, 0)),
        out_shape=jax.ShapeDtypeStruct((layers, kdim, n_pad), BF16),
        compiler_params=_cparams("arbitrary", "arbitrary"),
        name="cast_pad",
    )(w)


def _ffn1(h, norm_g, mod, wg, wu, cw, cb, layer, *, rows, nseg, n_lat_rows, seq_lat, seq_ctx):
    _, kdim, f = wg.shape
    tm, tn = 1024, 512
    assert n_lat_rows % tm == 0 and all(s % tm == 0 or tm % s == 0 for s in (seq_lat, seq_ctx))
    prev, nxt = _halo_specs(tm, kdim, rows)
    seg = lambda i: jnp.minimum((i * tm) // seq_lat, nseg)
    return pl.pallas_call(
        functools.partial(_ffn1_kernel, tm=tm, n_lat_rows=n_lat_rows, seq_lat=seq_lat, seq_ctx=seq_ctx),
        grid=(_div(rows, tm), _div(f, tn)),
        in_specs=[pl.BlockSpec((tm, kdim), lambda i, j: (i, 0)), prev, nxt,
                  pl.BlockSpec((1, kdim), lambda i, j: (0, 0)),
                  pl.BlockSpec((1, 1, kdim), lambda i, j: (seg(i) * 6 + 3, 0, 0)),
                  pl.BlockSpec((1, 1, kdim), lambda i, j: (seg(i) * 6 + 4, 0, 0)),
                  pl.BlockSpec((None, kdim, tn), lambda i, j: (layer, 0, j)),
                  pl.BlockSpec((None, kdim, tn), lambda i, j: (layer, 0, j)),
                  pl.BlockSpec((3, tn), lambda i, j: (0, j)),
                  pl.BlockSpec((1, tn), lambda i, j: (0, j))],
        out_specs=pl.BlockSpec((tm, tn), lambda i, j: (i, j)),
        out_shape=jax.ShapeDtypeStruct((rows, f), BF16),
        scratch_shapes=[pltpu.VMEM((tm + 2 * BF16_ROWS, kdim), BF16)],
        compiler_params=_cparams("arbitrary", "arbitrary"),
        name="ffn_gate_up",
    )(h, h, h, norm_g.reshape(1, kdim), mod, mod, wg, wu, cw, cb)


def _pool_kernel(u_ref, w_ref, b_ref, s_ref, h_ref, gate_ref, o_ref, xp_ref, *, seq):
    grp = pl.program_id(1)
    pad = SUBLANES
    zeros = jnp.zeros((pad, u_ref.shape[1]), F32)
    xp_ref[0:pad] = zeros
    xp_ref[pad:pad + seq] = u_ref[...]
    xp_ref[pad + seq:2 * pad + seq] = zeros
    rt = min(seq, 256)
    for gi, win in enumerate(POOL_WINDOWS):
        @pl.when(grp == gi)
        def _(win=win):
            left, right = win // 2, win - 1 - win // 2
            for rc in range(seq // rt):
                base = pad + rc * rt
                t = rc * rt + lax.broadcasted_iota(jnp.int32, (rt, 1), 0)
                acc = xp_ref[base - left:base - left + rt]
                for j in range(1, win):
                    acc = acc + xp_ref[base - left + j:base - left + j + rt]
                cnt = (jnp.minimum(t + right, seq - 1) - jnp.maximum(t - left, 0) + 1).astype(F32)
                p = acc / cnt - xp_ref[base:base + rt]
                y = jnp.dot(p.astype(BF16), w_ref[0], preferred_element_type=F32) + b_ref[...]
                rows = slice(rc * rt, (rc + 1) * rt)
                o_ref[rows] = h_ref[rows] + gate_ref[0] * (y * s_ref[...])


def _pool_mix(u, w, b, scale, h, mod, *, seq, row0, nseq, seg0):
    d = u.shape[1]
    cg = d // len(POOL_WINDOWS)
    blk0 = row0 // seq
    tile = pl.BlockSpec((seq, cg), lambda s, g: (blk0 + s, g))
    vec = pl.BlockSpec((1, cg), lambda s, g: (0, g))
    return pl.pallas_call(
        functools.partial(_pool_kernel, seq=seq),
        grid=(nseq, len(POOL_WINDOWS)),
        in_specs=[tile, pl.BlockSpec((1, cg, cg), lambda s, g: (g, 0, 0)), vec, vec, tile,
                  pl.BlockSpec((1, 1, cg), lambda s, g: ((seg0(s)) * 6 + 2, 0, g))],
        out_specs=tile,
        out_shape=jax.ShapeDtypeStruct(h.shape, F32),
        scratch_shapes=[pltpu.VMEM((seq + 2 * SUBLANES, cg), F32)],
        input_output_aliases={4: 0},
        compiler_params=_cparams("arbitrary", "arbitrary"),
        name="pool_mix",
    )(u, w, b.reshape(1, d), scale.reshape(1, d), h, mod)


def _na_lat_kernel(q_ref, k_ref, v_ref, qc_ref, kc_ref, vc_ref, cos_ref, sin_ref, bias_ref, o_ref, oc_ref,
                   q_s, k_s, *, rows, kh):
    hd = NA_HEAD_DIM
    lane = lax.broadcasted_iota(jnp.int32, (1, LANES), 1)
    low = (lane % 32) < 16
    head0 = lane < hd
    cos, sin = cos_ref[...], sin_ref[...]

    def rope(x):
        swapped = jnp.where(low, pltpu.roll(x, LANES - 16, axis=1), pltpu.roll(x, 16, axis=1))
        return x * cos + swapped * sin

    q_s[...] = (rope(q_ref[...].astype(F32)) * (hd ** -0.5)).astype(BF16)
    k_s[...] = rope(k_ref[...].astype(F32)).astype(BF16)
    nloc = kh * GRID_W
    rows_per_step = 8
    dot = lambda a, b: jnp.dot(a, b, preferred_element_type=F32)
    zero = jnp.zeros((), BF16)

    qc = qc_ref[...] * (hd ** -0.5)
    n_ctx = qc.shape[0]
    qcm = jnp.concatenate([jnp.where(head0, qc, zero), jnp.where(head0, zero, qc)], axis=0)
    sc = _nt(qcm, kc_ref[...])
    pc = jnp.exp(sc - jnp.max(sc, axis=-1, keepdims=True))
    oc = dot(pc.astype(BF16), vc_ref[...]) / jnp.sum(pc, axis=-1, keepdims=True)
    oc_ref[...] = jnp.where(head0, oc[:n_ctx], oc[n_ctx:]).astype(oc_ref.dtype)

    def body(step, carry):
        n = range(rows_per_step)
        r = [step * rows_per_step + dr for dr in n]
        rs = [jnp.clip(r[i] - kh // 2, 0, rows - kh) for i in n]
        q0 = [pl.multiple_of(r[i] * GRID_W, GRID_W) for i in n]
        k0 = [pl.multiple_of(rs[i] * GRID_W, GRID_W) for i in n]
        qr = [q_s[pl.ds(q0[i], GRID_W), :] for i in n]
        qm = [jnp.concatenate([jnp.where(head0, qr[i], zero), jnp.where(head0, zero, qr[i])], axis=0) for i in n]
        s_loc = [_nt(qm[i], k_s[pl.ds(k0[i], nloc), :]) + bias_ref[r[i] - rs[i]] for i in n]
        s_ctx = [_nt(qm[i], kc_ref[...]) for i in n]
        m = [jnp.maximum(jnp.max(s_loc[i], axis=-1, keepdims=True), jnp.max(s_ctx[i], axis=-1, keepdims=True))
             for i in n]
        p_loc = [jnp.exp(s_loc[i] - m[i]) for i in n]
        p_ctx = [jnp.exp(s_ctx[i] - m[i]) for i in n]
        den = [jnp.sum(p_loc[i], axis=-1, keepdims=True) + jnp.sum(p_ctx[i], axis=-1, keepdims=True) for i in n]
        o = [dot(p_loc[i].astype(BF16), v_ref[pl.ds(k0[i], nloc), :]) + dot(p_ctx[i].astype(BF16), vc_ref[...])
             for i in n]
        o = [o[i] / den[i] for i in n]
        for i in n:
            o_ref[pl.ds(q0[i], GRID_W), :] = jnp.where(head0, o[i][:GRID_W], o[i][GRID_W:]).astype(o_ref.dtype)
        return carry

    lax.fori_loop(0, _div(rows, rows_per_step), body, 0)


def _rope_tables(t_len, hd):
    half = hd // 2
    pos = jnp.arange(t_len)
    inv_freq = ROPE_BASE ** (-jnp.arange(0, half, 2, dtype=F32) / half)
    d = jnp.arange(hd)
    p = jnp.where(d[None, :] < half, (pos // GRID_W)[:, None], (pos % GRID_W)[:, None]).astype(F32)
    ang = p * inv_freq[d % (half // 2)][None, :]
    cos = jnp.cos(ang)
    sin = jnp.where((d % half) < half // 2, -jnp.sin(ang), jnp.sin(ang))
    return jnp.tile(cos, (1, 2)), jnp.tile(sin, (1, 2))


def _na_bias_table(rpb, kh):
    col = jnp.arange(GRID_W)
    col_start = jnp.clip(col - NA_KW // 2, 0, GRID_W - NA_KW)
    col_ok = (col[None, :] >= col_start[:, None]) & (col[None, :] < col_start[:, None] + NA_KW)
    dc = jnp.clip(col[None, :] - col[:, None], -(NA_KW - 1), NA_KW - 1) + NA_KW - 1
    rpb_cols = rpb[:, :, dc]
    dr = jnp.arange(kh)[None, :] - jnp.arange(kh)[:, None] + NA_KH - 1
    tab = rpb_cols[:, dr]
    tab = jnp.where(col_ok[None, None, None], tab, -1e30)
    heads = rpb.shape[0]
    tab = tab.transpose(0, 1, 3, 2, 4).reshape(heads // 2, 2, kh, GRID_W, kh * GRID_W)
    return tab.transpose(0, 2, 1, 3, 4).reshape(heads // 2, kh, 2 * GRID_W, kh * GRID_W)


def _neighbourhood_attention(qkv, rpb, *, bsz, t_len, l_len):
    d = qkv.shape[1] // 3
    hd = NA_HEAD_DIM
    heads = d // hd
    npair = d // LANES
    rows = t_len // GRID_W
    kh = min(NA_KH, rows)
    n_lat = bsz * t_len
    n_all = qkv.shape[0]
    ctx0 = _div(n_lat, l_len)
    cos, sin = _rope_tables(t_len, hd)
    bias = _na_bias_table(rpb, kh)
    lat_blk = lambda off: pl.BlockSpec((t_len, LANES), lambda b, p: (b, off + p))
    ctx_blk = lambda off: pl.BlockSpec((l_len, LANES), lambda b, p: (ctx0 + b, off + p))
    tab_blk = pl.BlockSpec((t_len, LANES), lambda b, p: (0, 0))
    o_lat, o_ctx = pl.pallas_call(
        functools.partial(_na_lat_kernel, rows=rows, kh=kh),
        grid=(bsz, npair),
        in_specs=[lat_blk(0), lat_blk(npair), lat_blk(2 * npair), ctx_blk(0), ctx_blk(npair), ctx_blk(2 * npair),
                  tab_blk, tab_blk,
                  pl.BlockSpec((None, kh, 2 * GRID_W, kh * GRID_W), lambda b, p: (p, 0, 0, 0))],
        out_specs=[lat_blk(0), pl.BlockSpec((l_len, LANES), lambda b, p: (b, p))],
        out_shape=[jax.ShapeDtypeStruct((n_lat, d), BF16), jax.ShapeDtypeStruct((n_all - n_lat, d), BF16)],
        scratch_shapes=[pltpu.VMEM((t_len, LANES), BF16), pltpu.VMEM((t_len, LANES), BF16)],
        compiler_params=_cparams("arbitrary", "arbitrary"),
        name="na_attention",
    )(qkv, qkv, qkv, qkv, qkv, qkv, cos, sin, bias)
    return jnp.concatenate([o_lat, o_ctx], axis=0)


def _sg_kernel(z_u_ref, z_v_ref, g_ref, ws_ref, bs_ref, o_ref, *, nchunk):
    zv = z_v_ref[...].astype(F32)
    zv = zv * lax.rsqrt(jnp.mean(zv * zv, axis=-1, keepdims=True) + NORM_EPS) * g_ref[...]
    zvb = zv.astype(BF16)
    for c in range(nchunk):
        rs = slice(c * SG_CHUNK, (c + 1) * SG_CHUNK)
        for g in range(SG_GROUPS):
            ls = slice(g * LANES, (g + 1) * LANES)
            mixed = jnp.dot(ws_ref[g], zvb[rs, ls], preferred_element_type=F32) + bs_ref[:, ls]
            o_ref[rs, ls] = (z_u_ref[rs, ls].astype(F32) * mixed).astype(o_ref.dtype)


def _spatial_gate(z, norm_g, w_s, b_s):
    m, two_w = z.shape
    width = two_w // 2
    nchunk = 2
    tm = nchunk * SG_CHUNK
    bs_full = jnp.repeat(b_s.T, width // SG_GROUPS, axis=1)
    return pl.pallas_call(
        functools.partial(_sg_kernel, nchunk=nchunk),
        grid=(_div(m, tm),),
        in_specs=[pl.BlockSpec((tm, width), lambda i: (i, 0)),
                  pl.BlockSpec((tm, width), lambda i: (i, 1)),
                  pl.BlockSpec((1, width), lambda i: (0, 0)),
                  pl.BlockSpec((SG_GROUPS, SG_CHUNK, SG_CHUNK), lambda i: (0, 0, 0)),
                  pl.BlockSpec((SG_CHUNK, width), lambda i: (0, 0))],
        out_specs=pl.BlockSpec((tm, width), lambda i: (i, 0)),
        out_shape=jax.ShapeDtypeStruct((m, width), BF16),
        compiler_params=_cparams("arbitrary"),
        name="spatial_gate",
    )(z, z, norm_g.reshape(1, width), w_s.astype(BF16), bs_full)


def _shift_kernel(u_ref, up_ref, un_ref, mu_ref, *o_refs, tm, n_lat_rows, seq_lat, seq_ctx):
    i = pl.program_id(0)
    u = u_ref[...]
    row, first, last = _seq_edges(i, tm, n_lat_rows, seq_lat, seq_ctx)
    u_prev, u_next = _shift_rows(u, up_ref[...], un_ref[...], row, first, last, tm)
    prev = u_prev - u
    nxt = u_next - u
    nmix = len(o_refs)
    for n in range(nmix):
        o_refs[n][...] = (u + prev * mu_ref[n:n + 1] + nxt * mu_ref[nmix + n:nmix + n + 1]).astype(BF16)


def _token_shift(u, mu, *, n_lat_rows, seq_lat, seq_ctx):
    m, d = u.shape
    nmix = mu.shape[1]
    tm, tn = 512, 512
    nb = tm // SUBLANES
    tile = pl.BlockSpec((tm, tn), lambda i, j: (i, j))
    prev = pl.BlockSpec((SUBLANES, tn), lambda i, j: (jnp.maximum(i * nb - 1, 0), j))
    nxt = pl.BlockSpec((SUBLANES, tn), lambda i, j: (jnp.minimum((i + 1) * nb, m // SUBLANES - 1), j))
    return pl.pallas_call(
        functools.partial(_shift_kernel, tm=tm, n_lat_rows=n_lat_rows, seq_lat=seq_lat, seq_ctx=seq_ctx),
        grid=(_div(m, tm), _div(d, tn)),
        in_specs=[tile, prev, nxt, pl.BlockSpec((2 * nmix, tn), lambda i, j: (0, j))],
        out_specs=[tile] * nmix,
        out_shape=[jax.ShapeDtypeStruct((m, d), BF16)] * nmix,
        compiler_params=_cparams("arbitrary", "arbitrary"),
        name="token_shift",
    )(u, u, u, mu.reshape(2 * nmix, d))


def _split2(x):
    hi = x.astype(BF16)
    lo = (x - hi.astype(F32)).astype(BF16)
    return hi, lo


def _chunk_chains(dirs, v_ref, *, nheads):
    c = SCAN_CHUNK
    hd = RW_HEAD_DIM
    ri = lax.broadcasted_iota(jnp.int32, (c, c), 0)
    ci = lax.broadcasted_iota(jnp.int32, (c, c), 1)
    strict_d = [(ci > ri) if d[-1] else (ci < ri) for d in dirs]
    incl_d = [(ci >= ri) if d[-1] else (ci <= ri) for d in dirs]
    diag_blk = (ri // SOLVE_BLOCK) == (ci // SOLVE_BLOCK)
    eye = jnp.where(ri == ci, 1.0, 0.0)
    chains = [(di, half, hh) for di in range(len(dirs)) for half in range(LANES // c) for hh in range(nheads)]
    src = [dirs[di] for di, _, _ in chains]
    strict = [strict_d[di] for di, _, _ in chains]
    incl = [incl_d[di] for di, _, _ in chains]
    rows = [slice(half * c, (half + 1) * c) for _, half, _ in chains]
    ls = [slice(hh * hd, (hh + 1) * hd) for _, _, hh in chains]
    pair = [hh // 2 for _, _, hh in chains]
    sub = [slice((hh % 2) * hd, (hh % 2 + 1) * hd) for _, _, hh in chains]
    n = range(len(chains))
    dot = lambda a, b: jnp.dot(a, b, preferred_element_type=F32)
    kt = [src[i][0][rows[i], ls[i]] for i in n]
    rt = [src[i][1][rows[i], ls[i]] for i in n]
    vv = [v_ref[rows[i], ls[i]] for i in n]
    ktrt = [jnp.concatenate([kt[i], rt[i]], axis=0) for i in n]
    pb = [_nt(ktrt[i], src[i][2][rows[i], ls[i]]) for i in n]
    pk = [_nt(ktrt[i], src[i][3][rows[i], ls[i]]) for i in n]
    a1 = [jnp.where(strict[i], pb[i][:c], 0.0) for i in n]
    a3 = [jnp.where(incl[i], pb[i][c:], 0.0).astype(BF16) for i in n]
    ad = [jnp.where(diag_blk, a1[i], 0.0) for i in n]
    ao = [(a1[i] - ad[i]).astype(BF16) for i in n]
    tinv = [eye - ad[i] for i in n]
    pw = [_mm(ad[i], ad[i]) for i in n]
    lhs_v = [jnp.concatenate([jnp.where(strict[i], pk[i][:c], 0.0).astype(BF16),
                              jnp.where(incl[i], pk[i][c:], 0.0).astype(BF16),
                              src[i][5][pair[i], sub[i], rows[i]]], axis=0) for i in n]
    pv = [dot(lhs_v[i], vv[i]) for i in n]
    tinv = [tinv[i] + _mm(tinv[i], pw[i]) for i in n]
    for _ in range(int(math.log2(SOLVE_BLOCK)) - 2):
        pw = [_mm(pw[i], pw[i]) for i in n]
        tinv = [tinv[i] + _mm(tinv[i], pw[i]) for i in n]
    rhs = [jnp.concatenate([kt[i].astype(F32), pv[i][:c]], axis=1).astype(BF16) for i in n]
    tinv = [t.astype(BF16) for t in tinv]
    nmat = [dot(tinv[i], ao[i]).astype(BF16) for i in n]
    x1 = [dot(tinv[i], rhs[i]) for i in n]
    x = x1
    for _ in range(c // SOLVE_BLOCK - 1):
        x = [x1[i] - dot(nmat[i], x[i].astype(BF16)) for i in n]
    lhs_x = [jnp.concatenate([src[i][4][pair[i], sub[i], rows[i]], a3[i]], axis=0) for i in n]
    px = [dot(lhs_x[i], x[i].astype(BF16)) for i in n]
    for i, (_, half, hh) in enumerate(chains):
        m_o, c_o, rp_o, y0_o = src[i][6]
        m_o[hh, 0, :, rows[i]] = (-px[i][:c, :hd]).astype(m_o.dtype)
        c_o[hh, 0, :, rows[i]] = pv[i][2 * c:] - px[i][:c, hd:]
        rp_o[rows[i], ls[i]] = (rt[i].astype(F32) - px[i][c:, :hd]).astype(rp_o.dtype)
        y0_o[rows[i], ls[i]] = pv[i][c:2 * c] - px[i][c:, hd:]


def _rw_chunk_kernel(k_ref, r_ref, v_ref, hw_ref, ha_ref, w2_ref, a2_ref, w0_ref, a0_ref, kk_ref, ka_ref,
                     rk_ref, tri_ref, bc_o, mf_o, cf_o, rpf_o, y0f_o, gf_o, mr_o, cr_o, rpr_o, y0r_o, gr_o,
                     kt_s, rt_s, bh_s, kh_s, v_s, bt_s, kbt_s, *, lora, nheads):
    npair = nheads // 2
    lane = lax.broadcasted_iota(jnp.int32, (1, LANES), 1)
    head0 = lane < RW_HEAD_DIM
    pairs = [slice(p * LANES, (p + 1) * LANES) for p in range(npair)]
    head_sum = lambda x: jnp.concatenate([_head_sum(x[:, ps], head0) for ps in pairs], axis=1)
    k = k_ref[...]
    r = r_ref[...]
    v_s[...] = v_ref[...].astype(BF16)
    kkv = k * kk_ref[...]
    kk = kkv * lax.rsqrt(jnp.maximum(head_sum(kkv * kkv), 1e-12))
    ka = ka_ref[...]
    rk = rk_ref[...]
    hw = hw_ref[...]
    ha = ha_ref[...]
    log_decay, beta, key = [], [], []
    for d in range(2):
        ls = slice(d * lora, (d + 1) * lora)
        pre = w0_ref[d] + jnp.dot(hw[:, ls], w2_ref[d], preferred_element_type=F32)
        z = -pre
        softplus = jnp.maximum(z, 0.0) + jnp.log(1.0 + jnp.exp(-jnp.abs(z)))
        log_decay.append(-jnp.exp(-softplus - 0.5))
        a = jax.nn.sigmoid(a0_ref[d] + jnp.dot(ha[:, ls], a2_ref[d], preferred_element_type=F32))
        beta.append(kk * a)
        key.append(k * (1.0 + (a - 1.0) * ka))
    bc_o[...] = head_sum(r * key[0] * rk) + head_sum(r * key[1] * rk)
    g_o = (gf_o, gr_o)
    for d in range(2):
        ld = log_decay[d]
        pieces = _split2(ld)
        csum = [sum(jnp.dot(tri_ref[e], p, preferred_element_type=F32) for p in pieces) for e in range(2)]
        cum = csum[d]
        total = csum[0] + csum[1] - ld
        kt_s[d] = (kk * jnp.exp(cum - ld)).astype(BF16)
        rt_s[d] = (r * jnp.exp(cum)).astype(BF16)
        inv = jnp.exp(-cum)
        bh_s[d] = (beta[d] * inv).astype(BF16)
        kh_s[d] = (key[d] * inv).astype(BF16)
        tail = jnp.exp(total - cum)
        bbar = beta[d] * tail
        kbar = key[d] * tail
        gdec = jnp.exp(total)
        for p, ps in enumerate(pairs):
            bt_s[d, p] = jnp.transpose(bbar[:, ps]).astype(BF16)
            kbt_s[d, p] = jnp.transpose(kbar[:, ps]).astype(BF16)
            g_o[d][p, 0] = jnp.transpose(gdec[:, ps])
    outs = ((mf_o, cf_o, rpf_o, y0f_o), (mr_o, cr_o, rpr_o, y0r_o))
    _chunk_chains([(kt_s.at[d], rt_s.at[d], bh_s.at[d], kh_s.at[d], bt_s.at[d], kbt_s.at[d], outs[d], d == 1)
                   for d in range(2)], v_s, nheads=nheads)


def _rw_chunks(k, r, v, hw, ha, w2, a2, w0, a0, k_k, k_a, r_k, *, bsz, t_len, l_len, lora):
    m, d = k.shape
    hd = RW_HEAD_DIM
    nheads = 8
    width = nheads * hd
    npair = nheads // 2
    tt = t_len + l_len
    ng = _div(tt, LANES)
    ctx_groups = _div(l_len, LANES)
    ctx0 = _div(bsz * t_len, LANES)

    def in_blk(b, g):
        return jnp.where(g < ctx_groups, ctx0 + b * ctx_groups + g,
                         b * (t_len // LANES) + g - ctx_groups)

    ri = jnp.arange(LANES)[:, None]
    ci = jnp.arange(LANES)[None, :]
    same = (ri // SCAN_CHUNK) == (ci // SCAN_CHUNK)
    tri = jnp.stack([same & (ci <= ri), same & (ci >= ri)]).astype(BF16)
    tile = pl.BlockSpec((LANES, width), lambda b, p, g: (in_blk(b, g), p))
    lora_blk = pl.BlockSpec((LANES, 2 * lora), lambda b, p, g: (in_blk(b, g), 0))
    wl_blk = pl.BlockSpec((2, lora, width), lambda b, p, g: (0, 0, p))
    v2_blk = pl.BlockSpec((2, 1, width), lambda b, p, g: (0, 0, p))
    v1_blk = pl.BlockSpec((1, width), lambda b, p, g: (0, p))
    tri_blk = pl.BlockSpec((2, LANES, LANES), lambda b, p, g: (0, 0, 0))
    row_o = pl.BlockSpec((None, LANES, width), lambda b, p, g: (b, g, p))
    sq_o = pl.BlockSpec((None, nheads, 1, hd, LANES), lambda b, p, g: (b, p, g, 0, 0))
    tr_o = pl.BlockSpec((None, npair, 1, LANES, LANES), lambda b, p, g: (b, p, g, 0, 0))
    row_shape = lambda dt: jax.ShapeDtypeStruct((bsz, tt, d), dt)
    sq_shape = lambda dt: jax.ShapeDtypeStruct((bsz, d // hd, ng, hd, LANES), dt)
    tr_shape = jax.ShapeDtypeStruct((bsz, d // LANES, ng, LANES, LANES), F32)
    dir_specs = [sq_o, sq_o, row_o, row_o, tr_o]
    dir_shapes = [sq_shape(BF16), sq_shape(F32), row_shape(BF16), row_shape(F32), tr_shape]
    operand = lambda: pltpu.VMEM((2, LANES, width), BF16)
    transposed = lambda: pltpu.VMEM((2, npair, LANES, LANES), BF16)
    out = pl.pallas_call(
        functools.partial(_rw_chunk_kernel, lora=lora, nheads=nheads),
        grid=(bsz, _div(d, width), ng),
        in_specs=[tile, tile, tile, lora_blk, lora_blk, wl_blk, wl_blk, v2_blk, v2_blk, v1_blk, v1_blk, v1_blk,
                  tri_blk],
        out_specs=[row_o] + dir_specs + dir_specs,
        out_shape=[row_shape(F32)] + dir_shapes + dir_shapes,
        scratch_shapes=[operand(), operand(), operand(), operand(), pltpu.VMEM((LANES, width), BF16),
                        transposed(), transposed()],
        compiler_params=_cparams("arbitrary", "arbitrary", "arbitrary"),
        name="rwkv_chunks",
    )(k, r, v, hw, ha, w2, a2, w0.reshape(2, 1, d), a0.reshape(2, 1, d), k_k.reshape(1, d),
      k_a.reshape(1, d), r_k.reshape(1, d), tri)
    return out[0], out[1:6], out[6:11]


def _state_pass_kernel(mf_ref, cf_ref, gf_ref, mr_ref, cr_ref, gr_ref, sf_o, sr_o, *, ctx_groups, n_groups,
                       nheads):
    c = SCAN_CHUNK
    hd = RW_HEAD_DIM
    heads = range(nheads)
    sf_o[...] = jnp.zeros(sf_o.shape, sf_o.dtype)
    sr_o[...] = jnp.zeros(sr_o.shape, sr_o.dtype)
    dirs = ((mf_ref, cf_ref, gf_ref, sf_o, False), (mr_ref, cr_ref, gr_ref, sr_o, True))

    def group(i, carry):
        out = []
        g_of = [i, jnp.where(i < ctx_groups, ctx_groups - 1 - i, n_groups - 1 - (i - ctx_groups))]
        states = [list(carry[0]), list(carry[1])]
        for step in range(LANES // c):
            s0b, ms = [[], []], [[], []]
            for di, (m_ref, c_ref, g_ref, s_o, rev) in enumerate(dirs):
                half = (LANES // c - 1 - step) if rev else step
                ts = slice(half * c, (half + 1) * c)
                for hh in heads:
                    sb = states[di][hh].astype(BF16)
                    s0b[di].append(sb)
                    blk = slice((hh % 2) * hd, (hh % 2 + 1) * hd)
                    s_o[hh // 2, g_of[di] * (LANES // c) + half, blk, blk] = sb
            for di, (m_ref, c_ref, g_ref, s_o, rev) in enumerate(dirs):
                half = (LANES // c - 1 - step) if rev else step
                ts = slice(half * c, (half + 1) * c)
                ms[di] = [jnp.dot(m_ref[hh, g_of[di], :, ts], s0b[di][hh], preferred_element_type=F32)
                          for hh in heads]
            for di, (m_ref, c_ref, g_ref, s_o, rev) in enumerate(dirs):
                half = (LANES // c - 1 - step) if rev else step
                ts = slice(half * c, (half + 1) * c)
                for hh in heads:
                    blk = slice((hh % 2) * hd, (hh % 2 + 1) * hd)
                    gc = g_ref[hh // 2, g_of[di], blk, half * c:half * c + 1]
                    states[di][hh] = gc * states[di][hh] + ms[di][hh] + c_ref[hh, g_of[di], :, ts]
        return tuple(states[0]), tuple(states[1])

    zero = tuple(jnp.zeros((hd, hd), F32) for _ in heads)
    lax.fori_loop(0, n_groups, group, (zero, zero))


def _state_pass(m_f, c_f, g_f, m_r, c_r, g_r, *, l_len):
    bsz, heads, ng, hd, _ = m_f.shape
    nheads = 4
    npair = nheads // 2
    nc = ng * (LANES // SCAN_CHUNK)
    sq = pl.BlockSpec((None, nheads, ng, hd, LANES), lambda b, p: (b, p, 0, 0, 0))
    tr = pl.BlockSpec((None, npair, ng, LANES, LANES), lambda b, p: (b, p, 0, 0, 0))
    st = pl.BlockSpec((None, npair, nc, LANES, LANES), lambda b, p: (b, p, 0, 0, 0))
    st_shape = jax.ShapeDtypeStruct((bsz, heads // 2, nc, LANES, LANES), BF16)
    return pl.pallas_call(
        functools.partial(_state_pass_kernel, ctx_groups=l_len // LANES, n_groups=ng, nheads=nheads),
        grid=(bsz, _div(heads, nheads)),
        in_specs=[sq, sq, tr, sq, sq, tr],
        out_specs=[st, st],
        out_shape=[st_shape, st_shape],
        compiler_params=_cparams("arbitrary", "arbitrary"),
        name="rwkv_state",
    )(m_f, c_f, g_f, m_r, c_r, g_r)


def _readout_kernel(rpf_ref, y0f_ref, sf_ref, rpr_ref, y0r_ref, sr_ref, bc_ref, v_ref, g_ref, lng_ref,
                    lnb_ref, o_ref):
    c = SCAN_CHUNK
    lane = lax.broadcasted_iota(jnp.int32, (1, LANES), 1)
    head0 = lane < RW_HEAD_DIM
    inv_n = 1.0 / RW_HEAD_DIM
    tiles = [(cc, pp) for cc in range(o_ref.shape[0] // c) for pp in range(o_ref.shape[1] // LANES)]
    rows = [slice(cc * c, (cc + 1) * c) for cc, _ in tiles]
    ls = [slice(pp * LANES, (pp + 1) * LANES) for _, pp in tiles]
    n = range(len(tiles))
    dot = lambda a, b: jnp.dot(a, b, preferred_element_type=F32)
    yf = [dot(rpf_ref[rows[i], ls[i]], sf_ref[tiles[i][1], tiles[i][0]]) for i in n]
    yr = [dot(rpr_ref[rows[i], ls[i]], sr_ref[tiles[i][1], tiles[i][0]]) for i in n]
    for i in n:
        y = (yf[i] + y0f_ref[rows[i], ls[i]]) + (yr[i] + y0r_ref[rows[i], ls[i]])
        mean = _head_sum(y, head0) * inv_n
        yc = y - mean
        var = _head_sum(yc * yc, head0) * inv_n
        yn = yc * lax.rsqrt(var + RW_GN_EPS) * lng_ref[:, ls[i]] + lnb_ref[:, ls[i]]
        o_ref[rows[i], ls[i]] = ((yn + bc_ref[rows[i], ls[i]] * v_ref[rows[i], ls[i]])
                                 * g_ref[rows[i], ls[i]]).astype(o_ref.dtype)


def _rw_readout(rp_f, y0_f, s_f, rp_r, y0_r, s_r, bc, v, g, ln_g, ln_b, *, bsz, t_len, l_len):
    d = v.shape[1]
    tm, tn = 256, 512
    per_b = t_len // tm
    off = l_len // tm
    blk = lambda i: (i // per_b, off + i % per_b)
    scan_blk = pl.BlockSpec((None, tm, tn), lambda i, p: (*blk(i), p))
    st_blk = pl.BlockSpec((None, tn // LANES, tm // SCAN_CHUNK, LANES, LANES),
                          lambda i, p: (blk(i)[0], p, blk(i)[1], 0, 0))
    tile = pl.BlockSpec((tm, tn), lambda i, p: (i, p))
    vec = pl.BlockSpec((1, tn), lambda i, p: (0, p))
    return pl.pallas_call(
        _readout_kernel,
        grid=(_div(bsz * t_len, tm), _div(d, tn)),
        in_specs=[scan_blk, scan_blk, st_blk, scan_blk, scan_blk, st_blk, scan_blk, tile, tile, vec, vec],
        out_specs=tile,
        out_shape=jax.ShapeDtypeStruct((bsz * t_len, d), BF16),
        compiler_params=_cparams("arbitrary", "arbitrary"),
        name="rwkv_readout",
    )(rp_f, y0_f, s_f, rp_r, y0_r, s_r, bc, v, g, ln_g.reshape(1, d), ln_b.reshape(1, d))


def _pad_cols(w, n):
    return jnp.pad(w, ((0, 0), (0, n - w.shape[1])))


def _pad_rows(w, n):
    return jnp.pad(w, ((0, n - w.shape[0]), (0, 0)))


def kernel(x, c, ctx, c_ctx, norm1_g, norm2_g, w_mod, b_mod, ffn_w_gate, ffn_w_up, ffn_conv_w, ffn_conv_b, ffn_w_down, final_norm_g, pool_w, pool_b, pool_scale, na_w_qkv, na_rpb, na_w_o, sg_w_in, sg_b_in, sg_norm_g, sg_w_s, sg_b_s, sg_w_o, rw_mu, rw_w_rkv, rw_w0, rw_w1, rw_w2, rw_a0, rw_a1, rw_a2, rw_g1, rw_g2, rw_k_k, rw_k_a, rw_r_k, rw_ln_g, rw_ln_b, rw_w_o):
    bsz, t_len, d = x.shape
    l_len = ctx.shape[1]
    depth = norm1_g.shape[0]
    n_mixers = 4
    n_lat = bsz * t_len
    n_all = n_lat + bsz * l_len
    f = ffn_w_gate.shape[2]
    f_pad = -(-f // 512) * 512
    wg_all = _cast_pad_cols(ffn_w_gate, f_pad)
    wu_all = _cast_pad_cols(ffn_w_up, f_pad)
    wd_all = ffn_w_down.astype(BF16)

    cvec = jnp.concatenate([c, c_ctx[None], jnp.zeros((SUBLANES - bsz - 1, d), F32)], axis=0)
    mods = _modulation(cvec, w_mod, b_mod)
    h = jnp.concatenate([x.reshape(n_lat, d), ctx.reshape(bsz * l_len, d)], axis=0)
    seg = dict(seq=t_len, nseg=bsz)
    edges = dict(n_lat_rows=n_lat, seq_lat=t_len, seq_ctx=l_len)

    for i in range(depth):
        m_kind, j = i % n_mixers, i // n_mixers
        last = i == depth - 1
        mod = mods[i].reshape(SUBLANES * 6, 1, d)
        rows = n_lat if last else n_all
        if m_kind == 0:
            u = _norm(h, norm1_g[i], mod, 0, 1, t_len, bsz, F32, rows=rows)
            pw = pool_w[j].astype(BF16)
            h = _pool_mix(u, pw, pool_b[j], pool_scale[j], h, mod, seq=t_len, row0=0, nseq=bsz,
                          seg0=lambda s: s)
            if not last:
                h = _pool_mix(u, pw, pool_b[j], pool_scale[j], h, mod, seq=l_len, row0=n_lat, nseq=bsz,
                              seg0=lambda s: bsz)
        elif m_kind == 1:
            qkv = _matmul(h, na_w_qkv[j].astype(BF16), out_dtype=BF16, norm=(norm1_g[i], mod, 0, 1), **seg)
            o = _neighbourhood_attention(qkv, na_rpb[j], bsz=bsz, t_len=t_len, l_len=l_len)
            h = _matmul(o, na_w_o[j].astype(BF16), res=h, mod=mod, k_gate=2, rows=rows, **seg)
        elif m_kind == 2:
            z = _matmul(h, sg_w_in[j].astype(BF16), bias=sg_b_in[j], act="gelu", out_dtype=BF16, rows=rows,
                        norm=(norm1_g[i], mod, 0, 1), **seg)
            gated = _spatial_gate(z, sg_norm_g[j], sg_w_s[j], sg_b_s[j])
            h = _matmul(gated, sg_w_o[j].astype(BF16), res=h, mod=mod, k_gate=2, rows=rows, **seg)
        else:
            lora = LANES
            u = _norm(h, norm1_g[i], mod, 0, 1, t_len, bsz, F32)
            xr, xw, xk, xv, xa, xg = _token_shift(u, rw_mu[j], **edges)
            w_rkv = rw_w_rkv.astype(BF16)
            r = _matmul(xr, w_rkv, w_lead=(j, 0))
            k = _matmul(xk, w_rkv, w_lead=(j, 1))
            v = _matmul(xv, w_rkv, w_lead=(j, 2))
            w1 = jnp.concatenate([_pad_cols(rw_w1[j, e], lora) for e in range(2)], axis=1).astype(BF16)
            a1 = jnp.concatenate([_pad_cols(rw_a1[j, e], lora) for e in range(2)], axis=1).astype(BF16)
            w2 = jnp.stack([_pad_rows(rw_w2[j, e], lora) for e in range(2)]).astype(BF16)
            a2 = jnp.stack([_pad_rows(rw_a2[j, e], lora) for e in range(2)]).astype(BF16)
            hw = _matmul(xw, w1, act="tanh", out_dtype=BF16)
            ha = _matmul(xa, a1, out_dtype=BF16)
            hg = _matmul(xg, rw_g1[j].astype(BF16), act="sigmoid", out_dtype=BF16, rows=n_lat)
            g = _matmul(hg, rw_g2[j].astype(BF16))
            bc, (m_f, c_f, rp_f, y0_f, g_f), (m_r, c_r, rp_r, y0_r, g_r) = _rw_chunks(
                k, r, v, hw, ha, w2, a2, rw_w0[j], rw_a0[j], rw_k_k[j], rw_k_a[j], rw_r_k[j],
                bsz=bsz, t_len=t_len, l_len=l_len, lora=lora)
            s_f, s_r = _state_pass(m_f, c_f, g_f, m_r, c_r, g_r, l_len=l_len)
            o = _rw_readout(rp_f, y0_f, s_f, rp_r, y0_r, s_r, bc, v, g, rw_ln_g[j], rw_ln_b[j],
                            bsz=bsz, t_len=t_len, l_len=l_len)
            h = _matmul(o, rw_w_o[j].astype(BF16), res=h, mod=mod, k_gate=2, rows=n_lat, **seg)
        cw = _pad_cols(ffn_conv_w[i], f_pad)
        cb = _pad_cols(ffn_conv_b[i][None], f_pad)
        mid = _ffn1(h, norm2_g[i], mod, wg_all, wu_all, cw, cb, i, rows=rows, nseg=bsz, **edges)
        h = _matmul(mid, wd_all, w_lead=(i,), res=h, mod=mod, k_gate=5, rows=rows, tm=1024, **seg)

    out = _norm(h, final_norm_g, None, 0, 0, t_len, bsz, F32, rows=n_lat)
    return out.reshape(bsz, t_len, d)
```

```python
import functools
import math

import jax
import jax.numpy as jnp
from jax import lax
from jax.experimental import pallas as pl
from jax.experimental.pallas import tpu as pltpu

F32 = jnp.float32
BF16 = jnp.bfloat16

GRID_W = 64
NORM_EPS = 1e-6
POOL_WINDOWS = (2, 4, 8, 16)
NA_HEAD_DIM = 64
NA_KH = 8
NA_KW = 16
ROPE_BASE = 10000.0
SG_CHUNK = 128
SG_GROUPS = 16
RW_HEAD_DIM = 64
RW_GN_EPS = 64e-5

LANES = 128
SUBLANES = 8
BF16_ROWS = 16
VMEM_LIMIT = 56 * 1024 * 1024

SCAN_CHUNK = 64
SOLVE_BLOCK = 16


def _cparams(*sem):
    return pltpu.CompilerParams(dimension_semantics=sem, vmem_limit_bytes=VMEM_LIMIT)


def _div(a, b):
    assert a % b == 0, (a, b)
    return a // b


def _nt(a, b):
    return lax.dot_general(a, b, (((1,), (1,)), ((), ())), preferred_element_type=F32)


def _mm(a, b):
    return jnp.dot(a.astype(BF16), b.astype(BF16), preferred_element_type=F32)


def _silu(x):
    return x * jax.nn.sigmoid(x)


def _head_sum(x, lane_is_head0):
    s0 = jnp.sum(jnp.where(lane_is_head0, x, 0.0), axis=-1, keepdims=True)
    s1 = jnp.sum(jnp.where(lane_is_head0, 0.0, x), axis=-1, keepdims=True)
    return jnp.where(lane_is_head0, s0, s1)


def _seq_edges(i, tm, n_lat_rows, seq_lat, seq_ctx):
    row = lax.broadcasted_iota(jnp.int32, (tm, 1), 0)
    grow = i * tm + row
    slen = jnp.where(grow < n_lat_rows, seq_lat, seq_ctx)
    pos = grow & (slen - 1)
    return row, pos == 0, pos == slen - 1


def _shift_rows(x, halo_prev, halo_next, row, first, last, tm):
    xp = pltpu.roll(x, 1, axis=0)
    xp = jnp.where(row == 0, halo_prev[SUBLANES - 1:SUBLANES], xp)
    xp = jnp.where(first, 0.0, xp)
    xn = pltpu.roll(x, tm - 1, axis=0)
    xn = jnp.where(row == tm - 1, halo_next[0:1], xn)
    xn = jnp.where(last, 0.0, xn)
    return xp, xn


def _shift_rows_tile(x, halo_prev, halo_next, i, tm, n_lat_rows, seq_lat, seq_ctx, interior):
    r0 = i * tm
    slen = jnp.where(r0 < n_lat_rows, seq_lat, seq_ctx)
    starts = (r0 & (slen - 1)) == 0
    ends = ((r0 + tm) & (slen - 1)) == 0
    row8 = lax.broadcasted_iota(jnp.int32, (SUBLANES, 1), 0)
    xp = pltpu.roll(x, 1, axis=0)
    top = jnp.where(row8 == 0, jnp.where(starts, 0.0, halo_prev[SUBLANES - 1:SUBLANES]), xp[:SUBLANES])
    xp = jnp.concatenate([top, xp[SUBLANES:]], axis=0)
    xn = pltpu.roll(x, tm - 1, axis=0)
    bot = jnp.where(row8 == SUBLANES - 1, jnp.where(ends, 0.0, halo_next[0:1]), xn[tm - SUBLANES:])
    xn = jnp.concatenate([xn[:tm - SUBLANES], bot], axis=0)
    if interior:
        _, first, last = _seq_edges(i, tm, n_lat_rows, seq_lat, seq_ctx)
        xp = jnp.where(first, 0.0, xp)
        xn = jnp.where(last, 0.0, xn)
    return xp, xn


def _mod_kernel(c_ref, w_ref, b_ref, o_ref):
    s = _silu(c_ref[...]).astype(BF16)
    o_ref[0] = jnp.dot(s, w_ref[0].astype(BF16), preferred_element_type=F32) + b_ref[0]


def _modulation(cvec, w_mod, b_mod):
    depth, d, n = w_mod.shape
    tn = 1024
    return pl.pallas_call(
        _mod_kernel,
        grid=(depth, _div(n, tn)),
        in_specs=[pl.BlockSpec((SUBLANES, d), lambda l, j: (0, 0)),
                  pl.BlockSpec((1, d, tn), lambda l, j: (l, 0, j)),
                  pl.BlockSpec((1, 1, tn), lambda l, j: (l, 0, j))],
        out_specs=pl.BlockSpec((1, SUBLANES, tn), lambda l, j: (l, 0, j)),
        out_shape=jax.ShapeDtypeStruct((depth, SUBLANES, n), F32),
        compiler_params=_cparams("arbitrary", "arbitrary"),
        name="modulation",
    )(cvec, w_mod, b_mod.reshape(depth, 1, n))


def _norm_kernel(*refs, modulate):
    if modulate:
        h_ref, g_ref, sh_ref, sc_ref, o_ref = refs
    else:
        h_ref, g_ref, o_ref = refs
    x = h_ref[...]
    y = x * lax.rsqrt(jnp.mean(x * x, axis=-1, keepdims=True) + NORM_EPS) * g_ref[...]
    if modulate:
        y = y * (1.0 + sc_ref[0]) + sh_ref[0]
    o_ref[...] = y.astype(o_ref.dtype)


def _norm(h, g, mod, k_shift, k_scale, seq, nseg, out_dtype, rows=None):
    m, d = h.shape if rows is None else (rows, h.shape[1])
    tm = 1024
    row_spec = pl.BlockSpec((tm, d), lambda i: (i, 0))
    in_specs = [row_spec, pl.BlockSpec((1, d), lambda i: (0, 0))]
    args = [h, g.reshape(1, d)]
    if mod is not None:
        seg = lambda i: jnp.minimum((i * tm) // seq, nseg)
        in_specs += [pl.BlockSpec((1, 1, d), lambda i: (seg(i) * 6 + k_shift, 0, 0)),
                     pl.BlockSpec((1, 1, d), lambda i: (seg(i) * 6 + k_scale, 0, 0))]
        args += [mod, mod]
    return pl.pallas_call(
        functools.partial(_norm_kernel, modulate=mod is not None),
        grid=(_div(m, tm),),
        in_specs=in_specs,
        out_specs=row_spec,
        out_shape=jax.ShapeDtypeStruct((m, d), out_dtype),
        compiler_params=_cparams("arbitrary"),
        name="rmsnorm_mod",
    )(*args)


def _norm_mod_rows(src_ref, dst_ref, dst0, nrows, gain, shift, step=128):
    for r0 in range(0, nrows, step):
        x = src_ref[r0:r0 + min(step, nrows)]
        y = x * lax.rsqrt(jnp.mean(x * x, axis=-1, keepdims=True) + NORM_EPS)
        dst_ref[dst0 + r0:dst0 + r0 + x.shape[0]] = (y * gain + shift).astype(dst_ref.dtype)


def _matmul_kernel(*refs, act, has_bias, has_res, has_norm):
    x_ref, w_ref = refs[0], refs[1]
    k = 2
    if has_norm:
        ng_ref, sh_ref, sc_ref = refs[k:k + 3]
        k += 3
        u_s = refs[-1]

        @pl.when(pl.program_id(1) == 0)
        def _():
            _norm_mod_rows(x_ref, u_s, 0, x_ref.shape[0], ng_ref[...] * (1.0 + sc_ref[0]), sh_ref[0])
        x_ref = u_s
    acc = jnp.dot(x_ref[:, :w_ref.shape[0]], w_ref[...], preferred_element_type=F32)
    if has_bias:
        acc = acc + refs[k][...]
        k += 1
    if act == "gelu":
        acc = jax.nn.gelu(acc, approximate=True)
    elif act == "tanh":
        acc = jnp.tanh(acc)
    elif act == "sigmoid":
        acc = jax.nn.sigmoid(acc)
    if has_res:
        acc = refs[k][...] + refs[k + 1][0] * acc
        k += 2
    o_ref = refs[k]
    o_ref[...] = acc.astype(o_ref.dtype)


def _matmul(x, w, *, bias=None, act=None, res=None, mod=None, k_gate=None, seq=None, nseg=None,
            out_dtype=F32, tm=1024, tn=512, rows=None, w_lead=(), norm=None):
    m = x.shape[0] if rows is None else rows
    kdim, n = w.shape[-2:]
    tn = min(tn, n)
    tm = min(tm, m)
    assert x.shape[1] >= kdim and (x.shape[1] == kdim or kdim % LANES == 0)
    in_specs = [pl.BlockSpec((tm, x.shape[1]), lambda i, j: (i, 0)),
                pl.BlockSpec((None,) * len(w_lead) + (kdim, tn), lambda i, j: (*w_lead, 0, j))]
    args = [x, w]
    scratch = []
    if norm is not None:
        norm_g, norm_mod, k_shift, k_scale = norm
        nseg_of = lambda i: jnp.minimum((i * tm) // seq, nseg)
        in_specs += [pl.BlockSpec((1, kdim), lambda i, j: (0, 0)),
                     pl.BlockSpec((1, 1, kdim), lambda i, j: (nseg_of(i) * 6 + k_shift, 0, 0)),
                     pl.BlockSpec((1, 1, kdim), lambda i, j: (nseg_of(i) * 6 + k_scale, 0, 0))]
        args += [norm_g.reshape(1, kdim), norm_mod, norm_mod]
        scratch = [pltpu.VMEM((tm, kdim), BF16)]
    if bias is not None:
        in_specs.append(pl.BlockSpec((1, tn), lambda i, j: (0, j)))
        args.append(bias.reshape(1, n))
    if res is not None:
        seg = lambda i: jnp.minimum((i * tm) // seq, nseg)
        in_specs += [pl.BlockSpec((tm, tn), lambda i, j: (i, j)),
                     pl.BlockSpec((1, 1, tn), lambda i, j: (seg(i) * 6 + k_gate, 0, j))]
        args += [res, mod]
    return pl.pallas_call(
        functools.partial(_matmul_kernel, act=act, has_bias=bias is not None, has_res=res is not None,
                          has_norm=norm is not None),
        grid=(_div(m, tm), _div(n, tn)),
        in_specs=in_specs,
        out_specs=pl.BlockSpec((tm, tn), lambda i, j: (i, j)),
        out_shape=jax.ShapeDtypeStruct((m, n), out_dtype),
        scratch_shapes=scratch,
        compiler_params=_cparams("arbitrary", "arbitrary"),
        name="matmul",
    )(*args)


def _ffn1_kernel(h_ref, hp_ref, hn_ref, ng_ref, sh_ref, sc_ref, wg_ref, wu_ref, cw_ref, cb_ref, o_ref, u_s,
                 *, tm, n_lat_rows, seq_lat, seq_ctx):
    i = pl.program_id(0)
    hr = hp_ref.shape[0]

    @pl.when(pl.program_id(1) == 0)
    def _():
        gain = ng_ref[...] * (1.0 + sc_ref[0])
        shift = sh_ref[0]
        _norm_mod_rows(hp_ref, u_s, 0, hr, gain, shift)
        _norm_mod_rows(h_ref, u_s, hr, tm, gain, shift)
        _norm_mod_rows(hn_ref, u_s, hr + tm, hr, gain, shift)

    x = u_s[hr:hr + tm]
    g_ext = jnp.dot(u_s[...], wg_ref[...], preferred_element_type=F32)
    g = g_ext[hr:hr + tm]
    gp = g_ext[hr - SUBLANES:hr]
    gn = g_ext[hr + tm:hr + tm + SUBLANES]
    g_prev, g_next = _shift_rows_tile(g, gp, gn, i, tm, n_lat_rows, seq_lat, seq_ctx,
                                      interior=min(seq_lat, seq_ctx) < tm)
    gte = g_prev * cw_ref[0:1] + g * cw_ref[1:2] + g_next * cw_ref[2:3] + cb_ref[...]
    up = jnp.dot(x, wu_ref[...], preferred_element_type=F32)
    o_ref[...] = (_silu(gte) * up).astype(o_ref.dtype)


def _halo_specs(tm, kdim, m):
    hr = BF16_ROWS
    nb = tm // hr
    prev = pl.BlockSpec((hr, kdim), lambda i, j: (jnp.maximum(i * nb - 1, 0), 0))
    nxt = pl.BlockSpec((hr, kdim), lambda i, j: (jnp.minimum((i + 1) * nb, m // hr - 1), 0))
    return prev, nxt


def _cast_pad_kernel(w_ref, o_ref):
    n = w_ref.shape[1]
    o_ref[:, :n] = w_ref[...].astype(o_ref.dtype)
    o_ref[:, n:] = jnp.zeros((o_ref.shape[0], o_ref.shape[1] - n), o_ref.dtype)


def _cast_pad_cols(w, n_pad):
    layers, kdim, n = w.shape
    tr = 256
    return pl.pallas_call(
        _cast_pad_kernel,
        grid=(layers, _div(kdim, tr)),
        in_specs=[pl.BlockSpec((None, tr, n), lambda l, i: (l, i, 0))],
        out_specs=pl.BlockSpec((None, tr, n_pad), lambda l, i: (l, i, 0)),
        out_shape=jax.ShapeDtypeStruct((layers, kdim, n_pad), BF16),
        compiler_params=_cparams("arbitrary", "arbitrary"),
        name="cast_pad",
    )(w)


def _ffn1(h, norm_g, mod, wg, wu, cw, cb, layer, *, rows, nseg, n_lat_rows, seq_lat, seq_ctx):
    _, kdim, f = wg.shape
    tm, tn = 1024, 512
    assert n_lat_rows % tm == 0 and all(s % tm == 0 or tm % s == 0 for s in (seq_lat, seq_ctx))
    prev, nxt = _halo_specs(tm, kdim, rows)
    seg = lambda i: jnp.minimum((i * tm) // seq_lat, nseg)
    return pl.pallas_call(
        functools.partial(_ffn1_kernel, tm=tm, n_lat_rows=n_lat_rows, seq_lat=seq_lat, seq_ctx=seq_ctx),
        grid=(_div(rows, tm), _div(f, tn)),
        in_specs=[pl.BlockSpec((tm, kdim), lambda i, j: (i, 0)), prev, nxt,
                  pl.BlockSpec((1, kdim), lambda i, j: (0, 0)),
                  pl.BlockSpec((1, 1, kdim), lambda i, j: (seg(i) * 6 + 3, 0, 0)),
                  pl.BlockSpec((1, 1, kdim), lambda i, j: (seg(i) * 6 + 4, 0, 0)),
                  pl.BlockSpec((None, kdim, tn), lambda i, j: (layer, 0, j)),
                  pl.BlockSpec((None, kdim, tn), lambda i, j: (layer, 0, j)),
                  pl.BlockSpec((3, tn), lambda i, j: (0, j)),
                  pl.BlockSpec((1, tn), lambda i, j: (0, j))],
        out_specs=pl.BlockSpec((tm, tn), lambda i, j: (i, j)),
        out_shape=jax.ShapeDtypeStruct((rows, f), BF16),
        scratch_shapes=[pltpu.VMEM((tm + 2 * BF16_ROWS, kdim), BF16)],
        compiler_params=_cparams("arbitrary", "arbitrary"),
        name="ffn_gate_up",
    )(h, h, h, norm_g.reshape(1, kdim), mod, mod, wg, wu, cw, cb)


def _pool_kernel(u_ref, w_ref, b_ref, s_ref, h_ref, gate_ref, o_ref, xp_ref, *, seq):
    grp = pl.program_id(1)
    pad = SUBLANES
    zeros = jnp.zeros((pad, u_ref.shape[1]), F32)
    xp_ref[0:pad] = zeros
    xp_ref[pad:pad + seq] = u_ref[...]
    xp_ref[pad + seq:2 * pad + seq] = zeros
    rt = min(seq, 256)
    for gi, win in enumerate(POOL_WINDOWS):
        @pl.when(grp == gi)
        def _(win=win):
            left, right = win // 2, win - 1 - win // 2
            for rc in range(seq // rt):
                base = pad + rc * rt
                t = rc * rt + lax.broadcasted_iota(jnp.int32, (rt, 1), 0)
                acc = xp_ref[base - left:base - left + rt]
                for j in range(1, win):
                    acc = acc + xp_ref[base - left + j:base - left + j + rt]
                cnt = (jnp.minimum(t + right, seq - 1) - jnp.maximum(t - left, 0) + 1).astype(F32)
                p = acc / cnt - xp_ref[base:base + rt]
                y = jnp.dot(p.astype(BF16), w_ref[0], preferred_element_type=F32) + b_ref[...]
                rows = slice(rc * rt, (rc + 1) * rt)
                o_ref[rows] = h_ref[rows] + gate_ref[0] * (y * s_ref[...])


def _pool_mix(u, w, b, scale, h, mod, *, seq, row0, nseq, seg0):
    d = u.shape[1]
    cg = d // len(POOL_WINDOWS)
    blk0 = row0 // seq
    tile = pl.BlockSpec((seq, cg), lambda s, g: (blk0 + s, g))
    vec = pl.BlockSpec((1, cg), lambda s, g: (0, g))
    return pl.pallas_call(
        functools.partial(_pool_kernel, seq=seq),
        grid=(nseq, len(POOL_WINDOWS)),
        in_specs=[tile, pl.BlockSpec((1, cg, cg), lambda s, g: (g, 0, 0)), vec, vec, tile,
                  pl.BlockSpec((1, 1, cg), lambda s, g: ((seg0(s)) * 6 + 2, 0, g))],
        out_specs=tile,
        out_shape=jax.ShapeDtypeStruct(h.shape, F32),
        scratch_shapes=[pltpu.VMEM((seq + 2 * SUBLANES, cg), F32)],
        input_output_aliases={4: 0},
        compiler_params=_cparams("arbitrary", "arbitrary"),
        name="pool_mix",
    )(u, w, b.reshape(1, d), scale.reshape(1, d), h, mod)


def _na_lat_kernel(q_ref, k_ref, v_ref, qc_ref, kc_ref, vc_ref, cos_ref, sin_ref, bias_ref, o_ref, oc_ref,
                   q_s, k_s, *, rows, kh):
    hd = NA_HEAD_DIM
    lane = lax.broadcasted_iota(jnp.int32, (1, LANES), 1)
    low = (lane % 32) < 16
    head0 = lane < hd
    cos, sin = cos_ref[...], sin_ref[...]

    def rope(x):
        swapped = jnp.where(low, pltpu.roll(x, LANES - 16, axis=1), pltpu.roll(x, 16, axis=1))
        return x * cos + swapped * sin

    q_s[...] = (rope(q_ref[...].astype(F32)) * (hd ** -0.5)).astype(BF16)
    k_s[...] = rope(k_ref[...].astype(F32)).astype(BF16)
    nloc = kh * GRID_W
    rows_per_step = 8
    dot = lambda a, b: jnp.dot(a, b, preferred_element_type=F32)
    zero = jnp.zeros((), BF16)

    qc = qc_ref[...] * (hd ** -0.5)
    n_ctx = qc.shape[0]
    qcm = jnp.concatenate([jnp.where(head0, qc, zero), jnp.where(head0, zero, qc)], axis=0)
    sc = _nt(qcm, kc_ref[...])
    pc = jnp.exp(sc - jnp.max(sc, axis=-1, keepdims=True))
    oc = dot(pc.astype(BF16), vc_ref[...]) / jnp.sum(pc, axis=-1, keepdims=True)
    oc_ref[...] = jnp.where(head0, oc[:n_ctx], oc[n_ctx:]).astype(oc_ref.dtype)

    def body(step, carry):
        n = range(rows_per_step)
        r = [step * rows_per_step + dr for dr in n]
        rs = [jnp.clip(r[i] - kh // 2, 0, rows - kh) for i in n]
        q0 = [pl.multiple_of(r[i] * GRID_W, GRID_W) for i in n]
        k0 = [pl.multiple_of(rs[i] * GRID_W, GRID_W) for i in n]
        qr = [q_s[pl.ds(q0[i], GRID_W), :] for i in n]
        qm = [jnp.concatenate([jnp.where(head0, qr[i], zero), jnp.where(head0, zero, qr[i])], axis=0) for i in n]
        s_loc = [_nt(qm[i], k_s[pl.ds(k0[i], nloc), :]) + bias_ref[r[i] - rs[i]] for i in n]
        s_ctx = [_nt(qm[i], kc_ref[...]) for i in n]
        m = [jnp.maximum(jnp.max(s_loc[i], axis=-1, keepdims=True), jnp.max(s_ctx[i], axis=-1, keepdims=True))
             for i in n]
        p_loc = [jnp.exp(s_loc[i] - m[i]) for i in n]
        p_ctx = [jnp.exp(s_ctx[i] - m[i]) for i in n]
        den = [jnp.sum(p_loc[i], axis=-1, keepdims=True) + jnp.sum(p_ctx[i], axis=-1, keepdims=True) for i in n]
        o = [dot(p_loc[i].astype(BF16), v_ref[pl.ds(k0[i], nloc), :]) + dot(p_ctx[i].astype(BF16), vc_ref[...])
             for i in n]
        o = [o[i] / den[i] for i in n]
        for i in n:
            o_ref[pl.ds(q0[i], GRID_W), :] = jnp.where(head0, o[i][:GRID_W], o[i][GRID_W:]).astype(o_ref.dtype)
        return carry

    lax.fori_loop(0, _div(rows, rows_per_step), body, 0)


def _rope_tables(t_len, hd):
    half = hd // 2
    pos = jnp.arange(t_len)
    inv_freq = ROPE_BASE ** (-jnp.arange(0, half, 2, dtype=F32) / half)
    d = jnp.arange(hd)
    p = jnp.where(d[None, :] < half, (pos // GRID_W)[:, None], (pos % GRID_W)[:, None]).astype(F32)
    ang = p * inv_freq[d % (half // 2)][None, :]
    cos = jnp.cos(ang)
    sin = jnp.where((d % half) < half // 2, -jnp.sin(ang), jnp.sin(ang))
    return jnp.tile(cos, (1, 2)), jnp.tile(sin, (1, 2))


def _na_bias_table(rpb, kh):
    col = jnp.arange(GRID_W)
    col_start = jnp.clip(col - NA_KW // 2, 0, GRID_W - NA_KW)
    col_ok = (col[None, :] >= col_start[:, None]) & (col[None, :] < col_start[:, None] + NA_KW)
    dc = jnp.clip(col[None, :] - col[:, None], -(NA_KW - 1), NA_KW - 1) + NA_KW - 1
    rpb_cols = rpb[:, :, dc]
    dr = jnp.arange(kh)[None, :] - jnp.arange(kh)[:, None] + NA_KH - 1
    tab = rpb_cols[:, dr]
    tab = jnp.where(col_ok[None, None, None], tab, -1e30)
    heads = rpb.shape[0]
    tab = tab.transpose(0, 1, 3, 2, 4).reshape(heads // 2, 2, kh, GRID_W, kh * GRID_W)
    return tab.transpose(0, 2, 1, 3, 4).reshape(heads // 2, kh, 2 * GRID_W, kh * GRID_W)


def _neighbourhood_attention(qkv, rpb, *, bsz, t_len, l_len):
    d = qkv.shape[1] // 3
    hd = NA_HEAD_DIM
    heads = d // hd
    npair = d // LANES
    rows = t_len // GRID_W
    kh = min(NA_KH, rows)
    n_lat = bsz * t_len
    n_all = qkv.shape[0]
    ctx0 = _div(n_lat, l_len)
    cos, sin = _rope_tables(t_len, hd)
    bias = _na_bias_table(rpb, kh)
    lat_blk = lambda off: pl.BlockSpec((t_len, LANES), lambda b, p: (b, off + p))
    ctx_blk = lambda off: pl.BlockSpec((l_len, LANES), lambda b, p: (ctx0 + b, off + p))
    tab_blk = pl.BlockSpec((t_len, LANES), lambda b, p: (0, 0))
    o_lat, o_ctx = pl.pallas_call(
        functools.partial(_na_lat_kernel, rows=rows, kh=kh),
        grid=(bsz, npair),
        in_specs=[lat_blk(0), lat_blk(npair), lat_blk(2 * npair), ctx_blk(0), ctx_blk(npair), ctx_blk(2 * npair),
                  tab_blk, tab_blk,
                  pl.BlockSpec((None, kh, 2 * GRID_W, kh * GRID_W), lambda b, p: (p, 0, 0, 0))],
        out_specs=[lat_blk(0), pl.BlockSpec((l_len, LANES), lambda b, p: (b, p))],
        out_shape=[jax.ShapeDtypeStruct((n_lat, d), BF16), jax.ShapeDtypeStruct((n_all - n_lat, d), BF16)],
        scratch_shapes=[pltpu.VMEM((t_len, LANES), BF16), pltpu.VMEM((t_len, LANES), BF16)],
        compiler_params=_cparams("arbitrary", "arbitrary"),
        name="na_attention",
    )(qkv, qkv, qkv, qkv, qkv, qkv, cos, sin, bias)
    return jnp.concatenate([o_lat, o_ctx], axis=0)


def _sg_kernel(z_u_ref, z_v_ref, g_ref, ws_ref, bs_ref, o_ref, *, nchunk):
    zv = z_v_ref[...].astype(F32)
    zv = zv * lax.rsqrt(jnp.mean(zv * zv, axis=-1, keepdims=True) + NORM_EPS) * g_ref[...]
    zvb = zv.astype(BF16)
    for c in range(nchunk):
        rs = slice(c * SG_CHUNK, (c + 1) * SG_CHUNK)
        for g in range(SG_GROUPS):
            ls = slice(g * LANES, (g + 1) * LANES)
            mixed = jnp.dot(ws_ref[g], zvb[rs, ls], preferred_element_type=F32) + bs_ref[:, ls]
            o_ref[rs, ls] = (z_u_ref[rs, ls].astype(F32) * mixed).astype(o_ref.dtype)


def _spatial_gate(z, norm_g, w_s, b_s):
    m, two_w = z.shape
    width = two_w // 2
    nchunk = 2
    tm = nchunk * SG_CHUNK
    bs_full = jnp.repeat(b_s.T, width // SG_GROUPS, axis=1)
    return pl.pallas_call(
        functools.partial(_sg_kernel, nchunk=nchunk),
        grid=(_div(m, tm),),
        in_specs=[pl.BlockSpec((tm, width), lambda i: (i, 0)),
                  pl.BlockSpec((tm, width), lambda i: (i, 1)),
                  pl.BlockSpec((1, width), lambda i: (0, 0)),
                  pl.BlockSpec((SG_GROUPS, SG_CHUNK, SG_CHUNK), lambda i: (0, 0, 0)),
                  pl.BlockSpec((SG_CHUNK, width), lambda i: (0, 0))],
        out_specs=pl.BlockSpec((tm, width), lambda i: (i, 0)),
        out_shape=jax.ShapeDtypeStruct((m, width), BF16),
        compiler_params=_cparams("arbitrary"),
        name="spatial_gate",
    )(z, z, norm_g.reshape(1, width), w_s.astype(BF16), bs_full)


def _shift_kernel(h_ref, hp_ref, hn_ref, ng_ref, sh_ref, sc_ref, mu_ref, *o_refs, tm, n_lat_rows, seq_lat,
                  seq_ctx):
    i = pl.program_id(0)
    o_refs, u_s = o_refs[:-1], o_refs[-1]
    gain = ng_ref[...] * (1.0 + sc_ref[0])
    shift = sh_ref[0]
    pad = SUBLANES
    _norm_mod_rows(hp_ref, u_s, 0, pad, gain, shift)
    _norm_mod_rows(h_ref, u_s, pad, tm, gain, shift)
    _norm_mod_rows(hn_ref, u_s, pad + tm, pad, gain, shift)
    nmix = len(o_refs)
    step, width = 64, 2 * LANES
    for c0 in range(0, u_s.shape[1], width):
        cols = slice(c0, c0 + width)
        mu = [mu_ref[n:n + 1, cols] for n in range(2 * nmix)]
        for r0 in range(0, tm, step):
            row = lax.broadcasted_iota(jnp.int32, (step, 1), 0)
            grow = i * tm + r0 + row
            slen = jnp.where(grow < n_lat_rows, seq_lat, seq_ctx)
            pos = grow & (slen - 1)
            u = u_s[pad + r0:pad + r0 + step, cols]
            prev = jnp.where(pos == 0, 0.0, u_s[pad + r0 - 1:pad + r0 - 1 + step, cols]) - u
            nxt = jnp.where(pos == slen - 1, 0.0, u_s[pad + r0 + 1:pad + r0 + 1 + step, cols]) - u
            for n in range(nmix):
                o_refs[n][r0:r0 + step, cols] = (u + prev * mu[n] + nxt * mu[nmix + n]).astype(BF16)


def _token_shift(h, norm_g, mod, mu, *, nseg, n_lat_rows, seq_lat, seq_ctx):
    m, d = h.shape
    nmix = mu.shape[1]
    tm = 256
    nb = tm // SUBLANES
    tile = pl.BlockSpec((tm, d), lambda i: (i, 0))
    prev = pl.BlockSpec((SUBLANES, d), lambda i: (jnp.maximum(i * nb - 1, 0), 0))
    nxt = pl.BlockSpec((SUBLANES, d), lambda i: (jnp.minimum((i + 1) * nb, m // SUBLANES - 1), 0))
    seg = lambda i: jnp.minimum((i * tm) // seq_lat, nseg)
    return pl.pallas_call(
        functools.partial(_shift_kernel, tm=tm, n_lat_rows=n_lat_rows, seq_lat=seq_lat, seq_ctx=seq_ctx),
        grid=(_div(m, tm),),
        in_specs=[tile, prev, nxt, pl.BlockSpec((1, d), lambda i: (0, 0)),
                  pl.BlockSpec((1, 1, d), lambda i: (seg(i) * 6 + 0, 0, 0)),
                  pl.BlockSpec((1, 1, d), lambda i: (seg(i) * 6 + 1, 0, 0)),
                  pl.BlockSpec((2 * nmix, d), lambda i: (0, 0))],
        out_specs=[tile] * nmix,
        out_shape=[jax.ShapeDtypeStruct((m, d), BF16)] * nmix,
        scratch_shapes=[pltpu.VMEM((tm + 2 * SUBLANES, d), F32)],
        compiler_params=_cparams("arbitrary"),
        name="token_shift",
    )(h, h, h, norm_g.reshape(1, d), mod, mod, mu.reshape(2 * nmix, d))


def _split2(x):
    hi = x.astype(BF16)
    lo = (x - hi.astype(F32)).astype(BF16)
    return hi, lo


def _chunk_chains(dirs, v_ref, *, nheads):
    c = SCAN_CHUNK
    hd = RW_HEAD_DIM
    ri = lax.broadcasted_iota(jnp.int32, (c, c), 0)
    ci = lax.broadcasted_iota(jnp.int32, (c, c), 1)
    strict_d = [(ci > ri) if d[-1] else (ci < ri) for d in dirs]
    incl_d = [(ci >= ri) if d[-1] else (ci <= ri) for d in dirs]
    diag_blk = (ri // SOLVE_BLOCK) == (ci // SOLVE_BLOCK)
    eye = jnp.where(ri == ci, 1.0, 0.0)
    chains = [(di, half, hh) for di in range(len(dirs)) for half in range(LANES // c) for hh in range(nheads)]
    src = [dirs[di] for di, _, _ in chains]
    strict = [strict_d[di] for di, _, _ in chains]
    incl = [incl_d[di] for di, _, _ in chains]
    rows = [slice(half * c, (half + 1) * c) for _, half, _ in chains]
    ls = [slice(hh * hd, (hh + 1) * hd) for _, _, hh in chains]
    pair = [hh // 2 for _, _, hh in chains]
    sub = [slice((hh % 2) * hd, (hh % 2 + 1) * hd) for _, _, hh in chains]
    n = range(len(chains))
    dot = lambda a, b: jnp.dot(a, b, preferred_element_type=F32)
    kt = [src[i][0][rows[i], ls[i]] for i in n]
    rt = [src[i][1][rows[i], ls[i]] for i in n]
    vv = [v_ref[rows[i], ls[i]] for i in n]
    ktrt = [jnp.concatenate([kt[i], rt[i]], axis=0) for i in n]
    pb = [_nt(ktrt[i], src[i][2][rows[i], ls[i]]) for i in n]
    pk = [_nt(ktrt[i], src[i][3][rows[i], ls[i]]) for i in n]
    a1 = [jnp.where(strict[i], pb[i][:c], 0.0) for i in n]
    a3 = [jnp.where(incl[i], pb[i][c:], 0.0).astype(BF16) for i in n]
    ad = [jnp.where(diag_blk, a1[i], 0.0) for i in n]
    ao = [(a1[i] - ad[i]).astype(BF16) for i in n]
    tinv = [eye - ad[i] for i in n]
    pw = [_mm(ad[i], ad[i]) for i in n]
    lhs_v = [jnp.concatenate([jnp.where(strict[i], pk[i][:c], 0.0).astype(BF16),
                              jnp.where(incl[i], pk[i][c:], 0.0).astype(BF16),
                              src[i][5][pair[i], sub[i], rows[i]]], axis=0) for i in n]
    pv = [dot(lhs_v[i], vv[i]) for i in n]
    tinv = [tinv[i] + _mm(tinv[i], pw[i]) for i in n]
    for _ in range(int(math.log2(SOLVE_BLOCK)) - 2):
        pw = [_mm(pw[i], pw[i]) for i in n]
        tinv = [tinv[i] + _mm(tinv[i], pw[i]) for i in n]
    rhs = [jnp.concatenate([kt[i].astype(F32), pv[i][:c]], axis=1).astype(BF16) for i in n]
    tinv = [t.astype(BF16) for t in tinv]
    nmat = [dot(tinv[i], ao[i]).astype(BF16) for i in n]
    x1 = [dot(tinv[i], rhs[i]) for i in n]
    x = x1
    for _ in range(c // SOLVE_BLOCK - 1):
        x = [x1[i] - dot(nmat[i], x[i].astype(BF16)) for i in n]
    lhs_x = [jnp.concatenate([src[i][4][pair[i], sub[i], rows[i]], a3[i]], axis=0) for i in n]
    px = [dot(lhs_x[i], x[i].astype(BF16)) for i in n]
    for i, (_, half, hh) in enumerate(chains):
        m_o, c_o, rp_o, y0_o = src[i][6]
        m_o[hh, 0, :, rows[i]] = (-px[i][:c, :hd]).astype(m_o.dtype)
        c_o[hh, 0, :, rows[i]] = pv[i][2 * c:] - px[i][:c, hd:]
        rp_o[rows[i], ls[i]] = (rt[i].astype(F32) - px[i][c:, :hd]).astype(rp_o.dtype)
        y0_o[rows[i], ls[i]] = pv[i][c:2 * c] - px[i][c:, hd:]


def _rw_chunk_kernel(k_ref, r_ref, v_ref, hw_ref, ha_ref, w2_ref, a2_ref, w0_ref, a0_ref, kk_ref, ka_ref,
                     rk_ref, tri_ref, bc_o, mf_o, cf_o, rpf_o, y0f_o, gf_o, mr_o, cr_o, rpr_o, y0r_o, gr_o,
                     kt_s, rt_s, bh_s, kh_s, v_s, bt_s, kbt_s, *, lora, nheads):
    npair = nheads // 2
    lane = lax.broadcasted_iota(jnp.int32, (1, LANES), 1)
    head0 = lane < RW_HEAD_DIM
    pairs = [slice(p * LANES, (p + 1) * LANES) for p in range(npair)]
    head_sum = lambda x: jnp.concatenate([_head_sum(x[:, ps], head0) for ps in pairs], axis=1)
    k = k_ref[...]
    r = r_ref[...]
    v_s[...] = v_ref[...].astype(BF16)
    kkv = k * kk_ref[...]
    kk = kkv * lax.rsqrt(jnp.maximum(head_sum(kkv * kkv), 1e-12))
    ka = ka_ref[...]
    rk = rk_ref[...]
    hw = hw_ref[...]
    ha = ha_ref[...]
    log_decay, beta, key = [], [], []
    for d in range(2):
        ls = slice(d * lora, (d + 1) * lora)
        pre = w0_ref[d] + jnp.dot(hw[:, ls], w2_ref[d], preferred_element_type=F32)
        z = -pre
        softplus = jnp.maximum(z, 0.0) + jnp.log(1.0 + jnp.exp(-jnp.abs(z)))
        log_decay.append(-jnp.exp(-softplus - 0.5))
        a = jax.nn.sigmoid(a0_ref[d] + jnp.dot(ha[:, ls], a2_ref[d], preferred_element_type=F32))
        beta.append(kk * a)
        key.append(k * (1.0 + (a - 1.0) * ka))
    bc_o[...] = head_sum(r * key[0] * rk) + head_sum(r * key[1] * rk)
    g_o = (gf_o, gr_o)
    for d in range(2):
        ld = log_decay[d]
        pieces = _split2(ld)
        csum = [sum(jnp.dot(tri_ref[e], p, preferred_element_type=F32) for p in pieces) for e in range(2)]
        cum = csum[d]
        total = csum[0] + csum[1] - ld
        kt_s[d] = (kk * jnp.exp(cum - ld)).astype(BF16)
        rt_s[d] = (r * jnp.exp(cum)).astype(BF16)
        inv = jnp.exp(-cum)
        bh_s[d] = (beta[d] * inv).astype(BF16)
        kh_s[d] = (key[d] * inv).astype(BF16)
        tail = jnp.exp(total - cum)
        bbar = beta[d] * tail
        kbar = key[d] * tail
        gdec = jnp.exp(total)
        for p, ps in enumerate(pairs):
            bt_s[d, p] = jnp.transpose(bbar[:, ps]).astype(BF16)
            kbt_s[d, p] = jnp.transpose(kbar[:, ps]).astype(BF16)
            g_o[d][p, 0] = jnp.transpose(gdec[:, ps])
    outs = ((mf_o, cf_o, rpf_o, y0f_o), (mr_o, cr_o, rpr_o, y0r_o))
    _chunk_chains([(kt_s.at[d], rt_s.at[d], bh_s.at[d], kh_s.at[d], bt_s.at[d], kbt_s.at[d], outs[d], d == 1)
                   for d in range(2)], v_s, nheads=nheads)


def _rw_chunks(k, r, v, hw, ha, w2, a2, w0, a0, k_k, k_a, r_k, *, bsz, t_len, l_len, lora):
    m, d = k.shape
    hd = RW_HEAD_DIM
    nheads = 8
    width = nheads * hd
    npair = nheads // 2
    tt = t_len + l_len
    ng = _div(tt, LANES)
    ctx_groups = _div(l_len, LANES)
    ctx0 = _div(bsz * t_len, LANES)

    def in_blk(b, g):
        return jnp.where(g < ctx_groups, ctx0 + b * ctx_groups + g,
                         b * (t_len // LANES) + g - ctx_groups)

    ri = jnp.arange(LANES)[:, None]
    ci = jnp.arange(LANES)[None, :]
    same = (ri // SCAN_CHUNK) == (ci // SCAN_CHUNK)
    tri = jnp.stack([same & (ci <= ri), same & (ci >= ri)]).astype(BF16)
    tile = pl.BlockSpec((LANES, width), lambda b, p, g: (in_blk(b, g), p))
    lora_blk = pl.BlockSpec((LANES, 2 * lora), lambda b, p, g: (in_blk(b, g), 0))
    wl_blk = pl.BlockSpec((2, lora, width), lambda b, p, g: (0, 0, p))
    v2_blk = pl.BlockSpec((2, 1, width), lambda b, p, g: (0, 0, p))
    v1_blk = pl.BlockSpec((1, width), lambda b, p, g: (0, p))
    tri_blk = pl.BlockSpec((2, LANES, LANES), lambda b, p, g: (0, 0, 0))
    row_o = pl.BlockSpec((None, LANES, width), lambda b, p, g: (b, g, p))
    sq_o = pl.BlockSpec((None, nheads, 1, hd, LANES), lambda b, p, g: (b, p, g, 0, 0))
    tr_o = pl.BlockSpec((None, npair, 1, LANES, LANES), lambda b, p, g: (b, p, g, 0, 0))
    row_shape = lambda dt: jax.ShapeDtypeStruct((bsz, tt, d), dt)
    sq_shape = lambda dt: jax.ShapeDtypeStruct((bsz, d // hd, ng, hd, LANES), dt)
    tr_shape = jax.ShapeDtypeStruct((bsz, d // LANES, ng, LANES, LANES), F32)
    dir_specs = [sq_o, sq_o, row_o, row_o, tr_o]
    dir_shapes = [sq_shape(BF16), sq_shape(F32), row_shape(BF16), row_shape(F32), tr_shape]
    operand = lambda: pltpu.VMEM((2, LANES, width), BF16)
    transposed = lambda: pltpu.VMEM((2, npair, LANES, LANES), BF16)
    out = pl.pallas_call(
        functools.partial(_rw_chunk_kernel, lora=lora, nheads=nheads),
        grid=(bsz, _div(d, width), ng),
        in_specs=[tile, tile, tile, lora_blk, lora_blk, wl_blk, wl_blk, v2_blk, v2_blk, v1_blk, v1_blk, v1_blk,
                  tri_blk],
        out_specs=[row_o] + dir_specs + dir_specs,
        out_shape=[row_shape(F32)] + dir_shapes + dir_shapes,
        scratch_shapes=[operand(), operand(), operand(), operand(), pltpu.VMEM((LANES, width), BF16),
                        transposed(), transposed()],
        compiler_params=_cparams("arbitrary", "arbitrary", "arbitrary"),
        name="rwkv_chunks",
    )(k, r, v, hw, ha, w2, a2, w0.reshape(2, 1, d), a0.reshape(2, 1, d), k_k.reshape(1, d),
      k_a.reshape(1, d), r_k.reshape(1, d), tri)
    return out[0], out[1:6], out[6:11]


def _state_pass_kernel(mf_ref, cf_ref, gf_ref, mr_ref, cr_ref, gr_ref, sf_o, sr_o, *, ctx_groups, n_groups,
                       nheads):
    c = SCAN_CHUNK
    hd = RW_HEAD_DIM
    heads = range(nheads)
    sf_o[...] = jnp.zeros(sf_o.shape, sf_o.dtype)
    sr_o[...] = jnp.zeros(sr_o.shape, sr_o.dtype)
    dirs = ((mf_ref, cf_ref, gf_ref, sf_o, False), (mr_ref, cr_ref, gr_ref, sr_o, True))

    def group(i, carry):
        out = []
        g_of = [i, jnp.where(i < ctx_groups, ctx_groups - 1 - i, n_groups - 1 - (i - ctx_groups))]
        states = [list(carry[0]), list(carry[1])]
        for step in range(LANES // c):
            s0b, ms = [[], []], [[], []]
            for di, (m_ref, c_ref, g_ref, s_o, rev) in enumerate(dirs):
                half = (LANES // c - 1 - step) if rev else step
                ts = slice(half * c, (half + 1) * c)
                for hh in heads:
                    sb = states[di][hh].astype(BF16)
                    s0b[di].append(sb)
                    blk = slice((hh % 2) * hd, (hh % 2 + 1) * hd)
                    s_o[hh // 2, g_of[di] * (LANES // c) + half, blk, blk] = sb
            for di, (m_ref, c_ref, g_ref, s_o, rev) in enumerate(dirs):
                half = (LANES // c - 1 - step) if rev else step
                ts = slice(half * c, (half + 1) * c)
                ms[di] = [jnp.dot(m_ref[hh, g_of[di], :, ts], s0b[di][hh], preferred_element_type=F32)
                          for hh in heads]
            for di, (m_ref, c_ref, g_ref, s_o, rev) in enumerate(dirs):
                half = (LANES // c - 1 - step) if rev else step
                ts = slice(half * c, (half + 1) * c)
                for hh in heads:
                    blk = slice((hh % 2) * hd, (hh % 2 + 1) * hd)
                    gc = g_ref[hh // 2, g_of[di], blk, half * c:half * c + 1]
                    states[di][hh] = gc * states[di][hh] + ms[di][hh] + c_ref[hh, g_of[di], :, ts]
        return tuple(states[0]), tuple(states[1])

    zero = tuple(jnp.zeros((hd, hd), F32) for _ in heads)
    lax.fori_loop(0, n_groups, group, (zero, zero))


def _state_pass(m_f, c_f, g_f, m_r, c_r, g_r, *, l_len):
    bsz, heads, ng, hd, _ = m_f.shape
    nheads = 4
    npair = nheads // 2
    nc = ng * (LANES // SCAN_CHUNK)
    sq = pl.BlockSpec((None, nheads, ng, hd, LANES), lambda b, p: (b, p, 0, 0, 0))
    tr = pl.BlockSpec((None, npair, ng, LANES, LANES), lambda b, p: (b, p, 0, 0, 0))
    st = pl.BlockSpec((None, npair, nc, LANES, LANES), lambda b, p: (b, p, 0, 0, 0))
    st_shape = jax.ShapeDtypeStruct((bsz, heads // 2, nc, LANES, LANES), BF16)
    return pl.pallas_call(
        functools.partial(_state_pass_kernel, ctx_groups=l_len // LANES, n_groups=ng, nheads=nheads),
        grid=(bsz, _div(heads, nheads)),
        in_specs=[sq, sq, tr, sq, sq, tr],
        out_specs=[st, st],
        out_shape=[st_shape, st_shape],
        compiler_params=_cparams("arbitrary", "arbitrary"),
        name="rwkv_state",
    )(m_f, c_f, g_f, m_r, c_r, g_r)


def _readout_kernel(rpf_ref, y0f_ref, sf_ref, rpr_ref, y0r_ref, sr_ref, bc_ref, v_ref, g_ref, lng_ref,
                    lnb_ref, o_ref):
    c = SCAN_CHUNK
    lane = lax.broadcasted_iota(jnp.int32, (1, LANES), 1)
    head0 = lane < RW_HEAD_DIM
    inv_n = 1.0 / RW_HEAD_DIM
    tiles = [(cc, pp) for cc in range(o_ref.shape[0] // c) for pp in range(o_ref.shape[1] // LANES)]
    rows = [slice(cc * c, (cc + 1) * c) for cc, _ in tiles]
    ls = [slice(pp * LANES, (pp + 1) * LANES) for _, pp in tiles]
    n = range(len(tiles))
    dot = lambda a, b: jnp.dot(a, b, preferred_element_type=F32)
    yf = [dot(rpf_ref[rows[i], ls[i]], sf_ref[tiles[i][1], tiles[i][0]]) for i in n]
    yr = [dot(rpr_ref[rows[i], ls[i]], sr_ref[tiles[i][1], tiles[i][0]]) for i in n]
    for i in n:
        y = (yf[i] + y0f_ref[rows[i], ls[i]]) + (yr[i] + y0r_ref[rows[i], ls[i]])
        mean = _head_sum(y, head0) * inv_n
        yc = y - mean
        var = _head_sum(yc * yc, head0) * inv_n
        yn = yc * lax.rsqrt(var + RW_GN_EPS) * lng_ref[:, ls[i]] + lnb_ref[:, ls[i]]
        o_ref[rows[i], ls[i]] = ((yn + bc_ref[rows[i], ls[i]] * v_ref[rows[i], ls[i]])
                                 * g_ref[rows[i], ls[i]]).astype(o_ref.dtype)


def _rw_readout(rp_f, y0_f, s_f, rp_r, y0_r, s_r, bc, v, g, ln_g, ln_b, *, bsz, t_len, l_len):
    d = v.shape[1]
    tm, tn = 256, 512
    per_b = t_len // tm
    off = l_len // tm
    blk = lambda i: (i // per_b, off + i % per_b)
    scan_blk = pl.BlockSpec((None, tm, tn), lambda i, p: (*blk(i), p))
    st_blk = pl.BlockSpec((None, tn // LANES, tm // SCAN_CHUNK, LANES, LANES),
                          lambda i, p: (blk(i)[0], p, blk(i)[1], 0, 0))
    tile = pl.BlockSpec((tm, tn), lambda i, p: (i, p))
    vec = pl.BlockSpec((1, tn), lambda i, p: (0, p))
    return pl.pallas_call(
        _readout_kernel,
        grid=(_div(bsz * t_len, tm), _div(d, tn)),
        in_specs=[scan_blk, scan_blk, st_blk, scan_blk, scan_blk, st_blk, scan_blk, tile, tile, vec, vec],
        out_specs=tile,
        out_shape=jax.ShapeDtypeStruct((bsz * t_len, d), BF16),
        compiler_params=_cparams("arbitrary", "arbitrary"),
        name="rwkv_readout",
    )(rp_f, y0_f, s_f, rp_r, y0_r, s_r, bc, v, g, ln_g.reshape(1, d), ln_b.reshape(1, d))


def _pad_cols(w, n):
    return jnp.pad(w, ((0, 0), (0, n - w.shape[1])))


def _pad_rows(w, n):
    return jnp.pad(w, ((0, n - w.shape[0]), (0, 0)))


def kernel(x, c, ctx, c_ctx, norm1_g, norm2_g, w_mod, b_mod, ffn_w_gate, ffn_w_up, ffn_conv_w, ffn_conv_b, ffn_w_down, final_norm_g, pool_w, pool_b, pool_scale, na_w_qkv, na_rpb, na_w_o, sg_w_in, sg_b_in, sg_norm_g, sg_w_s, sg_b_s, sg_w_o, rw_mu, rw_w_rkv, rw_w0, rw_w1, rw_w2, rw_a0, rw_a1, rw_a2, rw_g1, rw_g2, rw_k_k, rw_k_a, rw_r_k, rw_ln_g, rw_ln_b, rw_w_o):
    bsz, t_len, d = x.shape
    l_len = ctx.shape[1]
    depth = norm1_g.shape[0]
    n_mixers = 4
    n_lat = bsz * t_len
    n_all = n_lat + bsz * l_len
    f = ffn_w_gate.shape[2]
    f_pad = -(-f // 512) * 512
    wg_all = _cast_pad_cols(ffn_w_gate, f_pad)
    wu_all = _cast_pad_cols(ffn_w_up, f_pad)
    wd_all = ffn_w_down.astype(BF16)

    cvec = jnp.concatenate([c, c_ctx[None], jnp.zeros((SUBLANES - bsz - 1, d), F32)], axis=0)
    mods = _modulation(cvec, w_mod, b_mod)
    h = jnp.concatenate([x.reshape(n_lat, d), ctx.reshape(bsz * l_len, d)], axis=0)
    seg = dict(seq=t_len, nseg=bsz)
    edges = dict(n_lat_rows=n_lat, seq_lat=t_len, seq_ctx=l_len)

    for i in range(depth):
        m_kind, j = i % n_mixers, i // n_mixers
        last = i == depth - 1
        mod = mods[i].reshape(SUBLANES * 6, 1, d)
        rows = n_lat if last else n_all
        if m_kind == 0:
            u = _norm(h, norm1_g[i], mod, 0, 1, t_len, bsz, F32, rows=rows)
            pw = pool_w[j].astype(BF16)
            h = _pool_mix(u, pw, pool_b[j], pool_scale[j], h, mod, seq=t_len, row0=0, nseq=bsz,
                          seg0=lambda s: s)
            if not last:
                h = _pool_mix(u, pw, pool_b[j], pool_scale[j], h, mod, seq=l_len, row0=n_lat, nseq=bsz,
                              seg0=lambda s: bsz)
        elif m_kind == 1:
            qkv = _matmul(h, na_w_qkv[j].astype(BF16), out_dtype=BF16, norm=(norm1_g[i], mod, 0, 1), **seg)
            o = _neighbourhood_attention(qkv, na_rpb[j], bsz=bsz, t_len=t_len, l_len=l_len)
            h = _matmul(o, na_w_o[j].astype(BF16), res=h, mod=mod, k_gate=2, rows=rows, **seg)
        elif m_kind == 2:
            z = _matmul(h, sg_w_in[j].astype(BF16), bias=sg_b_in[j], act="gelu", out_dtype=BF16, rows=rows,
                        norm=(norm1_g[i], mod, 0, 1), **seg)
            gated = _spatial_gate(z, sg_norm_g[j], sg_w_s[j], sg_b_s[j])
            h = _matmul(gated, sg_w_o[j].astype(BF16), res=h, mod=mod, k_gate=2, rows=rows, **seg)
        else:
            lora = LANES
            xr, xw, xk, xv, xa, xg = _token_shift(h, norm1_g[i], mod, rw_mu[j], nseg=bsz, **edges)
            w_rkv = rw_w_rkv.astype(BF16)
            r = _matmul(xr, w_rkv, w_lead=(j, 0))
            k = _matmul(xk, w_rkv, w_lead=(j, 1))
            v = _matmul(xv, w_rkv, w_lead=(j, 2))
            w1 = jnp.concatenate([_pad_cols(rw_w1[j, e], lora) for e in range(2)], axis=1).astype(BF16)
            a1 = jnp.concatenate([_pad_cols(rw_a1[j, e], lora) for e in range(2)], axis=1).astype(BF16)
            w2 = jnp.stack([_pad_rows(rw_w2[j, e], lora) for e in range(2)]).astype(BF16)
            a2 = jnp.stack([_pad_rows(rw_a2[j, e], lora) for e in range(2)]).astype(BF16)
            hw = _matmul(xw, w1, act="tanh", out_dtype=BF16)
            ha = _matmul(xa, a1, out_dtype=BF16)
            hg = _matmul(xg, rw_g1[j].astype(BF16), act="sigmoid", out_dtype=BF16, rows=n_lat)
            g = _matmul(hg, rw_g2[j].astype(BF16))
            bc, (m_f, c_f, rp_f, y0_f, g_f), (m_r, c_r, rp_r, y0_r, g_r) = _rw_chunks(
                k, r, v, hw, ha, w2, a2, rw_w0[j], rw_a0[j], rw_k_k[j], rw_k_a[j], rw_r_k[j],
                bsz=bsz, t_len=t_len, l_len=l_len, lora=lora)
            s_f, s_r = _state_pass(m_f, c_f, g_f, m_r, c_r, g_r, l_len=l_len)
            o = _rw_readout(rp_f, y0_f, s_f, rp_r, y0_r, s_r, bc, v, g, rw_ln_g[j], rw_ln_b[j],
                            bsz=bsz, t_len=t_len, l_len=l_len)
            h = _matmul(o, rw_w_o[j].astype(BF16), res=h, mod=mod, k_gate=2, rows=n_lat, **seg)
        cw = _pad_cols(ffn_conv_w[i], f_pad)
        cb = _pad_cols(ffn_conv_b[i][None], f_pad)
        mid = _ffn1(h, norm2_g[i], mod, wg_all, wu_all, cw, cb, i, rows=rows, nseg=bsz, **edges)
        h = _matmul(mid, wd_all, w_lead=(i,), res=h, mod=mod, k_gate=5, rows=rows, tm=1024, **seg)

    out = _norm(h, final_norm_g, None, 0, 0, t_len, bsz, F32, rows=n_lat)
    return out.reshape(bsz, t_len, d)
```

```python
import functools
import math

import jax
import jax.numpy as jnp
from jax import lax
from jax.experimental import pallas as pl
from jax.experimental.pallas import tpu as pltpu

F32 = jnp.float32
BF16 = jnp.bfloat16

GRID_W = 64
NORM_EPS = 1e-6
POOL_WINDOWS = (2, 4, 8, 16)
NA_HEAD_DIM = 64
NA_KH = 8
NA_KW = 16
ROPE_BASE = 10000.0
SG_CHUNK = 128
SG_GROUPS = 16
RW_HEAD_DIM = 64
RW_GN_EPS = 64e-5

LANES = 128
SUBLANES = 8
BF16_ROWS = 16
VMEM_LIMIT = 56 * 1024 * 1024

SCAN_CHUNK = 64
SOLVE_BLOCK = 16


def _cparams(*sem):
    return pltpu.CompilerParams(dimension_semantics=sem, vmem_limit_bytes=VMEM_LIMIT)


def _div(a, b):
    assert a % b == 0, (a, b)
    return a // b


def _nt(a, b):
    return lax.dot_general(a, b, (((1,), (1,)), ((), ())), preferred_element_type=F32)


def _mm(a, b):
    return jnp.dot(a.astype(BF16), b.astype(BF16), preferred_element_type=F32)


def _silu(x):
    return x * jax.nn.sigmoid(x)


def _head_sum(x, lane_is_head0):
    s0 = jnp.sum(jnp.where(lane_is_head0, x, 0.0), axis=-1, keepdims=True)
    s1 = jnp.sum(jnp.where(lane_is_head0, 0.0, x), axis=-1, keepdims=True)
    return jnp.where(lane_is_head0, s0, s1)


def _seq_edges(i, tm, n_lat_rows, seq_lat, seq_ctx):
    row = lax.broadcasted_iota(jnp.int32, (tm, 1), 0)
    grow = i * tm + row
    slen = jnp.where(grow < n_lat_rows, seq_lat, seq_ctx)
    pos = grow & (slen - 1)
    return row, pos == 0, pos == slen - 1


def _shift_rows(x, halo_prev, halo_next, row, first, last, tm):
    xp = pltpu.roll(x, 1, axis=0)
    xp = jnp.where(row == 0, halo_prev[SUBLANES - 1:SUBLANES], xp)
    xp = jnp.where(first, 0.0, xp)
    xn = pltpu.roll(x, tm - 1, axis=0)
    xn = jnp.where(row == tm - 1, halo_next[0:1], xn)
    xn = jnp.where(last, 0.0, xn)
    return xp, xn


def _shift_rows_tile(x, halo_prev, halo_next, i, tm, n_lat_rows, seq_lat, seq_ctx, interior):
    r0 = i * tm
    slen = jnp.where(r0 < n_lat_rows, seq_lat, seq_ctx)
    starts = (r0 & (slen - 1)) == 0
    ends = ((r0 + tm) & (slen - 1)) == 0
    row8 = lax.broadcasted_iota(jnp.int32, (SUBLANES, 1), 0)
    xp = pltpu.roll(x, 1, axis=0)
    top = jnp.where(row8 == 0, jnp.where(starts, 0.0, halo_prev[SUBLANES - 1:SUBLANES]), xp[:SUBLANES])
    xp = jnp.concatenate([top, xp[SUBLANES:]], axis=0)
    xn = pltpu.roll(x, tm - 1, axis=0)
    bot = jnp.where(row8 == SUBLANES - 1, jnp.where(ends, 0.0, halo_next[0:1]), xn[tm - SUBLANES:])
    xn = jnp.concatenate([xn[:tm - SUBLANES], bot], axis=0)
    if interior:
        _, first, last = _seq_edges(i, tm, n_lat_rows, seq_lat, seq_ctx)
        xp = jnp.where(first, 0.0, xp)
        xn = jnp.where(last, 0.0, xn)
    return xp, xn


def _mod_kernel(c_ref, w_ref, b_ref, o_ref):
    s = _silu(c_ref[...]).astype(BF16)
    o_ref[0] = jnp.dot(s, w_ref[0].astype(BF16), preferred_element_type=F32) + b_ref[0]


def _modulation(cvec, w_mod, b_mod):
    depth, d, n = w_mod.shape
    tn = 1024
    return pl.pallas_call(
        _mod_kernel,
        grid=(depth, _div(n, tn)),
        in_specs=[pl.BlockSpec((SUBLANES, d), lambda l, j: (0, 0)),
                  pl.BlockSpec((1, d, tn), lambda l, j: (l, 0, j)),
                  pl.BlockSpec((1, 1, tn), lambda l, j: (l, 0, j))],
        out_specs=pl.BlockSpec((1, SUBLANES, tn), lambda l, j: (l, 0, j)),
        out_shape=jax.ShapeDtypeStruct((depth, SUBLANES, n), F32),
        compiler_params=_cparams("arbitrary", "arbitrary"),
        name="modulation",
    )(cvec, w_mod, b_mod.reshape(depth, 1, n))


def _norm_kernel(*refs, modulate):
    if modulate:
        h_ref, g_ref, sh_ref, sc_ref, o_ref = refs
    else:
        h_ref, g_ref, o_ref = refs
    x = h_ref[...]
    y = x * lax.rsqrt(jnp.mean(x * x, axis=-1, keepdims=True) + NORM_EPS) * g_ref[...]
    if modulate:
        y = y * (1.0 + sc_ref[0]) + sh_ref[0]
    o_ref[...] = y.astype(o_ref.dtype)


def _norm(h, g, mod, k_shift, k_scale, seq, nseg, out_dtype, rows=None, batched_out=False):
    m, d = h.shape if rows is None else (rows, h.shape[1])
    tm = 1024
    row_spec = pl.BlockSpec((tm, d), lambda i: (i, 0))
    out_spec, out_shape = row_spec, (m, d)
    if batched_out:
        per_seq = _div(seq, tm)
        out_spec = pl.BlockSpec((None, tm, d), lambda i: (i // per_seq, i % per_seq, 0))
        out_shape = (_div(m, seq), seq, d)
    in_specs = [row_spec, pl.BlockSpec((1, d), lambda i: (0, 0))]
    args = [h, g.reshape(1, d)]
    if mod is not None:
        seg = lambda i: jnp.minimum((i * tm) // seq, nseg)
        in_specs += [pl.BlockSpec((1, 1, d), lambda i: (seg(i) * 6 + k_shift, 0, 0)),
                     pl.BlockSpec((1, 1, d), lambda i: (seg(i) * 6 + k_scale, 0, 0))]
        args += [mod, mod]
    return pl.pallas_call(
        functools.partial(_norm_kernel, modulate=mod is not None),
        grid=(_div(m, tm),),
        in_specs=in_specs,
        out_specs=out_spec,
        out_shape=jax.ShapeDtypeStruct(out_shape, out_dtype),
        compiler_params=_cparams("arbitrary"),
        name="rmsnorm_mod",
    )(*args)


def _pack_rows_kernel(x_ref, c_ref, o_ref, *, lat_tiles):
    i = pl.program_id(0)

    @pl.when(i < lat_tiles)
    def _():
        o_ref[...] = x_ref[...]

    @pl.when(i >= lat_tiles)
    def _():
        o_ref[...] = c_ref[...]


def _pack_rows(x, ctx):
    bsz, t_len, d = x.shape
    n_ctx = bsz * ctx.shape[1]
    tm = 1024
    per_seq = _div(t_len, tm)
    lat_tiles = bsz * per_seq
    lat = lambda i: jnp.minimum(i, lat_tiles - 1)
    return pl.pallas_call(
        functools.partial(_pack_rows_kernel, lat_tiles=lat_tiles),
        grid=(lat_tiles + _div(n_ctx, tm),),
        in_specs=[pl.BlockSpec((None, tm, d), lambda i: (lat(i) // per_seq, lat(i) % per_seq, 0)),
                  pl.BlockSpec((tm, d), lambda i: (jnp.maximum(i - lat_tiles, 0), 0))],
        out_specs=pl.BlockSpec((tm, d), lambda i: (i, 0)),
        out_shape=jax.ShapeDtypeStruct((bsz * t_len + n_ctx, d), x.dtype),
        compiler_params=_cparams("arbitrary"),
        name="pack_rows",
    )(x, ctx.reshape(n_ctx, d))


def _norm_mod_rows(src_ref, dst_ref, dst0, nrows, gain, shift, step=128):
    for r0 in range(0, nrows, step):
        x = src_ref[r0:r0 + min(step, nrows)]
        y = x * lax.rsqrt(jnp.mean(x * x, axis=-1, keepdims=True) + NORM_EPS)
        dst_ref[dst0 + r0:dst0 + r0 + x.shape[0]] = (y * gain + shift).astype(dst_ref.dtype)


def _matmul_kernel(*refs, act, has_bias, has_res, has_norm):
    x_ref, w_ref = refs[0], refs[1]
    k = 2
    if has_norm:
        ng_ref, sh_ref, sc_ref = refs[k:k + 3]
        k += 3
        u_s = refs[-1]

        @pl.when(pl.program_id(1) == 0)
        def _():
            _norm_mod_rows(x_ref, u_s, 0, x_ref.shape[0], ng_ref[...] * (1.0 + sc_ref[0]), sh_ref[0])
        x_ref = u_s
    acc = jnp.dot(x_ref[:, :w_ref.shape[0]], w_ref[...], preferred_element_type=F32)
    if has_bias:
        acc = acc + refs[k][...]
        k += 1
    if act == "gelu":
        acc = jax.nn.gelu(acc, approximate=True)
    elif act == "tanh":
        acc = jnp.tanh(acc)
    elif act == "sigmoid":
        acc = jax.nn.sigmoid(acc)
    if has_res:
        acc = refs[k][...] + refs[k + 1][0] * acc
        k += 2
    o_ref = refs[k]
    o_ref[...] = acc.astype(o_ref.dtype)


def _matmul(x, w, *, bias=None, act=None, res=None, mod=None, k_gate=None, seq=None, nseg=None,
            out_dtype=F32, tm=1024, tn=512, rows=None, w_lead=(), norm=None):
    m = x.shape[0] if rows is None else rows
    kdim, n = w.shape[-2:]
    tn = min(tn, n)
    tm = min(tm, m)
    assert x.shape[1] >= kdim and (x.shape[1] == kdim or kdim % LANES == 0)
    in_specs = [pl.BlockSpec((tm, x.shape[1]), lambda i, j: (i, 0)),
                pl.BlockSpec((None,) * len(w_lead) + (kdim, tn), lambda i, j: (*w_lead, 0, j))]
    args = [x, w]
    scratch = []
    if norm is not None:
        norm_g, norm_mod, k_shift, k_scale = norm
        nseg_of = lambda i: jnp.minimum((i * tm) // seq, nseg)
        in_specs += [pl.BlockSpec((1, kdim), lambda i, j: (0, 0)),
                     pl.BlockSpec((1, 1, kdim), lambda i, j: (nseg_of(i) * 6 + k_shift, 0, 0)),
                     pl.BlockSpec((1, 1, kdim), lambda i, j: (nseg_of(i) * 6 + k_scale, 0, 0))]
        args += [norm_g.reshape(1, kdim), norm_mod, norm_mod]
        scratch = [pltpu.VMEM((tm, kdim), BF16)]
    if bias is not None:
        in_specs.append(pl.BlockSpec((1, tn), lambda i, j: (0, j)))
        args.append(bias.reshape(1, n))
    if res is not None:
        seg = lambda i: jnp.minimum((i * tm) // seq, nseg)
        in_specs += [pl.BlockSpec((tm, tn), lambda i, j: (i, j)),
                     pl.BlockSpec((1, 1, tn), lambda i, j: (seg(i) * 6 + k_gate, 0, j))]
        args += [res, mod]
    return pl.pallas_call(
        functools.partial(_matmul_kernel, act=act, has_bias=bias is not None, has_res=res is not None,
                          has_norm=norm is not None),
        grid=(_div(m, tm), _div(n, tn)),
        in_specs=in_specs,
        out_specs=pl.BlockSpec((tm, tn), lambda i, j: (i, j)),
        out_shape=jax.ShapeDtypeStruct((m, n), out_dtype),
        scratch_shapes=scratch,
        compiler_params=_cparams("arbitrary", "arbitrary"),
        name="matmul",
    )(*args)


def _ffn1_kernel(h_ref, hp_ref, hn_ref, ng_ref, sh_ref, sc_ref, wg_ref, wu_ref, cw_ref, cb_ref, o_ref, u_s,
                 *, tm, n_lat_rows, seq_lat, seq_ctx):
    i = pl.program_id(0)
    hr = hp_ref.shape[0]

    @pl.when(pl.program_id(1) == 0)
    def _():
        gain = ng_ref[...] * (1.0 + sc_ref[0])
        shift = sh_ref[0]
        _norm_mod_rows(hp_ref, u_s, 0, hr, gain, shift)
        _norm_mod_rows(h_ref, u_s, hr, tm, gain, shift)
        _norm_mod_rows(hn_ref, u_s, hr + tm, hr, gain, shift)

    x = u_s[hr:hr + tm]
    g_ext = jnp.dot(u_s[...], wg_ref[...], preferred_element_type=F32)
    g = g_ext[hr:hr + tm]
    gp = g_ext[hr - SUBLANES:hr]
    gn = g_ext[hr + tm:hr + tm + SUBLANES]
    g_prev, g_next = _shift_rows_tile(g, gp, gn, i, tm, n_lat_rows, seq_lat, seq_ctx,
                                      interior=min(seq_lat, seq_ctx) < tm)
    gte = g_prev * cw_ref[0:1] + g * cw_ref[1:2] + g_next * cw_ref[2:3] + cb_ref[...]
    up = jnp.dot(x, wu_ref[...], preferred_element_type=F32)
    o_ref[...] = (_silu(gte) * up).astype(o_ref.dtype)


def _halo_specs(tm, kdim, m):
    hr = BF16_ROWS
    nb = tm // hr
    prev = pl.BlockSpec((hr, kdim), lambda i, j: (jnp.maximum(i * nb - 1, 0), 0))
    nxt = pl.BlockSpec((hr, kdim), lambda i, j: (jnp.minimum((i + 1) * nb, m // hr - 1), 0))
    return prev, nxt


def _cast_pad_kernel(w_ref, o_ref):
    n = w_ref.shape[1]
    o_ref[:, :n] = w_ref[...].astype(o_ref.dtype)
    o_ref[:, n:] = jnp.zeros((o_ref.shape[0], o_ref.shape[1] - n), o_ref.dtype)


def _cast_pad_cols(w, n_pad):
    layers, kdim, n = w.shape
    tr = 256
    return pl.pallas_call(
        _cast_pad_kernel,
        grid=(layers, _div(kdim, tr)),
        in_specs=[pl.BlockSpec((None, tr, n), lambda l, i: (l, i, 0))],
        out_specs=pl.BlockSpec((None, tr, n_pad), lambda l, i: (l, i, 0)),
        out_shape=jax.ShapeDtypeStruct((layers, kdim, n_pad), BF16),
        compiler_params=_cparams("arbitrary", "arbitrary"),
        name="cast_pad",
    )(w)


def _ffn1(h, norm_g, mod, wg, wu, cw, cb, layer, *, rows, nseg, n_lat_rows, seq_lat, seq_ctx):
    _, kdim, f = wg.shape
    tm, tn = 1024, 512
    assert n_lat_rows % tm == 0 and all(s % tm == 0 or tm % s == 0 for s in (seq_lat, seq_ctx))
    prev, nxt = _halo_specs(tm, kdim, rows)
    seg = lambda i: jnp.minimum((i * tm) // seq_lat, nseg)
    return pl.pallas_call(
        functools.partial(_ffn1_kernel, tm=tm, n_lat_rows=n_lat_rows, seq_lat=seq_lat, seq_ctx=seq_ctx),
        grid=(_div(rows, tm), _div(f, tn)),
        in_specs=[pl.BlockSpec((tm, kdim), lambda i, j: (i, 0)), prev, nxt,
                  pl.BlockSpec((1, kdim), lambda i, j: (0, 0)),
                  pl.BlockSpec((1, 1, kdim), lambda i, j: (seg(i) * 6 + 3, 0, 0)),
                  pl.BlockSpec((1, 1, kdim), lambda i, j: (seg(i) * 6 + 4, 0, 0)),
                  pl.BlockSpec((None, kdim, tn), lambda i, j: (layer, 0, j)),
                  pl.BlockSpec((None, kdim, tn), lambda i, j: (layer, 0, j)),
                  pl.BlockSpec((3, tn), lambda i, j: (0, j)),
                  pl.BlockSpec((1, tn), lambda i, j: (0, j))],
        out_specs=pl.BlockSpec((tm, tn), lambda i, j: (i, j)),
        out_shape=jax.ShapeDtypeStruct((rows, f), BF16),
        scratch_shapes=[pltpu.VMEM((tm + 2 * BF16_ROWS, kdim), BF16)],
        compiler_params=_cparams("arbitrary", "arbitrary"),
        name="ffn_gate_up",
    )(h, h, h, norm_g.reshape(1, kdim), mod, mod, wg, wu, cw, cb)


def _pool_kernel(u_ref, w_ref, b_ref, s_ref, h_ref, gate_ref, o_ref, xp_ref, *, seq):
    grp = pl.program_id(1)
    pad = SUBLANES
    zeros = jnp.zeros((pad, u_ref.shape[1]), F32)
    xp_ref[0:pad] = zeros
    xp_ref[pad:pad + seq] = u_ref[...]
    xp_ref[pad + seq:2 * pad + seq] = zeros
    rt = min(seq, 256)
    for gi, win in enumerate(POOL_WINDOWS):
        @pl.when(grp == gi)
        def _(win=win):
            left, right = win // 2, win - 1 - win // 2
            for rc in range(seq // rt):
                base = pad + rc * rt
                t = rc * rt + lax.broadcasted_iota(jnp.int32, (rt, 1), 0)
                acc = xp_ref[base - left:base - left + rt]
                for j in range(1, win):
                    acc = acc + xp_ref[base - left + j:base - left + j + rt]
                cnt = (jnp.minimum(t + right, seq - 1) - jnp.maximum(t - left, 0) + 1).astype(F32)
                p = acc / cnt - xp_ref[base:base + rt]
                y = jnp.dot(p.astype(BF16), w_ref[0], preferred_element_type=F32) + b_ref[...]
                rows = slice(rc * rt, (rc + 1) * rt)
                o_ref[rows] = h_ref[rows] + gate_ref[0] * (y * s_ref[...])


def _pool_mix(u, w, b, scale, h, mod, *, seq, row0, nseq, seg0):
    d = u.shape[1]
    cg = d // len(POOL_WINDOWS)
    blk0 = row0 // seq
    tile = pl.BlockSpec((seq, cg), lambda s, g: (blk0 + s, g))
    vec = pl.BlockSpec((1, cg), lambda s, g: (0, g))
    return pl.pallas_call(
        functools.partial(_pool_kernel, seq=seq),
        grid=(nseq, len(POOL_WINDOWS)),
        in_specs=[tile, pl.BlockSpec((1, cg, cg), lambda s, g: (g, 0, 0)), vec, vec, tile,
                  pl.BlockSpec((1, 1, cg), lambda s, g: ((seg0(s)) * 6 + 2, 0, g))],
        out_specs=tile,
        out_shape=jax.ShapeDtypeStruct(h.shape, F32),
        scratch_shapes=[pltpu.VMEM((seq + 2 * SUBLANES, cg), F32)],
        input_output_aliases={4: 0},
        compiler_params=_cparams("arbitrary", "arbitrary"),
        name="pool_mix",
    )(u, w, b.reshape(1, d), scale.reshape(1, d), h, mod)


def _na_lat_kernel(q_ref, k_ref, v_ref, qc_ref, kc_ref, vc_ref, cos_ref, sin_ref, bias_ref, o_ref, oc_ref,
                   q_s, k_s, *, rows, kh):
    hd = NA_HEAD_DIM
    lane = lax.broadcasted_iota(jnp.int32, (1, LANES), 1)
    low = (lane % 32) < 16
    head0 = lane < hd
    cos, sin = cos_ref[...], sin_ref[...]

    def rope(x):
        swapped = jnp.where(low, pltpu.roll(x, LANES - 16, axis=1), pltpu.roll(x, 16, axis=1))
        return x * cos + swapped * sin

    q_s[...] = (rope(q_ref[...].astype(F32)) * (hd ** -0.5)).astype(BF16)
    k_s[...] = rope(k_ref[...].astype(F32)).astype(BF16)
    nloc = kh * GRID_W
    rows_per_step = 8
    dot = lambda a, b: jnp.dot(a, b, preferred_element_type=F32)
    zero = jnp.zeros((), BF16)

    qc = qc_ref[...] * (hd ** -0.5)
    n_ctx = qc.shape[0]
    qcm = jnp.concatenate([jnp.where(head0, qc, zero), jnp.where(head0, zero, qc)], axis=0)
    sc = _nt(qcm, kc_ref[...])
    pc = jnp.exp(sc - jnp.max(sc, axis=-1, keepdims=True))
    oc = dot(pc.astype(BF16), vc_ref[...]) / jnp.sum(pc, axis=-1, keepdims=True)
    oc_ref[...] = jnp.where(head0, oc[:n_ctx], oc[n_ctx:]).astype(oc_ref.dtype)

    def body(step, carry):
        n = range(rows_per_step)
        r = [step * rows_per_step + dr for dr in n]
        rs = [jnp.clip(r[i] - kh // 2, 0, rows - kh) for i in n]
        q0 = [pl.multiple_of(r[i] * GRID_W, GRID_W) for i in n]
        k0 = [pl.multiple_of(rs[i] * GRID_W, GRID_W) for i in n]
        qr = [q_s[pl.ds(q0[i], GRID_W), :] for i in n]
        qm = [jnp.concatenate([jnp.where(head0, qr[i], zero), jnp.where(head0, zero, qr[i])], axis=0) for i in n]
        s_loc = [_nt(qm[i], k_s[pl.ds(k0[i], nloc), :]) + bias_ref[r[i] - rs[i]] for i in n]
        s_ctx = [_nt(qm[i], kc_ref[...]) for i in n]
        m = [jnp.maximum(jnp.max(s_loc[i], axis=-1, keepdims=True), jnp.max(s_ctx[i], axis=-1, keepdims=True))
             for i in n]
        p_loc = [jnp.exp(s_loc[i] - m[i]) for i in n]
        p_ctx = [jnp.exp(s_ctx[i] - m[i]) for i in n]
        den = [jnp.sum(p_loc[i], axis=-1, keepdims=True) + jnp.sum(p_ctx[i], axis=-1, keepdims=True) for i in n]
        o = [dot(p_loc[i].astype(BF16), v_ref[pl.ds(k0[i], nloc), :]) + dot(p_ctx[i].astype(BF16), vc_ref[...])
             for i in n]
        o = [o[i] / den[i] for i in n]
        for i in n:
            o_ref[pl.ds(q0[i], GRID_W), :] = jnp.where(head0, o[i][:GRID_W], o[i][GRID_W:]).astype(o_ref.dtype)
        return carry

    lax.fori_loop(0, _div(rows, rows_per_step), body, 0)


def _rope_tables(t_len, hd):
    half = hd // 2
    pos = jnp.arange(t_len)
    inv_freq = ROPE_BASE ** (-jnp.arange(0, half, 2, dtype=F32) / half)
    d = jnp.arange(hd)
    p = jnp.where(d[None, :] < half, (pos // GRID_W)[:, None], (pos % GRID_W)[:, None]).astype(F32)
    ang = p * inv_freq[d % (half // 2)][None, :]
    cos = jnp.cos(ang)
    sin = jnp.where((d % half) < half // 2, -jnp.sin(ang), jnp.sin(ang))
    return jnp.tile(cos, (1, 2)), jnp.tile(sin, (1, 2))


def _na_bias_table(rpb, kh):
    col = jnp.arange(GRID_W)
    col_start = jnp.clip(col - NA_KW // 2, 0, GRID_W - NA_KW)
    col_ok = (col[None, :] >= col_start[:, None]) & (col[None, :] < col_start[:, None] + NA_KW)
    dc = jnp.clip(col[None, :] - col[:, None], -(NA_KW - 1), NA_KW - 1) + NA_KW - 1
    rpb_cols = rpb[:, :, dc]
    dr = jnp.arange(kh)[None, :] - jnp.arange(kh)[:, None] + NA_KH - 1
    tab = rpb_cols[:, dr]
    tab = jnp.where(col_ok[None, None, None], tab, -1e30)
    heads = rpb.shape[0]
    tab = tab.transpose(0, 1, 3, 2, 4).reshape(heads // 2, 2, kh, GRID_W, kh * GRID_W)
    return tab.transpose(0, 2, 1, 3, 4).reshape(heads // 2, kh, 2 * GRID_W, kh * GRID_W)


def _neighbourhood_attention(qkv, rpb, *, bsz, t_len, l_len):
    d = qkv.shape[1] // 3
    hd = NA_HEAD_DIM
    heads = d // hd
    npair = d // LANES
    rows = t_len // GRID_W
    kh = min(NA_KH, rows)
    n_lat = bsz * t_len
    n_all = qkv.shape[0]
    ctx0 = _div(n_lat, l_len)
    cos, sin = _rope_tables(t_len, hd)
    bias = _na_bias_table(rpb, kh)
    lat_blk = lambda off: pl.BlockSpec((t_len, LANES), lambda b, p: (b, off + p))
    ctx_blk = lambda off: pl.BlockSpec((l_len, LANES), lambda b, p: (ctx0 + b, off + p))
    tab_blk = pl.BlockSpec((t_len, LANES), lambda b, p: (0, 0))
    o_lat, o_ctx = pl.pallas_call(
        functools.partial(_na_lat_kernel, rows=rows, kh=kh),
        grid=(bsz, npair),
        in_specs=[lat_blk(0), lat_blk(npair), lat_blk(2 * npair), ctx_blk(0), ctx_blk(npair), ctx_blk(2 * npair),
                  tab_blk, tab_blk,
                  pl.BlockSpec((None, kh, 2 * GRID_W, kh * GRID_W), lambda b, p: (p, 0, 0, 0))],
        out_specs=[lat_blk(0), pl.BlockSpec((l_len, LANES), lambda b, p: (b, p))],
        out_shape=[jax.ShapeDtypeStruct((n_lat, d), BF16), jax.ShapeDtypeStruct((n_all - n_lat, d), BF16)],
        scratch_shapes=[pltpu.VMEM((t_len, LANES), BF16), pltpu.VMEM((t_len, LANES), BF16)],
        compiler_params=_cparams("arbitrary", "arbitrary"),
        name="na_attention",
    )(qkv, qkv, qkv, qkv, qkv, qkv, cos, sin, bias)
    return jnp.concatenate([o_lat, o_ctx], axis=0)


def _sg_kernel(z_u_ref, z_v_ref, g_ref, ws_ref, bs_ref, o_ref, *, nchunk):
    zv = z_v_ref[...].astype(F32)
    zv = zv * lax.rsqrt(jnp.mean(zv * zv, axis=-1, keepdims=True) + NORM_EPS) * g_ref[...]
    zvb = zv.astype(BF16)
    for c in range(nchunk):
        rs = slice(c * SG_CHUNK, (c + 1) * SG_CHUNK)
        for g in range(SG_GROUPS):
            ls = slice(g * LANES, (g + 1) * LANES)
            mixed = jnp.dot(ws_ref[g], zvb[rs, ls], preferred_element_type=F32) + bs_ref[:, ls]
            o_ref[rs, ls] = (z_u_ref[rs, ls].astype(F32) * mixed).astype(o_ref.dtype)


def _spatial_gate(z, norm_g, w_s, b_s):
    m, two_w = z.shape
    width = two_w // 2
    nchunk = 2
    tm = nchunk * SG_CHUNK
    bs_full = jnp.repeat(b_s.T, width // SG_GROUPS, axis=1)
    return pl.pallas_call(
        functools.partial(_sg_kernel, nchunk=nchunk),
        grid=(_div(m, tm),),
        in_specs=[pl.BlockSpec((tm, width), lambda i: (i, 0)),
                  pl.BlockSpec((tm, width), lambda i: (i, 1)),
                  pl.BlockSpec((1, width), lambda i: (0, 0)),
                  pl.BlockSpec((SG_GROUPS, SG_CHUNK, SG_CHUNK), lambda i: (0, 0, 0)),
                  pl.BlockSpec((SG_CHUNK, width), lambda i: (0, 0))],
        out_specs=pl.BlockSpec((tm, width), lambda i: (i, 0)),
        out_shape=jax.ShapeDtypeStruct((m, width), BF16),
        compiler_params=_cparams("arbitrary"),
        name="spatial_gate",
    )(z, z, norm_g.reshape(1, width), w_s.astype(BF16), bs_full)


def _shift_kernel(h_ref, hp_ref, hn_ref, ng_ref, sh_ref, sc_ref, mu_ref, *o_refs, tm, n_lat_rows, seq_lat,
                  seq_ctx):
    i = pl.program_id(0)
    o_refs, u_s = o_refs[:-1], o_refs[-1]
    gain = ng_ref[...] * (1.0 + sc_ref[0])
    shift = sh_ref[0]
    pad = SUBLANES
    _norm_mod_rows(hp_ref, u_s, 0, pad, gain, shift)
    _norm_mod_rows(h_ref, u_s, pad, tm, gain, shift)
    _norm_mod_rows(hn_ref, u_s, pad + tm, pad, gain, shift)
    nmix = len(o_refs)
    step, width = 64, 2 * LANES
    for c0 in range(0, u_s.shape[1], width):
        cols = slice(c0, c0 + width)
        mu = [mu_ref[n:n + 1, cols] for n in range(2 * nmix)]
        for r0 in range(0, tm, step):
            row = lax.broadcasted_iota(jnp.int32, (step, 1), 0)
            grow = i * tm + r0 + row
            slen = jnp.where(grow < n_lat_rows, seq_lat, seq_ctx)
            pos = grow & (slen - 1)
            u = u_s[pad + r0:pad + r0 + step, cols]
            prev = jnp.where(pos == 0, 0.0, u_s[pad + r0 - 1:pad + r0 - 1 + step, cols]) - u
            nxt = jnp.where(pos == slen - 1, 0.0, u_s[pad + r0 + 1:pad + r0 + 1 + step, cols]) - u
            for n in range(nmix):
                o_refs[n][r0:r0 + step, cols] = (u + prev * mu[n] + nxt * mu[nmix + n]).astype(BF16)


def _token_shift(h, norm_g, mod, mu, *, nseg, n_lat_rows, seq_lat, seq_ctx):
    m, d = h.shape
    nmix = mu.shape[1]
    tm = 256
    nb = tm // SUBLANES
    tile = pl.BlockSpec((tm, d), lambda i: (i, 0))
    prev = pl.BlockSpec((SUBLANES, d), lambda i: (jnp.maximum(i * nb - 1, 0), 0))
    nxt = pl.BlockSpec((SUBLANES, d), lambda i: (jnp.minimum((i + 1) * nb, m // SUBLANES - 1), 0))
    seg = lambda i: jnp.minimum((i * tm) // seq_lat, nseg)
    return pl.pallas_call(
        functools.partial(_shift_kernel, tm=tm, n_lat_rows=n_lat_rows, seq_lat=seq_lat, seq_ctx=seq_ctx),
        grid=(_div(m, tm),),
        in_specs=[tile, prev, nxt, pl.BlockSpec((1, d), lambda i: (0, 0)),
                  pl.BlockSpec((1, 1, d), lambda i: (seg(i) * 6 + 0, 0, 0)),
                  pl.BlockSpec((1, 1, d), lambda i: (seg(i) * 6 + 1, 0, 0)),
                  pl.BlockSpec((2 * nmix, d), lambda i: (0, 0))],
        out_specs=[tile] * nmix,
        out_shape=[jax.ShapeDtypeStruct((m, d), BF16)] * nmix,
        scratch_shapes=[pltpu.VMEM((tm + 2 * SUBLANES, d), F32)],
        compiler_params=_cparams("arbitrary"),
        name="token_shift",
    )(h, h, h, norm_g.reshape(1, d), mod, mod, mu.reshape(2 * nmix, d))


def _split2(x):
    hi = x.astype(BF16)
    lo = (x - hi.astype(F32)).astype(BF16)
    return hi, lo


def _chunk_chains(dirs, v_ref, *, nheads):
    c = SCAN_CHUNK
    hd = RW_HEAD_DIM
    ri = lax.broadcasted_iota(jnp.int32, (c, c), 0)
    ci = lax.broadcasted_iota(jnp.int32, (c, c), 1)
    strict_d = [(ci > ri) if d[-1] else (ci < ri) for d in dirs]
    incl_d = [(ci >= ri) if d[-1] else (ci <= ri) for d in dirs]
    diag_blk = (ri // SOLVE_BLOCK) == (ci // SOLVE_BLOCK)
    eye = jnp.where(ri == ci, 1.0, 0.0)
    chains = [(di, half, hh) for di in range(len(dirs)) for half in range(LANES // c) for hh in range(nheads)]
    src = [dirs[di] for di, _, _ in chains]
    strict = [strict_d[di] for di, _, _ in chains]
    incl = [incl_d[di] for di, _, _ in chains]
    rows = [slice(half * c, (half + 1) * c) for _, half, _ in chains]
    ls = [slice(hh * hd, (hh + 1) * hd) for _, _, hh in chains]
    pair = [hh // 2 for _, _, hh in chains]
    sub = [slice((hh % 2) * hd, (hh % 2 + 1) * hd) for _, _, hh in chains]
    n = range(len(chains))
    dot = lambda a, b: jnp.dot(a, b, preferred_element_type=F32)
    kt = [src[i][0][rows[i], ls[i]] for i in n]
    rt = [src[i][1][rows[i], ls[i]] for i in n]
    vv = [v_ref[rows[i], ls[i]] for i in n]
    ktrt = [jnp.concatenate([kt[i], rt[i]], axis=0) for i in n]
    pb = [_nt(ktrt[i], src[i][2][rows[i], ls[i]]) for i in n]
    pk = [_nt(ktrt[i], src[i][3][rows[i], ls[i]]) for i in n]
    a1 = [jnp.where(strict[i], pb[i][:c], 0.0) for i in n]
    a3 = [jnp.where(incl[i], pb[i][c:], 0.0).astype(BF16) for i in n]
    ad = [jnp.where(diag_blk, a1[i], 0.0) for i in n]
    ao = [(a1[i] - ad[i]).astype(BF16) for i in n]
    tinv = [eye - ad[i] for i in n]
    pw = [_mm(ad[i], ad[i]) for i in n]
    lhs_v = [jnp.concatenate([jnp.where(strict[i], pk[i][:c], 0.0).astype(BF16),
                              jnp.where(incl[i], pk[i][c:], 0.0).astype(BF16),
                              src[i][5][pair[i], sub[i], rows[i]]], axis=0) for i in n]
    pv = [dot(lhs_v[i], vv[i]) for i in n]
    tinv = [tinv[i] + _mm(tinv[i], pw[i]) for i in n]
    for _ in range(int(math.log2(SOLVE_BLOCK)) - 2):
        pw = [_mm(pw[i], pw[i]) for i in n]
        tinv = [tinv[i] + _mm(tinv[i], pw[i]) for i in n]
    rhs = [jnp.concatenate([kt[i].astype(F32), pv[i][:c]], axis=1).astype(BF16) for i in n]
    tinv = [t.astype(BF16) for t in tinv]
    nmat = [dot(tinv[i], ao[i]).astype(BF16) for i in n]
    x1 = [dot(tinv[i], rhs[i]) for i in n]
    x = x1
    for _ in range(c // SOLVE_BLOCK - 1):
        x = [x1[i] - dot(nmat[i], x[i].astype(BF16)) for i in n]
    lhs_x = [jnp.concatenate([src[i][4][pair[i], sub[i], rows[i]], a3[i]], axis=0) for i in n]
    px = [dot(lhs_x[i], x[i].astype(BF16)) for i in n]
    for i, (_, half, hh) in enumerate(chains):
        m_o, c_o, rp_o, y0_o = src[i][6]
        m_o[hh, 0, :, rows[i]] = (-px[i][:c, :hd]).astype(m_o.dtype)
        c_o[hh, 0, :, rows[i]] = pv[i][2 * c:] - px[i][:c, hd:]
        rp_o[rows[i], ls[i]] = (rt[i].astype(F32) - px[i][c:, :hd]).astype(rp_o.dtype)
        y0_o[rows[i], ls[i]] = pv[i][c:2 * c] - px[i][c:, hd:]


def _rw_chunk_kernel(k_ref, r_ref, v_ref, hw_ref, ha_ref, w2_ref, a2_ref, w0_ref, a0_ref, kk_ref, ka_ref,
                     rk_ref, tri_ref, bc_o, mf_o, cf_o, rpf_o, y0f_o, gf_o, mr_o, cr_o, rpr_o, y0r_o, gr_o,
                     kt_s, rt_s, bh_s, kh_s, v_s, bt_s, kbt_s, *, lora, nheads):
    npair = nheads // 2
    lane = lax.broadcasted_iota(jnp.int32, (1, LANES), 1)
    head0 = lane < RW_HEAD_DIM
    pairs = [slice(p * LANES, (p + 1) * LANES) for p in range(npair)]
    head_sum = lambda x: jnp.concatenate([_head_sum(x[:, ps], head0) for ps in pairs], axis=1)
    k = k_ref[...]
    r = r_ref[...]
    v_s[...] = v_ref[...].astype(BF16)
    kkv = k * kk_ref[...]
    kk = kkv * lax.rsqrt(jnp.maximum(head_sum(kkv * kkv), 1e-12))
    ka = ka_ref[...]
    rk = rk_ref[...]
    hw = hw_ref[...]
    ha = ha_ref[...]
    log_decay, beta, key = [], [], []
    for d in range(2):
        ls = slice(d * lora, (d + 1) * lora)
        pre = w0_ref[d] + jnp.dot(hw[:, ls], w2_ref[d], preferred_element_type=F32)
        z = -pre
        softplus = jnp.maximum(z, 0.0) + jnp.log(1.0 + jnp.exp(-jnp.abs(z)))
        log_decay.append(-jnp.exp(-softplus - 0.5))
        a = jax.nn.sigmoid(a0_ref[d] + jnp.dot(ha[:, ls], a2_ref[d], preferred_element_type=F32))
        beta.append(kk * a)
        key.append(k * (1.0 + (a - 1.0) * ka))
    bc_o[...] = head_sum(r * key[0] * rk) + head_sum(r * key[1] * rk)
    g_o = (gf_o, gr_o)
    for d in range(2):
        ld = log_decay[d]
        pieces = _split2(ld)
        csum = [sum(jnp.dot(tri_ref[e], p, preferred_element_type=F32) for p in pieces) for e in range(2)]
        cum = csum[d]
        total = csum[0] + csum[1] - ld
        kt_s[d] = (kk * jnp.exp(cum - ld)).astype(BF16)
        rt_s[d] = (r * jnp.exp(cum)).astype(BF16)
        inv = jnp.exp(-cum)
        bh_s[d] = (beta[d] * inv).astype(BF16)
        kh_s[d] = (key[d] * inv).astype(BF16)
        tail = jnp.exp(total - cum)
        bbar = beta[d] * tail
        kbar = key[d] * tail
        gdec = jnp.exp(total)
        for p, ps in enumerate(pairs):
            bt_s[d, p] = jnp.transpose(bbar[:, ps]).astype(BF16)
            kbt_s[d, p] = jnp.transpose(kbar[:, ps]).astype(BF16)
            g_o[d][p, 0] = jnp.transpose(gdec[:, ps])
    outs = ((mf_o, cf_o, rpf_o, y0f_o), (mr_o, cr_o, rpr_o, y0r_o))
    _chunk_chains([(kt_s.at[d], rt_s.at[d], bh_s.at[d], kh_s.at[d], bt_s.at[d], kbt_s.at[d], outs[d], d == 1)
                   for d in range(2)], v_s, nheads=nheads)


def _rw_chunks(k, r, v, hw, ha, w2, a2, w0, a0, k_k, k_a, r_k, *, bsz, t_len, l_len, lora):
    m, d = k.shape
    hd = RW_HEAD_DIM
    nheads = 8
    width = nheads * hd
    npair = nheads // 2
    tt = t_len + l_len
    ng = _div(tt, LANES)
    ctx_groups = _div(l_len, LANES)
    ctx0 = _div(bsz * t_len, LANES)

    def in_blk(b, g):
        return jnp.where(g < ctx_groups, ctx0 + b * ctx_groups + g,
                         b * (t_len // LANES) + g - ctx_groups)

    ri = jnp.arange(LANES)[:, None]
    ci = jnp.arange(LANES)[None, :]
    same = (ri // SCAN_CHUNK) == (ci // SCAN_CHUNK)
    tri = jnp.stack([same & (ci <= ri), same & (ci >= ri)]).astype(BF16)
    tile = pl.BlockSpec((LANES, width), lambda b, p, g: (in_blk(b, g), p))
    lora_blk = pl.BlockSpec((LANES, 2 * lora), lambda b, p, g: (in_blk(b, g), 0))
    wl_blk = pl.BlockSpec((2, lora, width), lambda b, p, g: (0, 0, p))
    v2_blk = pl.BlockSpec((2, 1, width), lambda b, p, g: (0, 0, p))
    v1_blk = pl.BlockSpec((1, width), lambda b, p, g: (0, p))
    tri_blk = pl.BlockSpec((2, LANES, LANES), lambda b, p, g: (0, 0, 0))
    row_o = pl.BlockSpec((None, LANES, width), lambda b, p, g: (b, g, p))
    sq_o = pl.BlockSpec((None, nheads, 1, hd, LANES), lambda b, p, g: (b, p, g, 0, 0))
    tr_o = pl.BlockSpec((None, npair, 1, LANES, LANES), lambda b, p, g: (b, p, g, 0, 0))
    row_shape = lambda dt: jax.ShapeDtypeStruct((bsz, tt, d), dt)
    sq_shape = lambda dt: jax.ShapeDtypeStruct((bsz, d // hd, ng, hd, LANES), dt)
    tr_shape = jax.ShapeDtypeStruct((bsz, d // LANES, ng, LANES, LANES), F32)
    dir_specs = [sq_o, sq_o, row_o, row_o, tr_o]
    dir_shapes = [sq_shape(BF16), sq_shape(F32), row_shape(BF16), row_shape(F32), tr_shape]
    operand = lambda: pltpu.VMEM((2, LANES, width), BF16)
    transposed = lambda: pltpu.VMEM((2, npair, LANES, LANES), BF16)
    out = pl.pallas_call(
        functools.partial(_rw_chunk_kernel, lora=lora, nheads=nheads),
        grid=(bsz, _div(d, width), ng),
        in_specs=[tile, tile, tile, lora_blk, lora_blk, wl_blk, wl_blk, v2_blk, v2_blk, v1_blk, v1_blk, v1_blk,
                  tri_blk],
        out_specs=[row_o] + dir_specs + dir_specs,
        out_shape=[row_shape(F32)] + dir_shapes + dir_shapes,
        scratch_shapes=[operand(), operand(), operand(), operand(), pltpu.VMEM((LANES, width), BF16),
                        transposed(), transposed()],
        compiler_params=_cparams("arbitrary", "arbitrary", "arbitrary"),
        name="rwkv_chunks",
    )(k, r, v, hw, ha, w2, a2, w0.reshape(2, 1, d), a0.reshape(2, 1, d), k_k.reshape(1, d),
      k_a.reshape(1, d), r_k.reshape(1, d), tri)
    return out[0], out[1:6], out[6:11]


def _state_pass_kernel(mf_ref, cf_ref, gf_ref, mr_ref, cr_ref, gr_ref, sf_o, sr_o, *, ctx_groups, n_groups,
                       nheads):
    c = SCAN_CHUNK
    hd = RW_HEAD_DIM
    heads = range(nheads)
    sf_o[...] = jnp.zeros(sf_o.shape, sf_o.dtype)
    sr_o[...] = jnp.zeros(sr_o.shape, sr_o.dtype)
    dirs = ((mf_ref, cf_ref, gf_ref, sf_o, False), (mr_ref, cr_ref, gr_ref, sr_o, True))

    def group(i, carry):
        out = []
        g_of = [i, jnp.where(i < ctx_groups, ctx_groups - 1 - i, n_groups - 1 - (i - ctx_groups))]
        states = [list(carry[0]), list(carry[1])]
        for step in range(LANES // c):
            s0b, ms = [[], []], [[], []]
            for di, (m_ref, c_ref, g_ref, s_o, rev) in enumerate(dirs):
                half = (LANES // c - 1 - step) if rev else step
                ts = slice(half * c, (half + 1) * c)
                for hh in heads:
                    sb = states[di][hh].astype(BF16)
                    s0b[di].append(sb)
                    blk = slice((hh % 2) * hd, (hh % 2 + 1) * hd)
                    s_o[hh // 2, g_of[di] * (LANES // c) + half, blk, blk] = sb
            for di, (m_ref, c_ref, g_ref, s_o, rev) in enumerate(dirs):
                half = (LANES // c - 1 - step) if rev else step
                ts = slice(half * c, (half + 1) * c)
                ms[di] = [jnp.dot(m_ref[hh, g_of[di], :, ts], s0b[di][hh], preferred_element_type=F32)
                          for hh in heads]
            for di, (m_ref, c_ref, g_ref, s_o, rev) in enumerate(dirs):
                half = (LANES // c - 1 - step) if rev else step
                ts = slice(half * c, (half + 1) * c)
                for hh in heads:
                    blk = slice((hh % 2) * hd, (hh % 2 + 1) * hd)
                    gc = g_ref[hh // 2, g_of[di], blk, half * c:half * c + 1]
                    states[di][hh] = gc * states[di][hh] + ms[di][hh] + c_ref[hh, g_of[di], :, ts]
        return tuple(states[0]), tuple(states[1])

    zero = tuple(jnp.zeros((hd, hd), F32) for _ in heads)
    lax.fori_loop(0, n_groups, group, (zero, zero))


def _state_pass(m_f, c_f, g_f, m_r, c_r, g_r, *, l_len):
    bsz, heads, ng, hd, _ = m_f.shape
    nheads = 4
    npair = nheads // 2
    nc = ng * (LANES // SCAN_CHUNK)
    sq = pl.BlockSpec((None, nheads, ng, hd, LANES), lambda b, p: (b, p, 0, 0, 0))
    tr = pl.BlockSpec((None, npair, ng, LANES, LANES), lambda b, p: (b, p, 0, 0, 0))
    st = pl.BlockSpec((None, npair, nc, LANES, LANES), lambda b, p: (b, p, 0, 0, 0))
    st_shape = jax.ShapeDtypeStruct((bsz, heads // 2, nc, LANES, LANES), BF16)
    return pl.pallas_call(
        functools.partial(_state_pass_kernel, ctx_groups=l_len // LANES, n_groups=ng, nheads=nheads),
        grid=(bsz, _div(heads, nheads)),
        in_specs=[sq, sq, tr, sq, sq, tr],
        out_specs=[st, st],
        out_shape=[st_shape, st_shape],
        compiler_params=_cparams("arbitrary", "arbitrary"),
        name="rwkv_state",
    )(m_f, c_f, g_f, m_r, c_r, g_r)


def _readout_kernel(rpf_ref, y0f_ref, sf_ref, rpr_ref, y0r_ref, sr_ref, bc_ref, v_ref, g_ref, lng_ref,
                    lnb_ref, o_ref):
    c = SCAN_CHUNK
    lane = lax.broadcasted_iota(jnp.int32, (1, LANES), 1)
    head0 = lane < RW_HEAD_DIM
    inv_n = 1.0 / RW_HEAD_DIM
    tiles = [(cc, pp) for cc in range(o_ref.shape[0] // c) for pp in range(o_ref.shape[1] // LANES)]
    rows = [slice(cc * c, (cc + 1) * c) for cc, _ in tiles]
    ls = [slice(pp * LANES, (pp + 1) * LANES) for _, pp in tiles]
    n = range(len(tiles))
    dot = lambda a, b: jnp.dot(a, b, preferred_element_type=F32)
    yf = [dot(rpf_ref[rows[i], ls[i]], sf_ref[tiles[i][1], tiles[i][0]]) for i in n]
    yr = [dot(rpr_ref[rows[i], ls[i]], sr_ref[tiles[i][1], tiles[i][0]]) for i in n]
    for i in n:
        y = (yf[i] + y0f_ref[rows[i], ls[i]]) + (yr[i] + y0r_ref[rows[i], ls[i]])
        mean = _head_sum(y, head0) * inv_n
        yc = y - mean
        var = _head_sum(yc * yc, head0) * inv_n
        yn = yc * lax.rsqrt(var + RW_GN_EPS) * lng_ref[:, ls[i]] + lnb_ref[:, ls[i]]
        o_ref[rows[i], ls[i]] = ((yn + bc_ref[rows[i], ls[i]] * v_ref[rows[i], ls[i]])
                                 * g_ref[rows[i], ls[i]]).astype(o_ref.dtype)


def _rw_readout(rp_f, y0_f, s_f, rp_r, y0_r, s_r, bc, v, g, ln_g, ln_b, *, bsz, t_len, l_len):
    d = v.shape[1]
    tm, tn = 256, 512
    per_b = t_len // tm
    off = l_len // tm
    blk = lambda i: (i // per_b, off + i % per_b)
    scan_blk = pl.BlockSpec((None, tm, tn), lambda i, p: (*blk(i), p))
    st_blk = pl.BlockSpec((None, tn // LANES, tm // SCAN_CHUNK, LANES, LANES),
                          lambda i, p: (blk(i)[0], p, blk(i)[1], 0, 0))
    tile = pl.BlockSpec((tm, tn), lambda i, p: (i, p))
    vec = pl.BlockSpec((1, tn), lambda i, p: (0, p))
    return pl.pallas_call(
        _readout_kernel,
        grid=(_div(bsz * t_len, tm), _div(d, tn)),
        in_specs=[scan_blk, scan_blk, st_blk, scan_blk, scan_blk, st_blk, scan_blk, tile, tile, vec, vec],
        out_specs=tile,
        out_shape=jax.ShapeDtypeStruct((bsz * t_len, d), BF16),
        compiler_params=_cparams("arbitrary", "arbitrary"),
        name="rwkv_readout",
    )(rp_f, y0_f, s_f, rp_r, y0_r, s_r, bc, v, g, ln_g.reshape(1, d), ln_b.reshape(1, d))


def _pad_cols(w, n):
    return jnp.pad(w, ((0, 0), (0, n - w.shape[1])))


def _pad_rows(w, n):
    return jnp.pad(w, ((0, n - w.shape[0]), (0, 0)))


def kernel(x, c, ctx, c_ctx, norm1_g, norm2_g, w_mod, b_mod, ffn_w_gate, ffn_w_up, ffn_conv_w, ffn_conv_b, ffn_w_down, final_norm_g, pool_w, pool_b, pool_scale, na_w_qkv, na_rpb, na_w_o, sg_w_in, sg_b_in, sg_norm_g, sg_w_s, sg_b_s, sg_w_o, rw_mu, rw_w_rkv, rw_w0, rw_w1, rw_w2, rw_a0, rw_a1, rw_a2, rw_g1, rw_g2, rw_k_k, rw_k_a, rw_r_k, rw_ln_g, rw_ln_b, rw_w_o):
    bsz, t_len, d = x.shape
    l_len = ctx.shape[1]
    depth = norm1_g.shape[0]
    n_mixers = 4
    n_lat = bsz * t_len
    n_all = n_lat + bsz * l_len
    f = ffn_w_gate.shape[2]
    f_pad = -(-f // 512) * 512
    wg_all = _cast_pad_cols(ffn_w_gate, f_pad)
    wu_all = _cast_pad_cols(ffn_w_up, f_pad)
    wd_all = ffn_w_down.astype(BF16)

    cvec = jnp.concatenate([c, c_ctx[None], jnp.zeros((SUBLANES - bsz - 1, d), F32)], axis=0)
    mods = _modulation(cvec, w_mod, b_mod)
    h = _pack_rows(x, ctx)
    seg = dict(seq=t_len, nseg=bsz)
    edges = dict(n_lat_rows=n_lat, seq_lat=t_len, seq_ctx=l_len)

    for i in range(depth):
        m_kind, j = i % n_mixers, i // n_mixers
        last = i == depth - 1
        mod = mods[i].reshape(SUBLANES * 6, 1, d)
        rows = n_lat if last else n_all
        if m_kind == 0:
            u = _norm(h, norm1_g[i], mod, 0, 1, t_len, bsz, F32, rows=rows)
            pw = pool_w[j].astype(BF16)
            h = _pool_mix(u, pw, pool_b[j], pool_scale[j], h, mod, seq=t_len, row0=0, nseq=bsz,
                          seg0=lambda s: s)
            if not last:
                h = _pool_mix(u, pw, pool_b[j], pool_scale[j], h, mod, seq=l_len, row0=n_lat, nseq=bsz,
                              seg0=lambda s: bsz)
        elif m_kind == 1:
            qkv = _matmul(h, na_w_qkv[j].astype(BF16), out_dtype=BF16, norm=(norm1_g[i], mod, 0, 1), **seg)
            o = _neighbourhood_attention(qkv, na_rpb[j], bsz=bsz, t_len=t_len, l_len=l_len)
            h = _matmul(o, na_w_o[j].astype(BF16), res=h, mod=mod, k_gate=2, rows=rows, **seg)
        elif m_kind == 2:
            z = _matmul(h, sg_w_in[j].astype(BF16), bias=sg_b_in[j], act="gelu", out_dtype=BF16, rows=rows,
                        norm=(norm1_g[i], mod, 0, 1), **seg)
            gated = _spatial_gate(z, sg_norm_g[j], sg_w_s[j], sg_b_s[j])
            h = _matmul(gated, sg_w_o[j].astype(BF16), res=h, mod=mod, k_gate=2, rows=rows, **seg)
        else:
            lora = LANES
            xr, xw, xk, xv, xa, xg = _token_shift(h, norm1_g[i], mod, rw_mu[j], nseg=bsz, **edges)
            w_rkv = rw_w_rkv.astype(BF16)
            r = _matmul(xr, w_rkv, w_lead=(j, 0))
            k = _matmul(xk, w_rkv, w_lead=(j, 1))
            v = _matmul(xv, w_rkv, w_lead=(j, 2))
            w1 = jnp.concatenate([_pad_cols(rw_w1[j, e], lora) for e in range(2)], axis=1).astype(BF16)
            a1 = jnp.concatenate([_pad_cols(rw_a1[j, e], lora) for e in range(2)], axis=1).astype(BF16)
            w2 = jnp.stack([_pad_rows(rw_w2[j, e], lora) for e in range(2)]).astype(BF16)
            a2 = jnp.stack([_pad_rows(rw_a2[j, e], lora) for e in range(2)]).astype(BF16)
            hw = _matmul(xw, w1, act="tanh", out_dtype=BF16)
            ha = _matmul(xa, a1, out_dtype=BF16)
            hg = _matmul(xg, rw_g1[j].astype(BF16), act="sigmoid", out_dtype=BF16, rows=n_lat)
            g = _matmul(hg, rw_g2[j].astype(BF16))
            bc, (m_f, c_f, rp_f, y0_f, g_f), (m_r, c_r, rp_r, y0_r, g_r) = _rw_chunks(
                k, r, v, hw, ha, w2, a2, rw_w0[j], rw_a0[j], rw_k_k[j], rw_k_a[j], rw_r_k[j],
                bsz=bsz, t_len=t_len, l_len=l_len, lora=lora)
            s_f, s_r = _state_pass(m_f, c_f, g_f, m_r, c_r, g_r, l_len=l_len)
            o = _rw_readout(rp_f, y0_f, s_f, rp_r, y0_r, s_r, bc, v, g, rw_ln_g[j], rw_ln_b[j],
                            bsz=bsz, t_len=t_len, l_len=l_len)
            h = _matmul(o, rw_w_o[j].astype(BF16), res=h, mod=mod, k_gate=2, rows=n_lat, **seg)
        cw = _pad_cols(ffn_conv_w[i], f_pad)
        cb = _pad_cols(ffn_conv_b[i][None], f_pad)
        mid = _ffn1(h, norm2_g[i], mod, wg_all, wu_all, cw, cb, i, rows=rows, nseg=bsz, **edges)
        h = _matmul(mid, wd_all, w_lead=(i,), res=h, mod=mod, k_gate=5, rows=rows, tm=1024, **seg)

    return _norm(h, final_norm_g, None, 0, 0, t_len, bsz, F32, rows=n_lat, batched_out=True)
```

```python
import functools
import math

import jax
import jax.numpy as jnp
from jax import lax
from jax.experimental import pallas as pl
from jax.experimental.pallas import tpu as pltpu

F32 = jnp.float32
BF16 = jnp.bfloat16

GRID_W = 64
NORM_EPS = 1e-6
POOL_WINDOWS = (2, 4, 8, 16)
NA_HEAD_DIM = 64
NA_KH = 8
NA_KW = 16
ROPE_BASE = 10000.0
SG_CHUNK = 128
SG_GROUPS = 16
RW_HEAD_DIM = 64
RW_GN_EPS = 64e-5

LANES = 128
SUBLANES = 8
BF16_ROWS = 16
VMEM_LIMIT = 56 * 1024 * 1024

SCAN_CHUNK = 64
SOLVE_BLOCK = 16


def _cparams(*sem):
    return pltpu.CompilerParams(dimension_semantics=sem, vmem_limit_bytes=VMEM_LIMIT)


def _div(a, b):
    assert a % b == 0, (a, b)
    return a // b


def _nt(a, b):
    return lax.dot_general(a, b, (((1,), (1,)), ((), ())), preferred_element_type=F32)


def _mm(a, b):
    return jnp.dot(a.astype(BF16), b.astype(BF16), preferred_element_type=F32)


def _silu(x):
    return x * jax.nn.sigmoid(x)


def _head_sum(x, lane_is_head0):
    s0 = jnp.sum(jnp.where(lane_is_head0, x, 0.0), axis=-1, keepdims=True)
    s1 = jnp.sum(jnp.where(lane_is_head0, 0.0, x), axis=-1, keepdims=True)
    return jnp.where(lane_is_head0, s0, s1)


def _seq_edges(i, tm, n_lat_rows, seq_lat, seq_ctx):
    row = lax.broadcasted_iota(jnp.int32, (tm, 1), 0)
    grow = i * tm + row
    slen = jnp.where(grow < n_lat_rows, seq_lat, seq_ctx)
    pos = grow & (slen - 1)
    return row, pos == 0, pos == slen - 1


def _shift_rows(x, halo_prev, halo_next, row, first, last, tm):
    xp = pltpu.roll(x, 1, axis=0)
    xp = jnp.where(row == 0, halo_prev[SUBLANES - 1:SUBLANES], xp)
    xp = jnp.where(first, 0.0, xp)
    xn = pltpu.roll(x, tm - 1, axis=0)
    xn = jnp.where(row == tm - 1, halo_next[0:1], xn)
    xn = jnp.where(last, 0.0, xn)
    return xp, xn


def _shift_rows_tile(x, halo_prev, halo_next, i, tm, n_lat_rows, seq_lat, seq_ctx, interior):
    r0 = i * tm
    slen = jnp.where(r0 < n_lat_rows, seq_lat, seq_ctx)
    starts = (r0 & (slen - 1)) == 0
    ends = ((r0 + tm) & (slen - 1)) == 0
    row8 = lax.broadcasted_iota(jnp.int32, (SUBLANES, 1), 0)
    xp = pltpu.roll(x, 1, axis=0)
    top = jnp.where(row8 == 0, jnp.where(starts, 0.0, halo_prev[SUBLANES - 1:SUBLANES]), xp[:SUBLANES])
    xp = jnp.concatenate([top, xp[SUBLANES:]], axis=0)
    xn = pltpu.roll(x, tm - 1, axis=0)
    bot = jnp.where(row8 == SUBLANES - 1, jnp.where(ends, 0.0, halo_next[0:1]), xn[tm - SUBLANES:])
    xn = jnp.concatenate([xn[:tm - SUBLANES], bot], axis=0)
    if interior:
        _, first, last = _seq_edges(i, tm, n_lat_rows, seq_lat, seq_ctx)
        xp = jnp.where(first, 0.0, xp)
        xn = jnp.where(last, 0.0, xn)
    return xp, xn


def _mod_kernel(c_ref, w_ref, b_ref, o_ref):
    s = _silu(c_ref[...]).astype(BF16)
    o_ref[0] = jnp.dot(s, w_ref[0].astype(BF16), preferred_element_type=F32) + b_ref[0]


def _modulation(cvec, w_mod, b_mod):
    depth, d, n = w_mod.shape
    tn = 1024
    return pl.pallas_call(
        _mod_kernel,
        grid=(depth, _div(n, tn)),
        in_specs=[pl.BlockSpec((SUBLANES, d), lambda l, j: (0, 0)),
                  pl.BlockSpec((1, d, tn), lambda l, j: (l, 0, j)),
                  pl.BlockSpec((1, 1, tn), lambda l, j: (l, 0, j))],
        out_specs=pl.BlockSpec((1, SUBLANES, tn), lambda l, j: (l, 0, j)),
        out_shape=jax.ShapeDtypeStruct((depth, SUBLANES, n), F32),
        compiler_params=_cparams("arbitrary", "arbitrary"),
        name="modulation",
    )(cvec, w_mod, b_mod.reshape(depth, 1, n))


def _norm_kernel(*refs, modulate):
    if modulate:
        h_ref, g_ref, sh_ref, sc_ref, o_ref = refs
    else:
        h_ref, g_ref, o_ref = refs
    x = h_ref[...]
    y = x * lax.rsqrt(jnp.mean(x * x, axis=-1, keepdims=True) + NORM_EPS) * g_ref[...]
    if modulate:
        y = y * (1.0 + sc_ref[0]) + sh_ref[0]
    o_ref[...] = y.astype(o_ref.dtype)


def _norm(h, g, mod, k_shift, k_scale, seq, nseg, out_dtype, rows=None, batched_out=False):
    m, d = h.shape if rows is None else (rows, h.shape[1])
    tm = 1024
    row_spec = pl.BlockSpec((tm, d), lambda i: (i, 0))
    out_spec, out_shape = row_spec, (m, d)
    if batched_out:
        per_seq = _div(seq, tm)
        out_spec = pl.BlockSpec((None, tm, d), lambda i: (i // per_seq, i % per_seq, 0))
        out_shape = (_div(m, seq), seq, d)
    in_specs = [row_spec, pl.BlockSpec((1, d), lambda i: (0, 0))]
    args = [h, g.reshape(1, d)]
    if mod is not None:
        seg = lambda i: jnp.minimum((i * tm) // seq, nseg)
        in_specs += [pl.BlockSpec((1, 1, d), lambda i: (seg(i) * 6 + k_shift, 0, 0)),
                     pl.BlockSpec((1, 1, d), lambda i: (seg(i) * 6 + k_scale, 0, 0))]
        args += [mod, mod]
    return pl.pallas_call(
        functools.partial(_norm_kernel, modulate=mod is not None),
        grid=(_div(m, tm),),
        in_specs=in_specs,
        out_specs=out_spec,
        out_shape=jax.ShapeDtypeStruct(out_shape, out_dtype),
        compiler_params=_cparams("arbitrary"),
        name="rmsnorm_mod",
    )(*args)


def _pack_rows_kernel(x_ref, c_ref, o_ref, *, lat_tiles):
    i = pl.program_id(0)

    @pl.when(i < lat_tiles)
    def _():
        o_ref[...] = x_ref[...]

    @pl.when(i >= lat_tiles)
    def _():
        o_ref[...] = c_ref[...]


def _pack_rows(x, ctx):
    bsz, t_len, d = x.shape
    n_ctx = bsz * ctx.shape[1]
    tm = 1024
    per_seq = _div(t_len, tm)
    lat_tiles = bsz * per_seq
    lat = lambda i: jnp.minimum(i, lat_tiles - 1)
    return pl.pallas_call(
        functools.partial(_pack_rows_kernel, lat_tiles=lat_tiles),
        grid=(lat_tiles + _div(n_ctx, tm),),
        in_specs=[pl.BlockSpec((None, tm, d), lambda i: (lat(i) // per_seq, lat(i) % per_seq, 0)),
                  pl.BlockSpec((tm, d), lambda i: (jnp.maximum(i - lat_tiles, 0), 0))],
        out_specs=pl.BlockSpec((tm, d), lambda i: (i, 0)),
        out_shape=jax.ShapeDtypeStruct((bsz * t_len + n_ctx, d), x.dtype),
        compiler_params=_cparams("arbitrary"),
        name="pack_rows",
    )(x, ctx.reshape(n_ctx, d))


def _norm_mod_rows(src_ref, dst_ref, dst0, nrows, gain, shift, step=128):
    for r0 in range(0, nrows, step):
        x = src_ref[r0:r0 + min(step, nrows)]
        y = x * lax.rsqrt(jnp.mean(x * x, axis=-1, keepdims=True) + NORM_EPS)
        dst_ref[dst0 + r0:dst0 + r0 + x.shape[0]] = (y * gain + shift).astype(dst_ref.dtype)


def _matmul_kernel(*refs, act, has_bias, has_res, has_norm):
    x_ref, w_ref = refs[0], refs[1]
    k = 2
    if has_norm:
        ng_ref, sh_ref, sc_ref = refs[k:k + 3]
        k += 3
        u_s = refs[-1]

        @pl.when(pl.program_id(1) == 0)
        def _():
            _norm_mod_rows(x_ref, u_s, 0, x_ref.shape[0], ng_ref[...] * (1.0 + sc_ref[0]), sh_ref[0])
        x_ref = u_s
    acc = jnp.dot(x_ref[:, :w_ref.shape[0]], w_ref[...], preferred_element_type=F32)
    if has_bias:
        acc = acc + refs[k][...]
        k += 1
    if act == "gelu":
        acc = jax.nn.gelu(acc, approximate=True)
    elif act == "tanh":
        acc = jnp.tanh(acc)
    elif act == "sigmoid":
        acc = jax.nn.sigmoid(acc)
    if has_res:
        acc = refs[k][...] + refs[k + 1][0] * acc
        k += 2
    o_ref = refs[k]
    o_ref[...] = acc.astype(o_ref.dtype)


def _matmul(x, w, *, bias=None, act=None, res=None, mod=None, k_gate=None, seq=None, nseg=None,
            out_dtype=F32, tm=1024, tn=512, rows=None, w_lead=(), norm=None):
    m = x.shape[0] if rows is None else rows
    kdim, n = w.shape[-2:]
    tn = min(tn, n)
    tm = min(tm, m)
    assert x.shape[1] >= kdim and (x.shape[1] == kdim or kdim % LANES == 0)
    in_specs = [pl.BlockSpec((tm, x.shape[1]), lambda i, j: (i, 0)),
                pl.BlockSpec((None,) * len(w_lead) + (kdim, tn), lambda i, j: (*w_lead, 0, j))]
    args = [x, w]
    scratch = []
    if norm is not None:
        norm_g, norm_mod, k_shift, k_scale = norm
        nseg_of = lambda i: jnp.minimum((i * tm) // seq, nseg)
        in_specs += [pl.BlockSpec((1, kdim), lambda i, j: (0, 0)),
                     pl.BlockSpec((1, 1, kdim), lambda i, j: (nseg_of(i) * 6 + k_shift, 0, 0)),
                     pl.BlockSpec((1, 1, kdim), lambda i, j: (nseg_of(i) * 6 + k_scale, 0, 0))]
        args += [norm_g.reshape(1, kdim), norm_mod, norm_mod]
        scratch = [pltpu.VMEM((tm, kdim), BF16)]
    if bias is not None:
        in_specs.append(pl.BlockSpec((1, tn), lambda i, j: (0, j)))
        args.append(bias.reshape(1, n))
    if res is not None:
        seg = lambda i: jnp.minimum((i * tm) // seq, nseg)
        in_specs += [pl.BlockSpec((tm, tn), lambda i, j: (i, j)),
                     pl.BlockSpec((1, 1, tn), lambda i, j: (seg(i) * 6 + k_gate, 0, j))]
        args += [res, mod]
    return pl.pallas_call(
        functools.partial(_matmul_kernel, act=act, has_bias=bias is not None, has_res=res is not None,
                          has_norm=norm is not None),
        grid=(_div(m, tm), _div(n, tn)),
        in_specs=in_specs,
        out_specs=pl.BlockSpec((tm, tn), lambda i, j: (i, j)),
        out_shape=jax.ShapeDtypeStruct((m, n), out_dtype),
        scratch_shapes=scratch,
        compiler_params=_cparams("arbitrary", "arbitrary"),
        name="matmul",
    )(*args)


def _ffn1_kernel(h_ref, hp_ref, hn_ref, ng_ref, sh_ref, sc_ref, wg_ref, wu_ref, cw_ref, cb_ref, o_ref, u_s,
                 *, tm, n_lat_rows, seq_lat, seq_ctx):
    i = pl.program_id(0)
    hr = hp_ref.shape[0]

    @pl.when(pl.program_id(1) == 0)
    def _():
        gain = ng_ref[...] * (1.0 + sc_ref[0])
        shift = sh_ref[0]
        _norm_mod_rows(hp_ref, u_s, 0, hr, gain, shift)
        _norm_mod_rows(h_ref, u_s, hr, tm, gain, shift)
        _norm_mod_rows(hn_ref, u_s, hr + tm, hr, gain, shift)

    x = u_s[hr:hr + tm]
    g_ext = jnp.dot(u_s[...], wg_ref[...], preferred_element_type=F32)
    g = g_ext[hr:hr + tm]
    gp = g_ext[hr - SUBLANES:hr]
    gn = g_ext[hr + tm:hr + tm + SUBLANES]
    g_prev, g_next = _shift_rows_tile(g, gp, gn, i, tm, n_lat_rows, seq_lat, seq_ctx,
                                      interior=min(seq_lat, seq_ctx) < tm)
    gte = g_prev * cw_ref[0:1] + g * cw_ref[1:2] + g_next * cw_ref[2:3] + cb_ref[...]
    up = jnp.dot(x, wu_ref[...], preferred_element_type=F32)
    o_ref[...] = (_silu(gte) * up).astype(o_ref.dtype)


def _halo_specs(tm, kdim, m):
    hr = BF16_ROWS
    nb = tm // hr
    prev = pl.BlockSpec((hr, kdim), lambda i, j: (jnp.maximum(i * nb - 1, 0), 0))
    nxt = pl.BlockSpec((hr, kdim), lambda i, j: (jnp.minimum((i + 1) * nb, m // hr - 1), 0))
    return prev, nxt


def _cast_pad_kernel(w_ref, o_ref):
    n = w_ref.shape[1]
    o_ref[:, :n] = w_ref[...].astype(o_ref.dtype)
    o_ref[:, n:] = jnp.zeros((o_ref.shape[0], o_ref.shape[1] - n), o_ref.dtype)


def _cast_pad_cols(w, n_pad):
    layers, kdim, n = w.shape
    tr = 256
    return pl.pallas_call(
        _cast_pad_kernel,
        grid=(layers, _div(kdim, tr)),
        in_specs=[pl.BlockSpec((None, tr, n), lambda l, i: (l, i, 0))],
        out_specs=pl.BlockSpec((None, tr, n_pad), lambda l, i: (l, i, 0)),
        out_shape=jax.ShapeDtypeStruct((layers, kdim, n_pad), BF16),
        compiler_params=_cparams("arbitrary", "arbitrary"),
        name="cast_pad",
    )(w)


def _ffn1(h, norm_g, mod, wg, wu, cw, cb, layer, *, rows, nseg, n_lat_rows, seq_lat, seq_ctx):
    _, kdim, f = wg.shape
    tm, tn = 1024, 512
    assert n_lat_rows % tm == 0 and all(s % tm == 0 or tm % s == 0 for s in (seq_lat, seq_ctx))
    prev, nxt = _halo_specs(tm, kdim, rows)
    seg = lambda i: jnp.minimum((i * tm) // seq_lat, nseg)
    return pl.pallas_call(
        functools.partial(_ffn1_kernel, tm=tm, n_lat_rows=n_lat_rows, seq_lat=seq_lat, seq_ctx=seq_ctx),
        grid=(_div(rows, tm), _div(f, tn)),
        in_specs=[pl.BlockSpec((tm, kdim), lambda i, j: (i, 0)), prev, nxt,
                  pl.BlockSpec((1, kdim), lambda i, j: (0, 0)),
                  pl.BlockSpec((1, 1, kdim), lambda i, j: (seg(i) * 6 + 3, 0, 0)),
                  pl.BlockSpec((1, 1, kdim), lambda i, j: (seg(i) * 6 + 4, 0, 0)),
                  pl.BlockSpec((None, kdim, tn), lambda i, j: (layer, 0, j)),
                  pl.BlockSpec((None, kdim, tn), lambda i, j: (layer, 0, j)),
                  pl.BlockSpec((3, tn), lambda i, j: (0, j)),
                  pl.BlockSpec((1, tn), lambda i, j: (0, j))],
        out_specs=pl.BlockSpec((tm, tn), lambda i, j: (i, j)),
        out_shape=jax.ShapeDtypeStruct((rows, f), BF16),
        scratch_shapes=[pltpu.VMEM((tm + 2 * BF16_ROWS, kdim), BF16)],
        compiler_params=_cparams("arbitrary", "arbitrary"),
        name="ffn_gate_up",
    )(h, h, h, norm_g.reshape(1, kdim), mod, mod, wg, wu, cw, cb)


def _pool_kernel(u_ref, w_ref, b_ref, s_ref, h_ref, gate_ref, o_ref, xp_ref, *, seq):
    grp = pl.program_id(1)
    pad = SUBLANES
    zeros = jnp.zeros((pad, u_ref.shape[1]), F32)
    xp_ref[0:pad] = zeros
    xp_ref[pad:pad + seq] = u_ref[...]
    xp_ref[pad + seq:2 * pad + seq] = zeros
    rt = min(seq, 256)
    for gi, win in enumerate(POOL_WINDOWS):
        @pl.when(grp == gi)
        def _(win=win):
            left, right = win // 2, win - 1 - win // 2
            for rc in range(seq // rt):
                base = pad + rc * rt
                t = rc * rt + lax.broadcasted_iota(jnp.int32, (rt, 1), 0)
                acc = xp_ref[base - left:base - left + rt]
                for j in range(1, win):
                    acc = acc + xp_ref[base - left + j:base - left + j + rt]
                cnt = (jnp.minimum(t + right, seq - 1) - jnp.maximum(t - left, 0) + 1).astype(F32)
                p = acc / cnt - xp_ref[base:base + rt]
                y = jnp.dot(p.astype(BF16), w_ref[0], preferred_element_type=F32) + b_ref[...]
                rows = slice(rc * rt, (rc + 1) * rt)
                o_ref[rows] = h_ref[rows] + gate_ref[0] * (y * s_ref[...])


def _pool_mix(u, w, b, scale, h, mod, *, seq, row0, nseq, seg0):
    d = u.shape[1]
    cg = d // len(POOL_WINDOWS)
    blk0 = row0 // seq
    tile = pl.BlockSpec((seq, cg), lambda s, g: (blk0 + s, g))
    vec = pl.BlockSpec((1, cg), lambda s, g: (0, g))
    return pl.pallas_call(
        functools.partial(_pool_kernel, seq=seq),
        grid=(nseq, len(POOL_WINDOWS)),
        in_specs=[tile, pl.BlockSpec((1, cg, cg), lambda s, g: (g, 0, 0)), vec, vec, tile,
                  pl.BlockSpec((1, 1, cg), lambda s, g: ((seg0(s)) * 6 + 2, 0, g))],
        out_specs=tile,
        out_shape=jax.ShapeDtypeStruct(h.shape, F32),
        scratch_shapes=[pltpu.VMEM((seq + 2 * SUBLANES, cg), F32)],
        input_output_aliases={4: 0},
        compiler_params=_cparams("arbitrary", "arbitrary"),
        name="pool_mix",
    )(u, w, b.reshape(1, d), scale.reshape(1, d), h, mod)


def _na_lat_kernel(q_ref, k_ref, v_ref, qc_ref, kc_ref, vc_ref, cos_ref, sin_ref, bias_ref, o_ref, oc_ref,
                   q_s, k_s, *, rows, kh):
    hd = NA_HEAD_DIM
    lane = lax.broadcasted_iota(jnp.int32, (1, LANES), 1)
    low = (lane % 32) < 16
    head0 = lane < hd
    cos, sin = cos_ref[...], sin_ref[...]

    def rope(x):
        swapped = jnp.where(low, pltpu.roll(x, LANES - 16, axis=1), pltpu.roll(x, 16, axis=1))
        return x * cos + swapped * sin

    q_s[...] = (rope(q_ref[...].astype(F32)) * (hd ** -0.5)).astype(BF16)
    k_s[...] = rope(k_ref[...].astype(F32)).astype(BF16)
    nloc = kh * GRID_W
    rows_per_step = 8
    dot = lambda a, b: jnp.dot(a, b, preferred_element_type=F32)
    zero = jnp.zeros((), BF16)

    qc = qc_ref[...] * (hd ** -0.5)
    n_ctx = qc.shape[0]
    qcm = jnp.concatenate([jnp.where(head0, qc, zero), jnp.where(head0, zero, qc)], axis=0)
    sc = _nt(qcm, kc_ref[...])
    pc = jnp.exp(sc - jnp.max(sc, axis=-1, keepdims=True))
    oc = dot(pc.astype(BF16), vc_ref[...]) / jnp.sum(pc, axis=-1, keepdims=True)
    oc_ref[...] = jnp.where(head0, oc[:n_ctx], oc[n_ctx:]).astype(oc_ref.dtype)

    def body(step, carry):
        n = range(rows_per_step)
        r = [step * rows_per_step + dr for dr in n]
        rs = [jnp.clip(r[i] - kh // 2, 0, rows - kh) for i in n]
        q0 = [pl.multiple_of(r[i] * GRID_W, GRID_W) for i in n]
        k0 = [pl.multiple_of(rs[i] * GRID_W, GRID_W) for i in n]
        qr = [q_s[pl.ds(q0[i], GRID_W), :] for i in n]
        qm = [jnp.concatenate([jnp.where(head0, qr[i], zero), jnp.where(head0, zero, qr[i])], axis=0) for i in n]
        s_loc = [_nt(qm[i], k_s[pl.ds(k0[i], nloc), :]) + bias_ref[r[i] - rs[i]] for i in n]
        s_ctx = [_nt(qm[i], kc_ref[...]) for i in n]
        m = [jnp.maximum(jnp.max(s_loc[i], axis=-1, keepdims=True), jnp.max(s_ctx[i], axis=-1, keepdims=True))
             for i in n]
        p_loc = [jnp.exp(s_loc[i] - m[i]) for i in n]
        p_ctx = [jnp.exp(s_ctx[i] - m[i]) for i in n]
        den = [jnp.sum(p_loc[i], axis=-1, keepdims=True) + jnp.sum(p_ctx[i], axis=-1, keepdims=True) for i in n]
        o = [dot(p_loc[i].astype(BF16), v_ref[pl.ds(k0[i], nloc), :]) + dot(p_ctx[i].astype(BF16), vc_ref[...])
             for i in n]
        o = [o[i] / den[i] for i in n]
        for i in n:
            o_ref[pl.ds(q0[i], GRID_W), :] = jnp.where(head0, o[i][:GRID_W], o[i][GRID_W:]).astype(o_ref.dtype)
        return carry

    lax.fori_loop(0, _div(rows, rows_per_step), body, 0)


def _rope_tables(t_len, hd):
    half = hd // 2
    pos = jnp.arange(t_len)
    inv_freq = ROPE_BASE ** (-jnp.arange(0, half, 2, dtype=F32) / half)
    d = jnp.arange(hd)
    p = jnp.where(d[None, :] < half, (pos // GRID_W)[:, None], (pos % GRID_W)[:, None]).astype(F32)
    ang = p * inv_freq[d % (half // 2)][None, :]
    cos = jnp.cos(ang)
    sin = jnp.where((d % half) < half // 2, -jnp.sin(ang), jnp.sin(ang))
    return jnp.tile(cos, (1, 2)), jnp.tile(sin, (1, 2))


def _na_bias_table(rpb, kh):
    col = jnp.arange(GRID_W)
    col_start = jnp.clip(col - NA_KW // 2, 0, GRID_W - NA_KW)
    col_ok = (col[None, :] >= col_start[:, None]) & (col[None, :] < col_start[:, None] + NA_KW)
    dc = jnp.clip(col[None, :] - col[:, None], -(NA_KW - 1), NA_KW - 1) + NA_KW - 1
    heads = rpb.shape[0]
    rpb_cols = jnp.where(col_ok, rpb[:, :, dc], -1e30)
    rpb_cols = rpb_cols.reshape(heads // 2, 2, 2 * NA_KH - 1, GRID_W, GRID_W)
    per_shift = [rpb_cols[:, :, NA_KH - 1 - s:NA_KH - 1 - s + kh].transpose(0, 1, 3, 2, 4)
                 .reshape(heads // 2, 2 * GRID_W, kh * GRID_W) for s in range(kh)]
    return jnp.stack(per_shift, axis=1)


def _neighbourhood_attention(qkv, rpb, *, bsz, t_len, l_len):
    d = qkv.shape[1] // 3
    hd = NA_HEAD_DIM
    heads = d // hd
    npair = d // LANES
    rows = t_len // GRID_W
    kh = min(NA_KH, rows)
    n_lat = bsz * t_len
    n_all = qkv.shape[0]
    ctx0 = _div(n_lat, l_len)
    cos, sin = _rope_tables(t_len, hd)
    bias = _na_bias_table(rpb, kh)
    lat_blk = lambda off: pl.BlockSpec((t_len, LANES), lambda b, p: (b, off + p))
    ctx_blk = lambda off: pl.BlockSpec((l_len, LANES), lambda b, p: (ctx0 + b, off + p))
    tab_blk = pl.BlockSpec((t_len, LANES), lambda b, p: (0, 0))
    o_lat, o_ctx = pl.pallas_call(
        functools.partial(_na_lat_kernel, rows=rows, kh=kh),
        grid=(bsz, npair),
        in_specs=[lat_blk(0), lat_blk(npair), lat_blk(2 * npair), ctx_blk(0), ctx_blk(npair), ctx_blk(2 * npair),
                  tab_blk, tab_blk,
                  pl.BlockSpec((None, kh, 2 * GRID_W, kh * GRID_W), lambda b, p: (p, 0, 0, 0))],
        out_specs=[lat_blk(0), pl.BlockSpec((l_len, LANES), lambda b, p: (b, p))],
        out_shape=[jax.ShapeDtypeStruct((n_lat, d), BF16), jax.ShapeDtypeStruct((n_all - n_lat, d), BF16)],
        scratch_shapes=[pltpu.VMEM((t_len, LANES), BF16), pltpu.VMEM((t_len, LANES), BF16)],
        compiler_params=_cparams("arbitrary", "arbitrary"),
        name="na_attention",
    )(qkv, qkv, qkv, qkv, qkv, qkv, cos, sin, bias)
    return jnp.concatenate([o_lat, o_ctx], axis=0)


def _sg_kernel(z_u_ref, z_v_ref, g_ref, ws_ref, bs_ref, o_ref, *, nchunk):
    zv = z_v_ref[...].astype(F32)
    zv = zv * lax.rsqrt(jnp.mean(zv * zv, axis=-1, keepdims=True) + NORM_EPS) * g_ref[...]
    zvb = zv.astype(BF16)
    for c in range(nchunk):
        rs = slice(c * SG_CHUNK, (c + 1) * SG_CHUNK)
        for g in range(SG_GROUPS):
            ls = slice(g * LANES, (g + 1) * LANES)
            mixed = jnp.dot(ws_ref[g], zvb[rs, ls], preferred_element_type=F32) + bs_ref[:, ls]
            o_ref[rs, ls] = (z_u_ref[rs, ls].astype(F32) * mixed).astype(o_ref.dtype)


def _spatial_gate(z, norm_g, w_s, b_s):
    m, two_w = z.shape
    width = two_w // 2
    nchunk = 2
    tm = nchunk * SG_CHUNK
    bs_full = jnp.repeat(b_s.T, width // SG_GROUPS, axis=1)
    return pl.pallas_call(
        functools.partial(_sg_kernel, nchunk=nchunk),
        grid=(_div(m, tm),),
        in_specs=[pl.BlockSpec((tm, width), lambda i: (i, 0)),
                  pl.BlockSpec((tm, width), lambda i: (i, 1)),
                  pl.BlockSpec((1, width), lambda i: (0, 0)),
                  pl.BlockSpec((SG_GROUPS, SG_CHUNK, SG_CHUNK), lambda i: (0, 0, 0)),
                  pl.BlockSpec((SG_CHUNK, width), lambda i: (0, 0))],
        out_specs=pl.BlockSpec((tm, width), lambda i: (i, 0)),
        out_shape=jax.ShapeDtypeStruct((m, width), BF16),
        compiler_params=_cparams("arbitrary"),
        name="spatial_gate",
    )(z, z, norm_g.reshape(1, width), w_s.astype(BF16), bs_full)


def _shift_kernel(h_ref, hp_ref, hn_ref, ng_ref, sh_ref, sc_ref, mu_ref, *o_refs, tm, n_lat_rows, seq_lat,
                  seq_ctx):
    i = pl.program_id(0)
    o_refs, u_s = o_refs[:-1], o_refs[-1]
    gain = ng_ref[...] * (1.0 + sc_ref[0])
    shift = sh_ref[0]
    pad = SUBLANES
    _norm_mod_rows(hp_ref, u_s, 0, pad, gain, shift)
    _norm_mod_rows(h_ref, u_s, pad, tm, gain, shift)
    _norm_mod_rows(hn_ref, u_s, pad + tm, pad, gain, shift)
    nmix = len(o_refs)
    step, width = 64, 2 * LANES
    for c0 in range(0, u_s.shape[1], width):
        cols = slice(c0, c0 + width)
        mu = [mu_ref[n:n + 1, cols] for n in range(2 * nmix)]
        for r0 in range(0, tm, step):
            row = lax.broadcasted_iota(jnp.int32, (step, 1), 0)
            grow = i * tm + r0 + row
            slen = jnp.where(grow < n_lat_rows, seq_lat, seq_ctx)
            pos = grow & (slen - 1)
            u = u_s[pad + r0:pad + r0 + step, cols]
            prev = jnp.where(pos == 0, 0.0, u_s[pad + r0 - 1:pad + r0 - 1 + step, cols]) - u
            nxt = jnp.where(pos == slen - 1, 0.0, u_s[pad + r0 + 1:pad + r0 + 1 + step, cols]) - u
            for n in range(nmix):
                o_refs[n][r0:r0 + step, cols] = (u + prev * mu[n] + nxt * mu[nmix + n]).astype(BF16)


def _token_shift(h, norm_g, mod, mu, *, nseg, n_lat_rows, seq_lat, seq_ctx):
    m, d = h.shape
    nmix = mu.shape[1]
    tm = 256
    nb = tm // SUBLANES
    tile = pl.BlockSpec((tm, d), lambda i: (i, 0))
    prev = pl.BlockSpec((SUBLANES, d), lambda i: (jnp.maximum(i * nb - 1, 0), 0))
    nxt = pl.BlockSpec((SUBLANES, d), lambda i: (jnp.minimum((i + 1) * nb, m // SUBLANES - 1), 0))
    seg = lambda i: jnp.minimum((i * tm) // seq_lat, nseg)
    return pl.pallas_call(
        functools.partial(_shift_kernel, tm=tm, n_lat_rows=n_lat_rows, seq_lat=seq_lat, seq_ctx=seq_ctx),
        grid=(_div(m, tm),),
        in_specs=[tile, prev, nxt, pl.BlockSpec((1, d), lambda i: (0, 0)),
                  pl.BlockSpec((1, 1, d), lambda i: (seg(i) * 6 + 0, 0, 0)),
                  pl.BlockSpec((1, 1, d), lambda i: (seg(i) * 6 + 1, 0, 0)),
                  pl.BlockSpec((2 * nmix, d), lambda i: (0, 0))],
        out_specs=[tile] * nmix,
        out_shape=[jax.ShapeDtypeStruct((m, d), BF16)] * nmix,
        scratch_shapes=[pltpu.VMEM((tm + 2 * SUBLANES, d), F32)],
        compiler_params=_cparams("arbitrary"),
        name="token_shift",
    )(h, h, h, norm_g.reshape(1, d), mod, mod, mu.reshape(2 * nmix, d))


def _split2(x):
    hi = x.astype(BF16)
    lo = (x - hi.astype(F32)).astype(BF16)
    return hi, lo


def _chunk_chains(dirs, v_ref, *, nheads):
    c = SCAN_CHUNK
    hd = RW_HEAD_DIM
    ri = lax.broadcasted_iota(jnp.int32, (c, c), 0)
    ci = lax.broadcasted_iota(jnp.int32, (c, c), 1)
    strict_d = [(ci > ri) if d[-1] else (ci < ri) for d in dirs]
    incl_d = [(ci >= ri) if d[-1] else (ci <= ri) for d in dirs]
    diag_blk = (ri // SOLVE_BLOCK) == (ci // SOLVE_BLOCK)
    eye = jnp.where(ri == ci, 1.0, 0.0)
    chains = [(di, half, hh) for di in range(len(dirs)) for half in range(LANES // c) for hh in range(nheads)]
    src = [dirs[di] for di, _, _ in chains]
    strict = [strict_d[di] for di, _, _ in chains]
    incl = [incl_d[di] for di, _, _ in chains]
    rows = [slice(half * c, (half + 1) * c) for _, half, _ in chains]
    ls = [slice(hh * hd, (hh + 1) * hd) for _, _, hh in chains]
    pair = [hh // 2 for _, _, hh in chains]
    sub = [slice((hh % 2) * hd, (hh % 2 + 1) * hd) for _, _, hh in chains]
    n = range(len(chains))
    dot = lambda a, b: jnp.dot(a, b, preferred_element_type=F32)
    kt = [src[i][0][rows[i], ls[i]] for i in n]
    rt = [src[i][1][rows[i], ls[i]] for i in n]
    vv = [v_ref[rows[i], ls[i]] for i in n]
    ktrt = [jnp.concatenate([kt[i], rt[i]], axis=0) for i in n]
    pb = [_nt(ktrt[i], src[i][2][rows[i], ls[i]]) for i in n]
    pk = [_nt(ktrt[i], src[i][3][rows[i], ls[i]]) for i in n]
    a1 = [jnp.where(strict[i], pb[i][:c], 0.0) for i in n]
    a3 = [jnp.where(incl[i], pb[i][c:], 0.0).astype(BF16) for i in n]
    ad = [jnp.where(diag_blk, a1[i], 0.0) for i in n]
    ao = [(a1[i] - ad[i]).astype(BF16) for i in n]
    tinv = [eye - ad[i] for i in n]
    pw = [_mm(ad[i], ad[i]) for i in n]
    lhs_v = [jnp.concatenate([jnp.where(strict[i], pk[i][:c], 0.0).astype(BF16),
                              jnp.where(incl[i], pk[i][c:], 0.0).astype(BF16),
                              src[i][5][pair[i], sub[i], rows[i]]], axis=0) for i in n]
    pv = [dot(lhs_v[i], vv[i]) for i in n]
    tinv = [tinv[i] + _mm(tinv[i], pw[i]) for i in n]
    for _ in range(int(math.log2(SOLVE_BLOCK)) - 2):
        pw = [_mm(pw[i], pw[i]) for i in n]
        tinv = [tinv[i] + _mm(tinv[i], pw[i]) for i in n]
    rhs = [jnp.concatenate([kt[i].astype(F32), pv[i][:c]], axis=1).astype(BF16) for i in n]
    tinv = [t.astype(BF16) for t in tinv]
    nmat = [dot(tinv[i], ao[i]).astype(BF16) for i in n]
    x1 = [dot(tinv[i], rhs[i]) for i in n]
    x = x1
    for _ in range(c // SOLVE_BLOCK - 1):
        x = [x1[i] - dot(nmat[i], x[i].astype(BF16)) for i in n]
    lhs_x = [jnp.concatenate([src[i][4][pair[i], sub[i], rows[i]], a3[i]], axis=0) for i in n]
    px = [dot(lhs_x[i], x[i].astype(BF16)) for i in n]
    for i, (_, half, hh) in enumerate(chains):
        m_o, c_o, rp_o, y0_o = src[i][6]
        m_o[hh, 0, :, rows[i]] = (-px[i][:c, :hd]).astype(m_o.dtype)
        c_o[hh, 0, :, rows[i]] = pv[i][2 * c:] - px[i][:c, hd:]
        rp_o[rows[i], ls[i]] = (rt[i].astype(F32) - px[i][c:, :hd]).astype(rp_o.dtype)
        y0_o[rows[i], ls[i]] = pv[i][c:2 * c] - px[i][c:, hd:]


def _rw_chunk_kernel(k_ref, r_ref, v_ref, hw_ref, ha_ref, w2_ref, a2_ref, w0_ref, a0_ref, kk_ref, ka_ref,
                     rk_ref, tri_ref, bc_o, mf_o, cf_o, rpf_o, y0f_o, gf_o, mr_o, cr_o, rpr_o, y0r_o, gr_o,
                     kt_s, rt_s, bh_s, kh_s, v_s, bt_s, kbt_s, *, lora, nheads):
    npair = nheads // 2
    lane = lax.broadcasted_iota(jnp.int32, (1, LANES), 1)
    head0 = lane < RW_HEAD_DIM
    pairs = [slice(p * LANES, (p + 1) * LANES) for p in range(npair)]
    head_sum = lambda x: jnp.concatenate([_head_sum(x[:, ps], head0) for ps in pairs], axis=1)
    k = k_ref[...]
    r = r_ref[...]
    v_s[...] = v_ref[...].astype(BF16)
    kkv = k * kk_ref[...]
    kk = kkv * lax.rsqrt(jnp.maximum(head_sum(kkv * kkv), 1e-12))
    ka = ka_ref[...]
    rk = rk_ref[...]
    hw = hw_ref[...]
    ha = ha_ref[...]
    log_decay, beta, key = [], [], []
    for d in range(2):
        ls = slice(d * lora, (d + 1) * lora)
        pre = w0_ref[d] + jnp.dot(hw[:, ls], w2_ref[d], preferred_element_type=F32)
        z = -pre
        softplus = jnp.maximum(z, 0.0) + jnp.log(1.0 + jnp.exp(-jnp.abs(z)))
        log_decay.append(-jnp.exp(-softplus - 0.5))
        a = jax.nn.sigmoid(a0_ref[d] + jnp.dot(ha[:, ls], a2_ref[d], preferred_element_type=F32))
        beta.append(kk * a)
        key.append(k * (1.0 + (a - 1.0) * ka))
    bc_o[...] = head_sum(r * key[0] * rk) + head_sum(r * key[1] * rk)
    g_o = (gf_o, gr_o)
    for d in range(2):
        ld = log_decay[d]
        pieces = _split2(ld)
        csum = [sum(jnp.dot(tri_ref[e], p, preferred_element_type=F32) for p in pieces) for e in range(2)]
        cum = csum[d]
        total = csum[0] + csum[1] - ld
        kt_s[d] = (kk * jnp.exp(cum - ld)).astype(BF16)
        rt_s[d] = (r * jnp.exp(cum)).astype(BF16)
        inv = jnp.exp(-cum)
        bh_s[d] = (beta[d] * inv).astype(BF16)
        kh_s[d] = (key[d] * inv).astype(BF16)
        tail = jnp.exp(total - cum)
        bbar = beta[d] * tail
        kbar = key[d] * tail
        gdec = jnp.exp(total)
        for p, ps in enumerate(pairs):
            bt_s[d, p] = jnp.transpose(bbar[:, ps]).astype(BF16)
            kbt_s[d, p] = jnp.transpose(kbar[:, ps]).astype(BF16)
            g_o[d][p, 0] = jnp.transpose(gdec[:, ps])
    outs = ((mf_o, cf_o, rpf_o, y0f_o), (mr_o, cr_o, rpr_o, y0r_o))
    _chunk_chains([(kt_s.at[d], rt_s.at[d], bh_s.at[d], kh_s.at[d], bt_s.at[d], kbt_s.at[d], outs[d], d == 1)
                   for d in range(2)], v_s, nheads=nheads)


def _rw_chunks(k, r, v, hw, ha, w2, a2, w0, a0, k_k, k_a, r_k, *, bsz, t_len, l_len, lora):
    m, d = k.shape
    hd = RW_HEAD_DIM
    nheads = 8
    width = nheads * hd
    npair = nheads // 2
    tt = t_len + l_len
    ng = _div(tt, LANES)
    ctx_groups = _div(l_len, LANES)
    ctx0 = _div(bsz * t_len, LANES)

    def in_blk(b, g):
        return jnp.where(g < ctx_groups, ctx0 + b * ctx_groups + g,
                         b * (t_len // LANES) + g - ctx_groups)

    ri = jnp.arange(LANES)[:, None]
    ci = jnp.arange(LANES)[None, :]
    same = (ri // SCAN_CHUNK) == (ci // SCAN_CHUNK)
    tri = jnp.stack([same & (ci <= ri), same & (ci >= ri)]).astype(BF16)
    tile = pl.BlockSpec((LANES, width), lambda b, p, g: (in_blk(b, g), p))
    lora_blk = pl.BlockSpec((LANES, 2 * lora), lambda b, p, g: (in_blk(b, g), 0))
    wl_blk = pl.BlockSpec((2, lora, width), lambda b, p, g: (0, 0, p))
    v2_blk = pl.BlockSpec((2, 1, width), lambda b, p, g: (0, 0, p))
    v1_blk = pl.BlockSpec((1, width), lambda b, p, g: (0, p))
    tri_blk = pl.BlockSpec((2, LANES, LANES), lambda b, p, g: (0, 0, 0))
    row_o = pl.BlockSpec((None, LANES, width), lambda b, p, g: (b, g, p))
    sq_o = pl.BlockSpec((None, nheads, 1, hd, LANES), lambda b, p, g: (b, p, g, 0, 0))
    tr_o = pl.BlockSpec((None, npair, 1, LANES, LANES), lambda b, p, g: (b, p, g, 0, 0))
    row_shape = lambda dt: jax.ShapeDtypeStruct((bsz, tt, d), dt)
    sq_shape = lambda dt: jax.ShapeDtypeStruct((bsz, d // hd, ng, hd, LANES), dt)
    tr_shape = jax.ShapeDtypeStruct((bsz, d // LANES, ng, LANES, LANES), F32)
    dir_specs = [sq_o, sq_o, row_o, row_o, tr_o]
    dir_shapes = [sq_shape(BF16), sq_shape(F32), row_shape(BF16), row_shape(F32), tr_shape]
    operand = lambda: pltpu.VMEM((2, LANES, width), BF16)
    transposed = lambda: pltpu.VMEM((2, npair, LANES, LANES), BF16)
    out = pl.pallas_call(
        functools.partial(_rw_chunk_kernel, lora=lora, nheads=nheads),
        grid=(bsz, _div(d, width), ng),
        in_specs=[tile, tile, tile, lora_blk, lora_blk, wl_blk, wl_blk, v2_blk, v2_blk, v1_blk, v1_blk, v1_blk,
                  tri_blk],
        out_specs=[row_o] + dir_specs + dir_specs,
        out_shape=[row_shape(F32)] + dir_shapes + dir_shapes,
        scratch_shapes=[operand(), operand(), operand(), operand(), pltpu.VMEM((LANES, width), BF16),
                        transposed(), transposed()],
        compiler_params=_cparams("arbitrary", "arbitrary", "arbitrary"),
        name="rwkv_chunks",
    )(k, r, v, hw, ha, w2, a2, w0.reshape(2, 1, d), a0.reshape(2, 1, d), k_k.reshape(1, d),
      k_a.reshape(1, d), r_k.reshape(1, d), tri)
    return out[0], out[1:6], out[6:11]


def _state_pass_kernel(mf_ref, cf_ref, gf_ref, mr_ref, cr_ref, gr_ref, sf_o, sr_o, *, ctx_groups, n_groups,
                       nheads):
    c = SCAN_CHUNK
    hd = RW_HEAD_DIM
    heads = range(nheads)
    sf_o[...] = jnp.zeros(sf_o.shape, sf_o.dtype)
    sr_o[...] = jnp.zeros(sr_o.shape, sr_o.dtype)
    dirs = ((mf_ref, cf_ref, gf_ref, sf_o, False), (mr_ref, cr_ref, gr_ref, sr_o, True))

    def group(i, carry):
        out = []
        g_of = [i, jnp.where(i < ctx_groups, ctx_groups - 1 - i, n_groups - 1 - (i - ctx_groups))]
        states = [list(carry[0]), list(carry[1])]
        for step in range(LANES // c):
            s0b, ms = [[], []], [[], []]
            for di, (m_ref, c_ref, g_ref, s_o, rev) in enumerate(dirs):
                half = (LANES // c - 1 - step) if rev else step
                ts = slice(half * c, (half + 1) * c)
                for hh in heads:
                    sb = states[di][hh].astype(BF16)
                    s0b[di].append(sb)
                    blk = slice((hh % 2) * hd, (hh % 2 + 1) * hd)
                    s_o[hh // 2, g_of[di] * (LANES // c) + half, blk, blk] = sb
            for di, (m_ref, c_ref, g_ref, s_o, rev) in enumerate(dirs):
                half = (LANES // c - 1 - step) if rev else step
                ts = slice(half * c, (half + 1) * c)
                ms[di] = [jnp.dot(m_ref[hh, g_of[di], :, ts], s0b[di][hh], preferred_element_type=F32)
                          for hh in heads]
            for di, (m_ref, c_ref, g_ref, s_o, rev) in enumerate(dirs):
                half = (LANES // c - 1 - step) if rev else step
                ts = slice(half * c, (half + 1) * c)
                for hh in heads:
                    blk = slice((hh % 2) * hd, (hh % 2 + 1) * hd)
                    gc = g_ref[hh // 2, g_of[di], blk, half * c:half * c + 1]
                    states[di][hh] = gc * states[di][hh] + ms[di][hh] + c_ref[hh, g_of[di], :, ts]
        return tuple(states[0]), tuple(states[1])

    zero = tuple(jnp.zeros((hd, hd), F32) for _ in heads)
    lax.fori_loop(0, n_groups, group, (zero, zero))


def _state_pass(m_f, c_f, g_f, m_r, c_r, g_r, *, l_len):
    bsz, heads, ng, hd, _ = m_f.shape
    nheads = 4
    npair = nheads // 2
    nc = ng * (LANES // SCAN_CHUNK)
    sq = pl.BlockSpec((None, nheads, ng, hd, LANES), lambda b, p: (b, p, 0, 0, 0))
    tr = pl.BlockSpec((None, npair, ng, LANES, LANES), lambda b, p: (b, p, 0, 0, 0))
    st = pl.BlockSpec((None, npair, nc, LANES, LANES), lambda b, p: (b, p, 0, 0, 0))
    st_shape = jax.ShapeDtypeStruct((bsz, heads // 2, nc, LANES, LANES), BF16)
    return pl.pallas_call(
        functools.partial(_state_pass_kernel, ctx_groups=l_len // LANES, n_groups=ng, nheads=nheads),
        grid=(bsz, _div(heads, nheads)),
        in_specs=[sq, sq, tr, sq, sq, tr],
        out_specs=[st, st],
        out_shape=[st_shape, st_shape],
        compiler_params=_cparams("arbitrary", "arbitrary"),
        name="rwkv_state",
    )(m_f, c_f, g_f, m_r, c_r, g_r)


def _readout_kernel(rpf_ref, y0f_ref, sf_ref, rpr_ref, y0r_ref, sr_ref, bc_ref, v_ref, g_ref, lng_ref,
                    lnb_ref, o_ref):
    c = SCAN_CHUNK
    lane = lax.broadcasted_iota(jnp.int32, (1, LANES), 1)
    head0 = lane < RW_HEAD_DIM
    inv_n = 1.0 / RW_HEAD_DIM
    tiles = [(cc, pp) for cc in range(o_ref.shape[0] // c) for pp in range(o_ref.shape[1] // LANES)]
    rows = [slice(cc * c, (cc + 1) * c) for cc, _ in tiles]
    ls = [slice(pp * LANES, (pp + 1) * LANES) for _, pp in tiles]
    n = range(len(tiles))
    dot = lambda a, b: jnp.dot(a, b, preferred_element_type=F32)
    yf = [dot(rpf_ref[rows[i], ls[i]], sf_ref[tiles[i][1], tiles[i][0]]) for i in n]
    yr = [dot(rpr_ref[rows[i], ls[i]], sr_ref[tiles[i][1], tiles[i][0]]) for i in n]
    for i in n:
        y = (yf[i] + y0f_ref[rows[i], ls[i]]) + (yr[i] + y0r_ref[rows[i], ls[i]])
        mean = _head_sum(y, head0) * inv_n
        yc = y - mean
        var = _head_sum(yc * yc, head0) * inv_n
        yn = yc * lax.rsqrt(var + RW_GN_EPS) * lng_ref[:, ls[i]] + lnb_ref[:, ls[i]]
        o_ref[rows[i], ls[i]] = ((yn + bc_ref[rows[i], ls[i]] * v_ref[rows[i], ls[i]])
                                 * g_ref[rows[i], ls[i]]).astype(o_ref.dtype)


def _rw_readout(rp_f, y0_f, s_f, rp_r, y0_r, s_r, bc, v, g, ln_g, ln_b, *, bsz, t_len, l_len):
    d = v.shape[1]
    tm, tn = 256, 512
    per_b = t_len // tm
    off = l_len // tm
    blk = lambda i: (i // per_b, off + i % per_b)
    scan_blk = pl.BlockSpec((None, tm, tn), lambda i, p: (*blk(i), p))
    st_blk = pl.BlockSpec((None, tn // LANES, tm // SCAN_CHUNK, LANES, LANES),
                          lambda i, p: (blk(i)[0], p, blk(i)[1], 0, 0))
    tile = pl.BlockSpec((tm, tn), lambda i, p: (i, p))
    vec = pl.BlockSpec((1, tn), lambda i, p: (0, p))
    return pl.pallas_call(
        _readout_kernel,
        grid=(_div(bsz * t_len, tm), _div(d, tn)),
        in_specs=[scan_blk, scan_blk, st_blk, scan_blk, scan_blk, st_blk, scan_blk, tile, tile, vec, vec],
        out_specs=tile,
        out_shape=jax.ShapeDtypeStruct((bsz * t_len, d), BF16),
        compiler_params=_cparams("arbitrary", "arbitrary"),
        name="rwkv_readout",
    )(rp_f, y0_f, s_f, rp_r, y0_r, s_r, bc, v, g, ln_g.reshape(1, d), ln_b.reshape(1, d))


def _pad_cols(w, n):
    return jnp.pad(w, ((0, 0), (0, n - w.shape[1])))


def _pad_rows(w, n):
    return jnp.pad(w, ((0, n - w.shape[0]), (0, 0)))


def kernel(x, c, ctx, c_ctx, norm1_g, norm2_g, w_mod, b_mod, ffn_w_gate, ffn_w_up, ffn_conv_w, ffn_conv_b, ffn_w_down, final_norm_g, pool_w, pool_b, pool_scale, na_w_qkv, na_rpb, na_w_o, sg_w_in, sg_b_in, sg_norm_g, sg_w_s, sg_b_s, sg_w_o, rw_mu, rw_w_rkv, rw_w0, rw_w1, rw_w2, rw_a0, rw_a1, rw_a2, rw_g1, rw_g2, rw_k_k, rw_k_a, rw_r_k, rw_ln_g, rw_ln_b, rw_w_o):
    bsz, t_len, d = x.shape
    l_len = ctx.shape[1]
    depth = norm1_g.shape[0]
    n_mixers = 4
    n_lat = bsz * t_len
    n_all = n_lat + bsz * l_len
    f = ffn_w_gate.shape[2]
    f_pad = -(-f // 512) * 512
    wg_all = _cast_pad_cols(ffn_w_gate, f_pad)
    wu_all = _cast_pad_cols(ffn_w_up, f_pad)
    wd_all = ffn_w_down.astype(BF16)

    cvec = jnp.concatenate([c, c_ctx[None], jnp.zeros((SUBLANES - bsz - 1, d), F32)], axis=0)
    mods = _modulation(cvec, w_mod, b_mod)
    h = _pack_rows(x, ctx)
    seg = dict(seq=t_len, nseg=bsz)
    edges = dict(n_lat_rows=n_lat, seq_lat=t_len, seq_ctx=l_len)

    for i in range(depth):
        m_kind, j = i % n_mixers, i // n_mixers
        last = i == depth - 1
        mod = mods[i].reshape(SUBLANES * 6, 1, d)
        rows = n_lat if last else n_all
        if m_kind == 0:
            u = _norm(h, norm1_g[i], mod, 0, 1, t_len, bsz, F32, rows=rows)
            pw = pool_w[j].astype(BF16)
            h = _pool_mix(u, pw, pool_b[j], pool_scale[j], h, mod, seq=t_len, row0=0, nseq=bsz,
                          seg0=lambda s: s)
            if not last:
                h = _pool_mix(u, pw, pool_b[j], pool_scale[j], h, mod, seq=l_len, row0=n_lat, nseq=bsz,
                              seg0=lambda s: bsz)
        elif m_kind == 1:
            qkv = _matmul(h, na_w_qkv[j].astype(BF16), out_dtype=BF16, norm=(norm1_g[i], mod, 0, 1), **seg)
            o = _neighbourhood_attention(qkv, na_rpb[j], bsz=bsz, t_len=t_len, l_len=l_len)
            h = _matmul(o, na_w_o[j].astype(BF16), res=h, mod=mod, k_gate=2, rows=rows, **seg)
        elif m_kind == 2:
            z = _matmul(h, sg_w_in[j].astype(BF16), bias=sg_b_in[j], act="gelu", out_dtype=BF16, rows=rows,
                        norm=(norm1_g[i], mod, 0, 1), **seg)
            gated = _spatial_gate(z, sg_norm_g[j], sg_w_s[j], sg_b_s[j])
            h = _matmul(gated, sg_w_o[j].astype(BF16), res=h, mod=mod, k_gate=2, rows=rows, **seg)
        else:
            lora = LANES
            xr, xw, xk, xv, xa, xg = _token_shift(h, norm1_g[i], mod, rw_mu[j], nseg=bsz, **edges)
            w_rkv = rw_w_rkv.astype(BF16)
            r = _matmul(xr, w_rkv, w_lead=(j, 0))
            k = _matmul(xk, w_rkv, w_lead=(j, 1))
            v = _matmul(xv, w_rkv, w_lead=(j, 2))
            w1 = jnp.concatenate([_pad_cols(rw_w1[j, e], lora) for e in range(2)], axis=1).astype(BF16)
            a1 = jnp.concatenate([_pad_cols(rw_a1[j, e], lora) for e in range(2)], axis=1).astype(BF16)
            w2 = jnp.stack([_pad_rows(rw_w2[j, e], lora) for e in range(2)]).astype(BF16)
            a2 = jnp.stack([_pad_rows(rw_a2[j, e], lora) for e in range(2)]).astype(BF16)
            hw = _matmul(xw, w1, act="tanh", out_dtype=BF16)
            ha = _matmul(xa, a1, out_dtype=BF16)
            hg = _matmul(xg, rw_g1[j].astype(BF16), act="sigmoid", out_dtype=BF16, rows=n_lat)
            g = _matmul(hg, rw_g2[j].astype(BF16))
            bc, (m_f, c_f, rp_f, y0_f, g_f), (m_r, c_r, rp_r, y0_r, g_r) = _rw_chunks(
                k, r, v, hw, ha, w2, a2, rw_w0[j], rw_a0[j], rw_k_k[j], rw_k_a[j], rw_r_k[j],
                bsz=bsz, t_len=t_len, l_len=l_len, lora=lora)
            s_f, s_r = _state_pass(m_f, c_f, g_f, m_r, c_r, g_r, l_len=l_len)
            o = _rw_readout(rp_f, y0_f, s_f, rp_r, y0_r, s_r, bc, v, g, rw_ln_g[j], rw_ln_b[j],
                            bsz=bsz, t_len=t_len, l_len=l_len)
            h = _matmul(o, rw_w_o[j].astype(BF16), res=h, mod=mod, k_gate=2, rows=n_lat, **seg)
        cw = _pad_cols(ffn_conv_w[i], f_pad)
        cb = _pad_cols(ffn_conv_b[i][None], f_pad)
        mid = _ffn1(h, norm2_g[i], mod, wg_all, wu_all, cw, cb, i, rows=rows, nseg=bsz, **edges)
        h = _matmul(mid, wd_all, w_lead=(i,), res=h, mod=mod, k_gate=5, rows=rows, tm=1024, **seg)

    return _norm(h, final_norm_g, None, 0, 0, t_len, bsz, F32, rows=n_lat, batched_out=True)
```
